```python
import math
import jax
import jax.numpy as jnp
from jax import lax
import numpy as np

D_MODEL = 4096
BATCH = 2
SEQ = 8192
DEPTH = 2

CHUNK = 64
QBLOCK = 128
ROPE_THETA = 10000.0
NORM_EPS = 1e-6
FOX_HEADS = 8
FOX_HEAD_DIM = 128
DSA_HEADS = 8
DSA_HEAD_DIM = 128
IDX_HEADS = 32
IDX_HEAD_DIM = 64
INDEX_TOPK = 256
DIFF_HEADS = 8
DIFF_QK_DIM = 128
DIFF_V_DIM = 256
N_BRANCH = 3
D_FF = 11008
CONV_WIDTH = 3
FOX_WIDTH = FOX_HEADS * FOX_HEAD_DIM
DSA_WIDTH = DSA_HEADS * DSA_HEAD_DIM
DIFF_QK_WIDTH = DIFF_HEADS * 2 * DIFF_QK_DIM
DIFF_V_WIDTH = DIFF_HEADS * DIFF_V_DIM
IN_SIZES = (FOX_WIDTH, FOX_WIDTH, FOX_WIDTH, FOX_HEADS,
            DSA_WIDTH, DSA_WIDTH, DSA_WIDTH, IDX_HEADS * IDX_HEAD_DIM, IDX_HEAD_DIM, IDX_HEADS,
            DIFF_QK_WIDTH, DIFF_QK_WIDTH, DIFF_V_WIDTH,
            N_BRANCH * D_MODEL)
D_IN = sum(IN_SIZES)
IN_SPLITS = tuple(int(s) for s in np.cumsum(IN_SIZES)[:-1])

kernel_name = "hybrid_fox_dsa_diff_convffn_trunk"


def rms_norm(x, g):
    x32 = x.astype(jnp.float32)
    y = x32 * lax.rsqrt(jnp.mean(x32 * x32, axis=-1, keepdims=True) + NORM_EPS)
    return (y * g.astype(jnp.float32)).astype(x.dtype)


def rope_tables(S, d, dtype):
    pos = jnp.arange(S, dtype=jnp.float32)
    inv_freq = ROPE_THETA ** (-jnp.arange(0, d, 2, dtype=jnp.float32) / d)
    ang = pos[:, None] * inv_freq[None, :]
    return jnp.cos(ang).astype(dtype), jnp.sin(ang).astype(dtype)


def apply_rope(x, cos, sin):
    c, s = cos[:, None, :], sin[:, None, :]
    x1, x2 = jnp.split(x, 2, axis=-1)
    return jnp.concatenate([x1 * c - x2 * s, x2 * c + x1 * s], axis=-1)


def sweep_query_blocks(block_fn, S):
    out = lax.map(block_fn, jnp.arange(S // QBLOCK))
    out = jnp.moveaxis(out, 0, 1)
    return out.reshape((out.shape[0], S) + out.shape[3:])


def forgetting_attention(q, k, v, log_f):
    B, S, H, d = q.shape
    F = jnp.cumsum(log_f, axis=1).transpose(0, 2, 1)
    kpos = jnp.arange(S)

    def block(i):
        start = i * QBLOCK
        qb = lax.dynamic_slice_in_dim(q, start, QBLOCK, axis=1)
        Fq = lax.dynamic_slice_in_dim(F, start, QBLOCK, axis=2)
        qpos = start + jnp.arange(QBLOCK)
        logits = (jnp.einsum('bqhd,bkhd->bhqk', qb, k).astype(jnp.float32) * (d ** -0.5)
                  + Fq[..., None] - F[:, :, None, :])
        logits = jnp.where(qpos[:, None] >= kpos[None, :], logits, -jnp.inf)
        p = jax.nn.softmax(logits, axis=-1).astype(v.dtype)
        return jnp.einsum('bhqk,bkhd->bqhd', p, v)

    return sweep_query_blocks(block, S)


def indexed_sparse_attention(q, k, v, iq, ik, iw, topk):
    B, S, H, d = q.shape
    kchunk = jnp.arange(S) // CHUNK
    idx_scale = (IDX_HEAD_DIM ** -0.5) * (IDX_HEADS ** -0.5)
    gather = jax.vmap(lambda kb, ib: kb[ib])

    def block(i):
        start = i * QBLOCK
        qb = lax.dynamic_slice_in_dim(q, start, QBLOCK, axis=1)
        iqb = lax.dynamic_slice_in_dim(iq, start, QBLOCK, axis=1)
        iwb = lax.dynamic_slice_in_dim(iw, start, QBLOCK, axis=1)
        qchunk = (start + jnp.arange(QBLOCK)) // CHUNK
        rel = jax.nn.relu(jnp.einsum('bqhd,bkd->bqhk', iqb, ik).astype(jnp.float32))
        score = jnp.einsum('bqhk,bqh->bqk', rel, iwb.astype(jnp.float32)) * idx_scale
        score = jnp.where(qchunk[:, None] >= kchunk[None, :], score, -jnp.inf)
        _, sel = lax.top_k(score, topk)
        k_sel = gather(k, sel)
        v_sel = gather(v, sel)
        sel_ok = (sel // CHUNK) <= qchunk[None, :, None]
        logits = jnp.einsum('bqhd,bqkhd->bqhk', qb, k_sel).astype(jnp.float32) * (d ** -0.5)
        logits = jnp.where(sel_ok[:, :, None, :], logits, -jnp.inf)
        p = jax.nn.softmax(logits, axis=-1).astype(v.dtype)
        return jnp.einsum('bqhk,bqkhd->bqhd', p, v_sel)

    return sweep_query_blocks(block, S)


def differential_attention(q1, q2, k1, k2, v, lam):
    B, S, H, dq = q1.shape
    kchunk = jnp.arange(S) // CHUNK
    scale = dq ** -0.5

    def block(i):
        start = i * QBLOCK
        q1b = lax.dynamic_slice_in_dim(q1, start, QBLOCK, axis=1)
        q2b = lax.dynamic_slice_in_dim(q2, start, QBLOCK, axis=1)
        qchunk = (start + jnp.arange(QBLOCK)) // CHUNK
        mask = qchunk[:, None] >= kchunk[None, :]
        l1 = jnp.einsum('bqhd,bkhd->bhqk', q1b, k1).astype(jnp.float32) * scale
        l2 = jnp.einsum('bqhd,bkhd->bhqk', q2b, k2).astype(jnp.float32) * scale
        p1 = jax.nn.softmax(jnp.where(mask, l1, -jnp.inf), axis=-1)
        p2 = jax.nn.softmax(jnp.where(mask, l2, -jnp.inf), axis=-1)
        p = (p1 - lam * p2).astype(v.dtype)
        return jnp.einsum('bhqk,bkhd->bqhd', p, v)

    return sweep_query_blocks(block, S)


def conv_ffn(h, w_gate, w_up, conv_w, conv_b, w_down):
    S = h.shape[1]
    z = h @ w_gate
    zp = jnp.pad(z, ((0, 0), (CONV_WIDTH - 1, 0), (0, 0)))
    zc = sum((conv_w[j] * zp[:, j:j + S] for j in range(CONV_WIDTH)), conv_b)
    return (jax.nn.gelu(zc) * (h @ w_up)) @ w_down


def setup_inputs(seed: int = 0) -> dict:
    key = jax.random.key(seed)
    ks = jax.random.split(key, 22)
    f32 = jnp.float32

    def nrm(k, shape, scale):
        return jax.random.normal(k, shape, f32) * scale

    def gain(k, shape):
        return 1.0 + nrm(k, shape, 0.05)

    L, D = DEPTH, D_MODEL
    return {
        "x": nrm(ks[0], (BATCH, SEQ, D), 1.0),
        "g_mix_pre": gain(ks[1], (L, D)),
        "g_mix_post": gain(ks[2], (L, D)),
        "w_in": nrm(ks[3], (L, D, D_IN), D ** -0.5),
        "b_forget": 2.0 + nrm(ks[4], (L, FOX_HEADS), 0.5),
        "b_gate": nrm(ks[5], (L, N_BRANCH, D), 0.01),
        "lam_q1": nrm(ks[6], (L, DIFF_QK_DIM), 0.1),
        "lam_k1": nrm(ks[7], (L, DIFF_QK_DIM), 0.1),
        "lam_q2": nrm(ks[8], (L, DIFF_QK_DIM), 0.1),
        "lam_k2": nrm(ks[9], (L, DIFF_QK_DIM), 0.1),
        "g_subln": gain(ks[10], (L, DIFF_V_DIM)),
        "w_oa": nrm(ks[11], (L, FOX_WIDTH, D), FOX_WIDTH ** -0.5),
        "w_ob": nrm(ks[12], (L, DSA_WIDTH, D), DSA_WIDTH ** -0.5),
        "w_oc": nrm(ks[13], (L, DIFF_V_WIDTH, D), DIFF_V_WIDTH ** -0.5),
        "w_out": nrm(ks[14], (L, D, D), D ** -0.5),
        "g_ffn_pre": gain(ks[15], (L, D)),
        "g_ffn_post": gain(ks[16], (L, D)),
        "w_ffn_gate": nrm(ks[17], (L, D, D_FF), D ** -0.5),
        "w_ffn_up": nrm(ks[18], (L, D, D_FF), D ** -0.5),
        "conv_w": nrm(ks[19], (L, CONV_WIDTH, D_FF), CONV_WIDTH ** -0.5),
        "conv_b": nrm(ks[20], (L, D_FF), 0.01),
        "w_ffn_down": nrm(ks[21], (L, D_FF, D), D_FF ** -0.5),
    }


def reference(x, g_mix_pre, g_mix_post, w_in, b_forget, b_gate, lam_q1, lam_k1, lam_q2, lam_k2,
              g_subln, w_oa, w_ob, w_oc, w_out, g_ffn_pre, g_ffn_post, w_ffn_gate, w_ffn_up,
              conv_w, conv_b, w_ffn_down):
    B, S, D = x.shape
    f32 = jnp.float32
    topk = min(INDEX_TOPK, S // 4)
    cos_h, sin_h = rope_tables(S, DSA_HEAD_DIM, x.dtype)
    cos_i, sin_i = rope_tables(S, IDX_HEAD_DIM, x.dtype)

    for l in range(DEPTH):
        h = rms_norm(x, g_mix_pre[l])
        (qa, ka, va, fa, qb, kb, vb, iq, ik, iw, qc, kc, vc, gl) = jnp.split(
            h @ w_in[l], IN_SPLITS, axis=-1)

        log_f = jax.nn.log_sigmoid(fa.astype(f32) + b_forget[l].astype(f32))
        o_a = forgetting_attention(qa.reshape(B, S, FOX_HEADS, FOX_HEAD_DIM),
                                   ka.reshape(B, S, FOX_HEADS, FOX_HEAD_DIM),
                                   va.reshape(B, S, FOX_HEADS, FOX_HEAD_DIM), log_f)

        qb = apply_rope(qb.reshape(B, S, DSA_HEADS, DSA_HEAD_DIM), cos_h, sin_h)
        kb = apply_rope(kb.reshape(B, S, DSA_HEADS, DSA_HEAD_DIM), cos_h, sin_h)
        iq = apply_rope(iq.reshape(B, S, IDX_HEADS, IDX_HEAD_DIM), cos_i, sin_i)
        ik = apply_rope(ik.reshape(B, S, 1, IDX_HEAD_DIM), cos_i, sin_i)[:, :, 0]
        o_b = indexed_sparse_attention(qb, kb, vb.reshape(B, S, DSA_HEADS, DSA_HEAD_DIM),
                                       iq, ik, iw, topk)

        q1, q2 = jnp.split(qc.reshape(B, S, DIFF_HEADS, 2 * DIFF_QK_DIM), 2, axis=-1)
        k1, k2 = jnp.split(kc.reshape(B, S, DIFF_HEADS, 2 * DIFF_QK_DIM), 2, axis=-1)
        q1, q2 = apply_rope(q1, cos_h, sin_h), apply_rope(q2, cos_h, sin_h)
        k1, k2 = apply_rope(k1, cos_h, sin_h), apply_rope(k2, cos_h, sin_h)
        lam_init = 0.8 - 0.6 * math.exp(-0.3 * l)
        lam = (jnp.exp(jnp.sum(lam_q1[l].astype(f32) * lam_k1[l].astype(f32)))
               - jnp.exp(jnp.sum(lam_q2[l].astype(f32) * lam_k2[l].astype(f32))) + lam_init)
        o_c = differential_attention(q1, q2, k1, k2, vc.reshape(B, S, DIFF_HEADS, DIFF_V_DIM), lam)
        o_c = rms_norm(o_c, g_subln[l]) * (1.0 - lam_init)

        gates = jax.nn.sigmoid(gl.reshape(B, S, N_BRANCH, D) + b_gate[l])
        merged = (gates[:, :, 0] * (o_a.reshape(B, S, FOX_WIDTH) @ w_oa[l])
                  + gates[:, :, 1] * (o_b.reshape(B, S, DSA_WIDTH) @ w_ob[l])
                  + gates[:, :, 2] * (o_c.reshape(B, S, DIFF_V_WIDTH) @ w_oc[l]))
        x = x + rms_norm(merged @ w_out[l], g_mix_post[l])

        h = rms_norm(x, g_ffn_pre[l])
        y = conv_ffn(h, w_ffn_gate[l], w_ffn_up[l], conv_w[l], conv_b[l], w_ffn_down[l])
        x = x + rms_norm(y, g_ffn_post[l])
    return x
```

```python
import functools
import math

import jax
import jax.numpy as jnp
from jax import lax
from jax.experimental import pallas as pl
from jax.experimental.pallas import tpu as pltpu

CHUNK = 64
ROPE_THETA = 10000.0
NORM_EPS = 1e-6
HEAD_DIM = 128
IDX_HEAD_DIM = 64
DIFF_V_DIM = 256
INDEX_TOPK = 256
N_BRANCH = 3
CONV_WIDTH = 3

LANES = 128
VMEM_LIMIT_BYTES = 56 * 1024 * 1024
EPILOGUE_ROWS = 64
MASK_VALUE = -1e30
INT32_MIN = -(2 ** 31)

F32 = jnp.float32
BF16 = jnp.bfloat16
_NT = (((1,), (1,)), ((), ()))


def _params(*semantics):
    return pltpu.CompilerParams(dimension_semantics=semantics, vmem_limit_bytes=VMEM_LIMIT_BYTES)


def _tile(dim, want):
    t = min(dim, want)
    assert dim % t == 0, (dim, want)
    return t


def _rmsnorm_kernel(x_ref, g_ref, o_ref):
    x = x_ref[...]
    ms = jnp.mean(x * x, axis=-1, keepdims=True)
    o_ref[...] = (x * lax.rsqrt(ms + NORM_EPS) * g_ref[...]).astype(o_ref.dtype)


def _rmsnorm(x, g):
    m, d = x.shape
    bm = _tile(m, 512)
    return pl.pallas_call(
        _rmsnorm_kernel,
        grid=(m // bm,),
        in_specs=[pl.BlockSpec((bm, d), lambda i: (i, 0)), pl.BlockSpec((1, d), lambda i: (0, 0))],
        out_specs=pl.BlockSpec((bm, d), lambda i: (i, 0)),
        out_shape=jax.ShapeDtypeStruct((m, d), BF16),
        compiler_params=_params("parallel"),
    )(x, g.reshape(1, d))


def _rope128(x, cos, sin):
    return x * cos + pltpu.roll(x, HEAD_DIM // 2, 1) * sin


def _rope64(x, cos, sin_lo, sin_hi):
    return x * cos + pltpu.roll(x, LANES - 32, 1) * sin_lo + pltpu.roll(x, 32, 1) * sin_hi


def _inproj_kernel(a_ref, w_ref, c128_ref, s128_ref, c64_ref, slo_ref, shi_ref, o_ref, *,
                   q_tiles, rope128_tiles, rope64_tiles, q_scale):
    j = pl.program_id(1)

    def member(tiles):
        hit = j < 0
        for t in tiles:
            hit = hit | (j == t)
        return hit

    acc = jnp.dot(a_ref[...], w_ref[...], preferred_element_type=F32)
    acc = acc * jnp.where(member(q_tiles), q_scale, 1.0).astype(F32)
    is128 = member(rope128_tiles)
    is64 = member(rope64_tiles)
    n_sub = acc.shape[1] // LANES

    @pl.when(is128)
    def _():
        for c in range(n_sub):
            sl = slice(c * LANES, (c + 1) * LANES)
            o_ref[:, sl] = _rope128(acc[:, sl], c128_ref[...], s128_ref[...]).astype(o_ref.dtype)

    @pl.when(is64)
    def _():
        for c in range(n_sub):
            sl = slice(c * LANES, (c + 1) * LANES)
            o_ref[:, sl] = _rope64(acc[:, sl], c64_ref[...], slo_ref[...], shi_ref[...]).astype(o_ref.dtype)

    @pl.when(jnp.logical_not(is128 | is64))
    def _():
        o_ref[...] = acc.astype(o_ref.dtype)


def _inproj(h, w, tables, seq, *, bn, q_tiles, rope128_tiles, rope64_tiles, q_scale):
    m, k = h.shape
    n = w.shape[1]
    bm = _tile(seq, 1024)
    pos_blocks = seq // bm
    tab_spec = pl.BlockSpec((bm, LANES), lambda i, j: (i % pos_blocks, 0))
    kern = functools.partial(_inproj_kernel, q_tiles=q_tiles, rope128_tiles=rope128_tiles,
                             rope64_tiles=rope64_tiles, q_scale=q_scale)
    return pl.pallas_call(
        kern,
        grid=(m // bm, n // bn),
        in_specs=[pl.BlockSpec((bm, k), lambda i, j: (i, 0)),
                  pl.BlockSpec((k, bn), lambda i, j: (0, j))] + [tab_spec] * 5,
        out_specs=pl.BlockSpec((bm, bn), lambda i, j: (i, j)),
        out_shape=jax.ShapeDtypeStruct((m, n), BF16),
        compiler_params=_params("parallel", "arbitrary"),
    )(h, w, *tables)


def _inproj_small_kernel(a_ref, w_ref, c64_ref, slo_ref, shi_ref, o_ref):
    acc = jnp.dot(a_ref[...], w_ref[...], preferred_element_type=F32)
    roped = _rope64(acc, c64_ref[...], slo_ref[...], shi_ref[...])
    lane = lax.broadcasted_iota(jnp.int32, acc.shape, 1)
    o_ref[...] = jnp.where(lane < IDX_HEAD_DIM, roped, acc)


def _inproj_small(h, w, tables64, seq):
    m, k = h.shape
    bm = _tile(seq, 1024)
    pos_blocks = seq // bm
    tab_spec = pl.BlockSpec((bm, LANES), lambda i: (i % pos_blocks, 0))
    return pl.pallas_call(
        _inproj_small_kernel,
        grid=(m // bm,),
        in_specs=[pl.BlockSpec((bm, k), lambda i: (i, 0)),
                  pl.BlockSpec((k, LANES), lambda i: (0, 0))] + [tab_spec] * 3,
        out_specs=pl.BlockSpec((bm, LANES), lambda i: (i, 0)),
        out_shape=jax.ShapeDtypeStruct((m, LANES), F32),
        compiler_params=_params("parallel"),
    )(h, w, *tables64)


def _forget_cumsum_kernel(fa_ref, b_ref, o_ref):
    x = fa_ref[0] + b_ref[...]
    lf = jnp.minimum(x, 0.0) - jnp.log1p(jnp.exp(-jnp.abs(x)))
    row = lax.broadcasted_iota(jnp.int32, (LANES, LANES), 0)
    col = lax.broadcasted_iota(jnp.int32, (LANES, LANES), 1)
    tri = (row <= col).astype(BF16)
    carry = jnp.zeros((lf.shape[0], 1), F32)
    for c in range(lf.shape[1] // LANES):
        blk = lf[:, c * LANES:(c + 1) * LANES]
        hi = blk.astype(BF16)
        rem = blk - hi.astype(F32)
        mid = rem.astype(BF16)
        lo = (rem - mid.astype(F32)).astype(BF16)
        loc = (jnp.dot(hi, tri, preferred_element_type=F32)
               + jnp.dot(mid, tri, preferred_element_type=F32)
               + jnp.dot(lo, tri, preferred_element_type=F32))
        o_ref[0, :, c * LANES:(c + 1) * LANES] = loc + carry
        carry = carry + loc[:, LANES - 1:LANES]


def _forget_cumsum(fa_t, b_forget):
    b, hn, s = fa_t.shape
    return pl.pallas_call(
        _forget_cumsum_kernel,
        grid=(b,),
        in_specs=[pl.BlockSpec((1, hn, s), lambda i: (i, 0, 0)), pl.BlockSpec((hn, 1), lambda i: (0, 0))],
        out_specs=pl.BlockSpec((1, hn, s), lambda i: (i, 0, 0)),
        out_shape=jax.ShapeDtypeStruct((b, hn, s), F32),
        compiler_params=_params("parallel"),
    )(fa_t, b_forget.reshape(hn, 1).astype(F32))


def _online_update(s, v, m, l, acc):
    m_new = jnp.maximum(m, jnp.max(s, axis=1, keepdims=True))
    alpha = jnp.exp(m - m_new)
    p = jnp.exp(s - m_new)
    l_new = alpha * l + jnp.sum(p, axis=1, keepdims=True)
    acc_new = alpha * acc + jnp.dot(p.astype(v.dtype), v, preferred_element_type=F32)
    return m_new, l_new, acc_new


def _softmax_init(tq, dv):
    return (jnp.full((tq, 1), MASK_VALUE, F32), jnp.zeros((tq, 1), F32), jnp.zeros((tq, dv), F32))


def _fox_kernel(q_ref, k_ref, v_ref, fq_ref, fk_ref, o_ref, *, tq):
    i = pl.program_id(2)
    q = q_ref[0]
    fq = fq_ref[0, 0]

    def tile(j, carry, diagonal):
        off = pl.multiple_of(j * tq, tq)
        k = k_ref[0, pl.ds(off, tq), :]
        v = v_ref[0, pl.ds(off, tq), :]
        fk = fk_ref[0, 0, :, pl.ds(off, tq)]
        s = lax.dot_general(q, k, _NT, preferred_element_type=F32) + (fq - fk)
        if diagonal:
            row = lax.broadcasted_iota(jnp.int32, s.shape, 0)
            col = lax.broadcasted_iota(jnp.int32, s.shape, 1)
            s = jnp.where(row >= col, s, MASK_VALUE)
        return _online_update(s, v, *carry)

    carry = lax.fori_loop(0, i, lambda j, c: tile(j, c, False), _softmax_init(tq, HEAD_DIM))
    _, l, acc = tile(i, carry, True)
    o_ref[0] = (acc / l).astype(o_ref.dtype)


def _fox_attention(p3, f_cum, *, heads, q_off, k_off, v_off):
    b, s, _ = p3.shape
    tq = _tile(s, 256)
    fq = f_cum.reshape(b, heads, s, 1)
    fk = f_cum.reshape(b, heads, 1, s)
    return pl.pallas_call(
        functools.partial(_fox_kernel, tq=tq),
        grid=(b, heads, s // tq),
        in_specs=[pl.BlockSpec((1, tq, HEAD_DIM), lambda bi, h, i: (bi, i, q_off + h)),
                  pl.BlockSpec((1, s, HEAD_DIM), lambda bi, h, i: (bi, 0, k_off + h)),
                  pl.BlockSpec((1, s, HEAD_DIM), lambda bi, h, i: (bi, 0, v_off + h)),
                  pl.BlockSpec((1, 1, tq, 1), lambda bi, h, i: (bi, h, i, 0)),
                  pl.BlockSpec((1, 1, 1, s), lambda bi, h, i: (bi, h, 0, 0))],
        out_specs=pl.BlockSpec((1, tq, HEAD_DIM), lambda bi, h, i: (bi, i, h)),
        out_shape=jax.ShapeDtypeStruct((b, s, heads * HEAD_DIM), BF16),
        compiler_params=_params("parallel", "parallel", "arbitrary"),
    )(p3, p3, p3, fq, fk)


def _diff_kernel(q1_ref, q2_ref, k1_ref, k2_ref, v_ref, lq1_ref, lk1_ref, lq2_ref, lk2_ref, g_ref, o_ref, *,
                 tq, lam_init):
    i = pl.program_id(2)
    q1 = q1_ref[0]
    q2 = q2_ref[0]

    def tile(j, carry, diagonal):
        c1, c2 = carry
        off = pl.multiple_of(j * tq, tq)
        k1 = k1_ref[0, pl.ds(off, tq), :]
        k2 = k2_ref[0, pl.ds(off, tq), :]
        v = v_ref[0, pl.ds(off, tq), :]
        s1 = lax.dot_general(q1, k1, _NT, preferred_element_type=F32)
        s2 = lax.dot_general(q2, k2, _NT, preferred_element_type=F32)
        if diagonal:
            row = lax.broadcasted_iota(jnp.int32, s1.shape, 0) // CHUNK
            col = lax.broadcasted_iota(jnp.int32, s1.shape, 1) // CHUNK
            keep = row >= col
            s1 = jnp.where(keep, s1, MASK_VALUE)
            s2 = jnp.where(keep, s2, MASK_VALUE)
        return _online_update(s1, v, *c1), _online_update(s2, v, *c2)

    init = (_softmax_init(tq, DIFF_V_DIM), _softmax_init(tq, DIFF_V_DIM))
    carry = lax.fori_loop(0, i, lambda j, c: tile(j, c, False), init)
    (_, l1, a1), (_, l2, a2) = tile(i, carry, True)

    lam = (jnp.exp(jnp.sum(lq1_ref[...] * lk1_ref[...], axis=1, keepdims=True))
           - jnp.exp(jnp.sum(lq2_ref[...] * lk2_ref[...], axis=1, keepdims=True)) + lam_init)
    o = a1 / l1 - lam * (a2 / l2)
    ms = jnp.mean(o * o, axis=-1, keepdims=True)
    o = o * lax.rsqrt(ms + NORM_EPS) * g_ref[...]
    o_ref[0] = (o * (1.0 - lam_init)).astype(o_ref.dtype)


def _diff_attention(p3, lam_vecs, g_subln, *, heads, q_off, k_off, v_off256, lam_init):
    b, s, _ = p3.shape
    tq = _tile(s, 256)
    vec_spec = pl.BlockSpec((1, HEAD_DIM), lambda bi, h, i: (0, 0))
    return pl.pallas_call(
        functools.partial(_diff_kernel, tq=tq, lam_init=lam_init),
        grid=(b, heads, s // tq),
        in_specs=[pl.BlockSpec((1, tq, HEAD_DIM), lambda bi, h, i: (bi, i, q_off + 2 * h)),
                  pl.BlockSpec((1, tq, HEAD_DIM), lambda bi, h, i: (bi, i, q_off + 2 * h + 1)),
                  pl.BlockSpec((1, s, HEAD_DIM), lambda bi, h, i: (bi, 0, k_off + 2 * h)),
                  pl.BlockSpec((1, s, HEAD_DIM), lambda bi, h, i: (bi, 0, k_off + 2 * h + 1)),
                  pl.BlockSpec((1, s, DIFF_V_DIM), lambda bi, h, i: (bi, 0, v_off256 + h)),
                  vec_spec, vec_spec, vec_spec, vec_spec,
                  pl.BlockSpec((1, DIFF_V_DIM), lambda bi, h, i: (0, 0))],
        out_specs=pl.BlockSpec((1, tq, DIFF_V_DIM), lambda bi, h, i: (bi, i, h)),
        out_shape=jax.ShapeDtypeStruct((b, s, heads * DIFF_V_DIM), BF16),
        compiler_params=_params("parallel", "parallel", "arbitrary"),
    )(p3, p3, p3, p3, p3, *[v.reshape(1, HEAD_DIM).astype(F32) for v in lam_vecs],
      g_subln.reshape(1, DIFF_V_DIM).astype(F32))


def _ordered_key(x):
    bits = pltpu.bitcast(x, jnp.int32)
    return bits ^ (lax.shift_right_arithmetic(bits, 31) & 0x7FFFFFFF)


def _dsa_kernel(iq_ref, iw_ref, iklo_ref, ikhi_ref, q_ref, k_ref, v_ref, o_ref, keys_ref, *,
                tq, tk, idx_heads, heads, topk, idx_scale):
    i = pl.program_id(1)
    n_tiles = (i * tq) // tk + 1
    iw = iw_ref[0][:, IDX_HEAD_DIM:IDX_HEAD_DIM + idx_heads] * idx_scale
    row_chunk = (i * tq + lax.broadcasted_iota(jnp.int32, (tq, tk), 0)) // CHUNK
    col_iota = lax.broadcasted_iota(jnp.int32, (tq, tk), 1)

    def score_tile(j, _):
        off = pl.multiple_of(j * tk, tk)
        ik_lo = iklo_ref[0, pl.ds(off, tk), :]
        ik_hi = ikhi_ref[0, pl.ds(off, tk), :]
        sc = jnp.zeros((tq, tk), F32)
        for p in range(idx_heads // 2):
            a = iq_ref[0, :, p * LANES:(p + 1) * LANES]
            even = lax.dot_general(a, ik_lo, _NT, preferred_element_type=F32)
            odd = lax.dot_general(a, ik_hi, _NT, preferred_element_type=F32)
            sc = sc + jnp.maximum(even, 0.0) * iw[:, 2 * p:2 * p + 1]
            sc = sc + jnp.maximum(odd, 0.0) * iw[:, 2 * p + 1:2 * p + 2]
        valid = row_chunk >= (off + col_iota) // CHUNK
        keys_ref[:, pl.ds(off, tk)] = jnp.where(valid, _ordered_key(sc), INT32_MIN)
        return 0

    lax.fori_loop(0, n_tiles, score_tile, 0)

    def count_ge(cand):
        def body(j, part):
            off = pl.multiple_of(j * tk, tk)
            ge = (keys_ref[:, pl.ds(off, tk)] >= cand).astype(jnp.int32)
            for c in range(tk // LANES):
                part = part + ge[:, c * LANES:(c + 1) * LANES]
            return part
        part = lax.fori_loop(0, n_tiles, body, jnp.zeros((tq, LANES), jnp.int32))
        return jnp.sum(part, axis=1, keepdims=True)

    thr = jnp.where(count_ge(jnp.zeros((tq, 1), jnp.int32)) >= topk, 0, INT32_MIN).astype(jnp.int32)

    def bit_step(t, thr):
        cand = thr + lax.shift_left(jnp.int32(1), 30 - t)
        return jnp.where(count_ge(cand) >= topk, cand, thr)

    thr = lax.fori_loop(0, 31, bit_step, thr)
    thr = jnp.maximum(thr, INT32_MIN + 1)

    for h in range(heads):
        hs = slice(h * HEAD_DIM, (h + 1) * HEAD_DIM)
        q = q_ref[0, :, hs]

        def attend(j, carry, hs=hs, q=q):
            off = pl.multiple_of(j * tk, tk)
            k = k_ref[0, pl.ds(off, tk), hs]
            v = v_ref[0, pl.ds(off, tk), hs]
            s = lax.dot_general(q, k, _NT, preferred_element_type=F32)
            s = jnp.where(keys_ref[:, pl.ds(off, tk)] >= thr, s, MASK_VALUE)
            return _online_update(s, v, *carry)

        _, l, acc = lax.fori_loop(0, n_tiles, attend, _softmax_init(tq, HEAD_DIM))
        o_ref[0, :, hs] = (acc / l).astype(o_ref.dtype)


def _dsa_attention(p3, small3, ik_lo, ik_hi, *, heads, idx_heads, iq_off, q_off, k_off, v_off, topk):
    b, s, _ = p3.shape
    tq = _tile(s, 128)
    tk = _tile(s, 512)
    width = heads * HEAD_DIM
    iq_width = idx_heads * IDX_HEAD_DIM
    resident = dict(pipeline_mode=pl.Buffered(1))
    kern = functools.partial(_dsa_kernel, tq=tq, tk=tk, idx_heads=idx_heads, heads=heads, topk=topk,
                             idx_scale=(IDX_HEAD_DIM ** -0.5) * (idx_heads ** -0.5))
    return pl.pallas_call(
        kern,
        grid=(b, s // tq),
        in_specs=[pl.BlockSpec((1, tq, iq_width), lambda bi, i: (bi, i, iq_off)),
                  pl.BlockSpec((1, tq, LANES), lambda bi, i: (bi, i, 0)),
                  pl.BlockSpec((1, s, LANES), lambda bi, i: (bi, 0, 0), **resident),
                  pl.BlockSpec((1, s, LANES), lambda bi, i: (bi, 0, 0), **resident),
                  pl.BlockSpec((1, tq, width), lambda bi, i: (bi, i, q_off)),
                  pl.BlockSpec((1, s, width), lambda bi, i: (bi, 0, k_off), **resident),
                  pl.BlockSpec((1, s, width), lambda bi, i: (bi, 0, v_off), **resident)],
        out_specs=pl.BlockSpec((1, tq, width), lambda bi, i: (bi, i, 0)),
        out_shape=jax.ShapeDtypeStruct((b, s, width), BF16),
        scratch_shapes=[pltpu.VMEM((tq, s), jnp.int32)],
        compiler_params=_params("parallel", "arbitrary"),
    )(p3, small3, ik_lo, ik_hi, p3, p3, p3)


def _merge_kernel(oa_ref, ob_ref, oc_ref, wa_ref, wb_ref, wc_ref, g0_ref, g1_ref, g2_ref, bg_ref, o_ref):
    def branch(o_r, w_r, g_r, n):
        y = jnp.dot(o_r[...], w_r[...], preferred_element_type=F32)
        return jax.nn.sigmoid(g_r[...].astype(F32) + bg_ref[n:n + 1, :]) * y

    out = branch(oa_ref, wa_ref, g0_ref, 0) + branch(ob_ref, wb_ref, g1_ref, 1) + branch(oc_ref, wc_ref, g2_ref, 2)
    o_ref[...] = out.astype(o_ref.dtype)


def _merge(o_a, o_b, o_c, w_a, w_b, w_c, p, b_gate, *, gate_off, d):
    m = o_a.shape[0]
    bm = _tile(m, 512)
    bn = _tile(d, 1024)
    nj = d // bn

    def lhs(o):
        return pl.BlockSpec((bm, o.shape[1]), lambda i, j: (i, 0))

    def rhs(w):
        return pl.BlockSpec((w.shape[0], bn), lambda i, j: (0, j))

    def gate(n):
        return pl.BlockSpec((bm, bn), lambda i, j: (i, gate_off // bn + n * nj + j))

    assert gate_off % bn == 0
    return pl.pallas_call(
        _merge_kernel,
        grid=(m // bm, nj),
        in_specs=[lhs(o_a), lhs(o_b), lhs(o_c), rhs(w_a), rhs(w_b), rhs(w_c), gate(0), gate(1), gate(2),
                  pl.BlockSpec((N_BRANCH, bn), lambda i, j: (0, j))],
        out_specs=pl.BlockSpec((bm, bn), lambda i, j: (i, j)),
        out_shape=jax.ShapeDtypeStruct((m, d), BF16),
        compiler_params=_params("parallel", "arbitrary"),
    )(o_a, o_b, o_c, w_a, w_b, w_c, p, p, p, b_gate.astype(F32))


def _proj_norm_resid_kernel(a_ref, w_ref, g_ref, r_ref, gn_ref, x_ref, h_ref, *, n_k, bn):
    kk = pl.program_id(1)
    n = x_ref.shape[1]

    @pl.when(kk == 0)
    def _():
        x_ref[...] = jnp.zeros_like(x_ref)

    for c in range(n // bn):
        sl = slice(c * bn, (c + 1) * bn)
        x_ref[:, sl] += jnp.dot(a_ref[...], w_ref[:, sl], preferred_element_type=F32)

    @pl.when(kk == n_k - 1)
    def _():
        rows = min(x_ref.shape[0], EPILOGUE_ROWS)

        def chunk(c, _):
            rs = pl.ds(pl.multiple_of(c * rows, rows), rows)
            y = x_ref[rs, :]
            ms = jnp.mean(y * y, axis=-1, keepdims=True)
            x_new = r_ref[rs, :] + y * lax.rsqrt(ms + NORM_EPS) * g_ref[...]
            x_ref[rs, :] = x_new
            ms2 = jnp.mean(x_new * x_new, axis=-1, keepdims=True)
            h_ref[rs, :] = (x_new * lax.rsqrt(ms2 + NORM_EPS) * gn_ref[...]).astype(h_ref.dtype)
            return 0

        lax.fori_loop(0, x_ref.shape[0] // rows, chunk, 0)


def _proj_norm_resid(a, w, g_post, resid, g_next):
    m, k = a.shape
    n = w.shape[1]
    bm = _tile(m, 512)
    bk = _tile(k, 512)
    n_k = k // bk
    vec = pl.BlockSpec((1, n), lambda i, kk: (0, 0))
    row = pl.BlockSpec((bm, n), lambda i, kk: (i, 0))
    return pl.pallas_call(
        functools.partial(_proj_norm_resid_kernel, n_k=n_k, bn=_tile(n, 1024)),
        grid=(m // bm, n_k),
        in_specs=[pl.BlockSpec((bm, bk), lambda i, kk: (i, kk)),
                  pl.BlockSpec((bk, n), lambda i, kk: (kk, 0)),
                  vec, row, vec],
        out_specs=[row, row],
        out_shape=[jax.ShapeDtypeStruct((m, n), F32), jax.ShapeDtypeStruct((m, n), BF16)],
        compiler_params=_params("parallel", "arbitrary"),
    )(a, w, g_post.reshape(1, n).astype(F32), resid, g_next.reshape(1, n).astype(F32))


def _gate_up_kernel(a_ref, wg_ref, wu_ref, z_ref, u_ref):
    a = a_ref[...]
    z_ref[...] = jnp.dot(a, wg_ref[...], preferred_element_type=F32).astype(z_ref.dtype)
    u_ref[...] = jnp.dot(a, wu_ref[...], preferred_element_type=F32).astype(u_ref.dtype)


def _gate_up(h, w_gate, w_up):
    m, k = h.shape
    n = w_gate.shape[1]
    bm = _tile(m, 1024)
    bn = _tile(n, 512)
    wspec = pl.BlockSpec((k, bn), lambda i, j: (0, j))
    ospec = pl.BlockSpec((bm, bn), lambda i, j: (i, j))
    return pl.pallas_call(
        _gate_up_kernel,
        grid=(m // bm, n // bn),
        in_specs=[pl.BlockSpec((bm, k), lambda i, j: (i, 0)), wspec, wspec],
        out_specs=[ospec, ospec],
        out_shape=[jax.ShapeDtypeStruct((m, n), BF16)] * 2,
        compiler_params=_params("parallel", "arbitrary"),
    )(h, w_gate, w_up)


HALO_ROWS = 16


def _conv_gate_kernel(z_ref, halo_ref, u_ref, cw_ref, cb_ref, o_ref, *, blocks_per_seq):
    i = pl.program_id(0)
    z = z_ref[...].astype(F32)
    halo = halo_ref[...].astype(F32)
    at_start = (i % blocks_per_seq) == 0
    prev1 = jnp.where(at_start, 0.0, halo[HALO_ROWS - 1:HALO_ROWS, :])
    prev2 = jnp.where(at_start, 0.0, halo[HALO_ROWS - 2:HALO_ROWS - 1, :])
    row = lax.broadcasted_iota(jnp.int32, z.shape, 0)
    z1 = jnp.where(row == 0, prev1, pltpu.roll(z, 1, 0))
    z2 = jnp.where(row == 0, prev2, jnp.where(row == 1, prev1, pltpu.roll(z, 2, 0)))
    zc = cw_ref[0:1, :] * z2 + cw_ref[1:2, :] * z1 + cw_ref[2:3, :] * z + cb_ref[...]
    gelu = 0.5 * zc * (1.0 + jnp.tanh(math.sqrt(2.0 / math.pi) * (zc + 0.044715 * (zc * zc * zc))))
    o_ref[...] = (gelu * u_ref[...].astype(F32)).astype(o_ref.dtype)


def _conv_gate(z, u, conv_w, conv_b, seq):
    m, n = z.shape
    bm = _tile(seq, 512)
    bn = _tile(n, 512)
    halo_per_block = bm // HALO_ROWS
    spec = pl.BlockSpec((bm, bn), lambda i, j: (i, j))
    return pl.pallas_call(
        functools.partial(_conv_gate_kernel, blocks_per_seq=seq // bm),
        grid=(m // bm, n // bn),
        in_specs=[spec,
                  pl.BlockSpec((HALO_ROWS, bn), lambda i, j: (jnp.maximum(i * halo_per_block - 1, 0), j)),
                  spec,
                  pl.BlockSpec((CONV_WIDTH, bn), lambda i, j: (0, j)),
                  pl.BlockSpec((1, bn), lambda i, j: (0, j))],
        out_specs=spec,
        out_shape=jax.ShapeDtypeStruct((m, n), BF16),
        compiler_params=_params("parallel", "parallel"),
    )(z, z, u, conv_w, conv_b)


def _rope_tables(s):
    pos = jnp.arange(s, dtype=F32)

    def cos_sin(d):
        inv_freq = ROPE_THETA ** (-jnp.arange(0, d, 2, dtype=F32) / d)
        ang = pos[:, None] * inv_freq[None, :]
        return jnp.cos(ang), jnp.sin(ang)

    c, sn = cos_sin(HEAD_DIM)
    c128 = jnp.concatenate([c, c], axis=1)
    s128 = jnp.concatenate([-sn, sn], axis=1)
    ci, si = cos_sin(IDX_HEAD_DIM)
    zero = jnp.zeros_like(si)
    c64 = jnp.concatenate([ci, ci, ci, ci], axis=1)
    s_lo = jnp.concatenate([-si, zero, -si, zero], axis=1)
    s_hi = jnp.concatenate([zero, si, zero, si], axis=1)
    return c128, s128, c64, s_lo, s_hi


def _pad_cols(w, n):
    return jnp.pad(w, ((0, 0), (0, n - w.shape[1])))


def kernel(x, g_mix_pre, g_mix_post, w_in, b_forget, b_gate, lam_q1, lam_k1, lam_q2, lam_k2, g_subln, w_oa, w_ob,
           w_oc, w_out, g_ffn_pre, g_ffn_post, w_ffn_gate, w_ffn_up, conv_w, conv_b, w_ffn_down):
    b, s, d = x.shape
    depth = w_in.shape[0]
    m = b * s
    fox_w, dsa_w, diffv_w = w_oa.shape[1], w_ob.shape[1], w_oc.shape[1]
    fox_h, dsa_h, diff_h = fox_w // HEAD_DIM, dsa_w // HEAD_DIM, diffv_w // DIFF_V_DIM
    diffqk_w = diff_h * 2 * HEAD_DIM
    known = 3 * fox_w + fox_h + 3 * dsa_w + IDX_HEAD_DIM + 2 * diffqk_w + diffv_w + N_BRANCH * d
    idx_h = (w_in.shape[2] - known) // (IDX_HEAD_DIM + 1)
    iq_w = idx_h * IDX_HEAD_DIM
    d_ff = w_ffn_gate.shape[2]
    topk = min(INDEX_TOPK, s // 4)
    assert known + idx_h * (IDX_HEAD_DIM + 1) == w_in.shape[2]
    assert idx_h % 2 == 0 and IDX_HEAD_DIM + idx_h + fox_h <= LANES and s % CHUNK == 0

    sizes = dict(qa=fox_w, ka=fox_w, va=fox_w, fa=fox_h, qb=dsa_w, kb=dsa_w, vb=dsa_w, iq=iq_w, ik=IDX_HEAD_DIM,
                 iw=idx_h, qc=diffqk_w, kc=diffqk_w, vc=diffv_w, gl=N_BRANCH * d)
    src, pos = {}, 0
    for name, width in sizes.items():
        src[name] = (pos, pos + width)
        pos += width
    big_order = ("qa", "ka", "va", "qb", "kb", "vb", "iq", "qc", "kc", "vc", "gl")
    off, pos = {}, 0
    for name in big_order:
        off[name] = pos
        pos += sizes[name]
    n_big = pos
    bn = math.gcd(1024, *[sizes[name] for name in big_order])
    assert bn % (2 * LANES) == 0
    assert off["iq"] % iq_w == 0 and off["qb"] % dsa_w == 0 and off["vc"] % DIFF_V_DIM == 0

    def tiles_of(*names):
        return tuple(t for name in names for t in range(off[name] // bn, (off[name] + sizes[name]) // bn))

    tables = _rope_tables(s)
    ffp = -(-d_ff // 1024) * 1024 if d_ff > 1024 else d_ff

    x2 = x.reshape(m, d)
    hcur = _rmsnorm(x2, g_mix_pre[0])
    for l in range(depth):
        wl = w_in[l]
        w_big = jnp.concatenate([wl[:, src[n][0]:src[n][1]] for n in big_order], axis=1).astype(BF16)
        w_small = _pad_cols(jnp.concatenate([wl[:, src[n][0]:src[n][1]] for n in ("ik", "iw", "fa")], axis=1),
                            LANES).astype(BF16)

        p = _inproj(hcur, w_big, tables, s, bn=bn, q_tiles=tiles_of("qa", "qb", "qc"),
                    rope128_tiles=tiles_of("qb", "kb", "qc", "kc"), rope64_tiles=tiles_of("iq"),
                    q_scale=HEAD_DIM ** -0.5)
        small = _inproj_small(hcur, w_small, tables[2:], s)
        p3 = p.reshape(b, s, n_big)
        small3 = small.reshape(b, s, LANES)

        fa_t = jnp.swapaxes(small3[:, :, IDX_HEAD_DIM + idx_h:IDX_HEAD_DIM + idx_h + fox_h], 1, 2)
        f_cum = _forget_cumsum(fa_t, b_forget[l])
        o_a = _fox_attention(p3, f_cum, heads=fox_h, q_off=off["qa"] // HEAD_DIM, k_off=off["ka"] // HEAD_DIM,
                             v_off=off["va"] // HEAD_DIM)

        ik = small3[:, :, :IDX_HEAD_DIM].astype(BF16)
        zeros = jnp.zeros_like(ik)
        ik_lo = jnp.concatenate([ik, zeros], axis=2)
        ik_hi = jnp.concatenate([zeros, ik], axis=2)
        o_b = _dsa_attention(p3, small3, ik_lo, ik_hi, heads=dsa_h, idx_heads=idx_h, iq_off=off["iq"] // iq_w,
                             q_off=off["qb"] // dsa_w, k_off=off["kb"] // dsa_w, v_off=off["vb"] // dsa_w, topk=topk)

        lam_init = 0.8 - 0.6 * math.exp(-0.3 * l)
        o_c = _diff_attention(p3, (lam_q1[l], lam_k1[l], lam_q2[l], lam_k2[l]), g_subln[l], heads=diff_h,
                              q_off=off["qc"] // HEAD_DIM, k_off=off["kc"] // HEAD_DIM,
                              v_off256=off["vc"] // DIFF_V_DIM, lam_init=lam_init)

        merged = _merge(o_a.reshape(m, fox_w), o_b.reshape(m, dsa_w), o_c.reshape(m, diffv_w),
                        w_oa[l].astype(BF16), w_ob[l].astype(BF16), w_oc[l].astype(BF16), p, b_gate[l],
                        gate_off=off["gl"], d=d)
        x2, hcur = _proj_norm_resid(merged, w_out[l].astype(BF16), g_mix_post[l], x2, g_ffn_pre[l])

        z, u = _gate_up(hcur, _pad_cols(w_ffn_gate[l], ffp).astype(BF16), _pad_cols(w_ffn_up[l], ffp).astype(BF16))
        act = _conv_gate(z, u, _pad_cols(conv_w[l], ffp).astype(F32), _pad_cols(conv_b[l][None, :], ffp).astype(F32), s)
        w_down = jnp.pad(w_ffn_down[l], ((0, ffp - d_ff), (0, 0))).astype(BF16)
        g_next = g_mix_pre[l + 1] if l + 1 < depth else g_mix_pre[0]
        x2, hcur = _proj_norm_resid(act, w_down, g_ffn_post[l], x2, g_next)
    return x2.reshape(b, s, d)
```

```python
import functools
import math

import jax
import jax.numpy as jnp
from jax import lax
from jax.experimental import pallas as pl
from jax.experimental.pallas import tpu as pltpu

CHUNK = 64
ROPE_THETA = 10000.0
NORM_EPS = 1e-6
HEAD_DIM = 128
IDX_HEAD_DIM = 64
DIFF_V_DIM = 256
INDEX_TOPK = 256
N_BRANCH = 3
CONV_WIDTH = 3

LANES = 128
VMEM_LIMIT_BYTES = 56 * 1024 * 1024
EPILOGUE_ROWS = 64
MASK_VALUE = -1e30
INT32_MIN = -(2 ** 31)
LOG2E = math.log2(math.e)

F32 = jnp.float32
BF16 = jnp.bfloat16
_NT = (((1,), (1,)), ((), ()))


def _params(*semantics):
    return pltpu.CompilerParams(dimension_semantics=semantics, vmem_limit_bytes=VMEM_LIMIT_BYTES)


def _tile(dim, want):
    t = min(dim, want)
    assert dim % t == 0, (dim, want)
    return t


def _rmsnorm_kernel(x_ref, g_ref, o_ref):
    x = x_ref[...]
    ms = jnp.mean(x * x, axis=-1, keepdims=True)
    o_ref[...] = (x * lax.rsqrt(ms + NORM_EPS) * g_ref[...]).astype(o_ref.dtype)


def _rmsnorm(x, g):
    m, d = x.shape
    bm = _tile(m, 512)
    return pl.pallas_call(
        _rmsnorm_kernel,
        grid=(m // bm,),
        in_specs=[pl.BlockSpec((bm, d), lambda i: (i, 0)), pl.BlockSpec((1, d), lambda i: (0, 0))],
        out_specs=pl.BlockSpec((bm, d), lambda i: (i, 0)),
        out_shape=jax.ShapeDtypeStruct((m, d), BF16),
        compiler_params=_params("parallel"),
    )(x, g.reshape(1, d))


def _rope128(x, cos, sin):
    return x * cos + pltpu.roll(x, HEAD_DIM // 2, 1) * sin


def _rope64(x, cos, sin_lo, sin_hi):
    return x * cos + pltpu.roll(x, LANES - 32, 1) * sin_lo + pltpu.roll(x, 32, 1) * sin_hi


def _inproj_kernel(a_ref, w_ref, c128_ref, s128_ref, c64_ref, slo_ref, shi_ref, o_ref, *,
                   q_tiles, rope128_tiles, rope64_tiles, q_scale):
    j = pl.program_id(1)

    def member(tiles):
        hit = j < 0
        for t in tiles:
            hit = hit | (j == t)
        return hit

    acc = jnp.dot(a_ref[...], w_ref[...], preferred_element_type=F32)
    acc = acc * jnp.where(member(q_tiles), q_scale, 1.0).astype(F32)
    is128 = member(rope128_tiles)
    is64 = member(rope64_tiles)
    n_sub = acc.shape[1] // LANES

    @pl.when(is128)
    def _():
        for c in range(n_sub):
            sl = slice(c * LANES, (c + 1) * LANES)
            o_ref[:, sl] = _rope128(acc[:, sl], c128_ref[...], s128_ref[...]).astype(o_ref.dtype)

    @pl.when(is64)
    def _():
        for c in range(n_sub):
            sl = slice(c * LANES, (c + 1) * LANES)
            o_ref[:, sl] = _rope64(acc[:, sl], c64_ref[...], slo_ref[...], shi_ref[...]).astype(o_ref.dtype)

    @pl.when(jnp.logical_not(is128 | is64))
    def _():
        o_ref[...] = acc.astype(o_ref.dtype)


def _inproj(h, w, tables, seq, *, bn, q_tiles, rope128_tiles, rope64_tiles, q_scale):
    m, k = h.shape
    n = w.shape[1]
    bm = _tile(seq, 1024)
    pos_blocks = seq // bm
    tab_spec = pl.BlockSpec((bm, LANES), lambda i, j: (i % pos_blocks, 0))
    kern = functools.partial(_inproj_kernel, q_tiles=q_tiles, rope128_tiles=rope128_tiles,
                             rope64_tiles=rope64_tiles, q_scale=q_scale)
    return pl.pallas_call(
        kern,
        grid=(m // bm, n // bn),
        in_specs=[pl.BlockSpec((bm, k), lambda i, j: (i, 0)),
                  pl.BlockSpec((k, bn), lambda i, j: (0, j))] + [tab_spec] * 5,
        out_specs=pl.BlockSpec((bm, bn), lambda i, j: (i, j)),
        out_shape=jax.ShapeDtypeStruct((m, n), BF16),
        compiler_params=_params("parallel", "arbitrary"),
    )(h, w, *tables)


def _inproj_small_kernel(a_ref, w_ref, c64_ref, slo_ref, shi_ref, o_ref):
    acc = jnp.dot(a_ref[...], w_ref[...], preferred_element_type=F32)
    roped = _rope64(acc, c64_ref[...], slo_ref[...], shi_ref[...])
    lane = lax.broadcasted_iota(jnp.int32, acc.shape, 1)
    o_ref[...] = jnp.where(lane < IDX_HEAD_DIM, roped, acc)


def _inproj_small(h, w, tables64, seq):
    m, k = h.shape
    bm = _tile(seq, 1024)
    pos_blocks = seq // bm
    tab_spec = pl.BlockSpec((bm, LANES), lambda i: (i % pos_blocks, 0))
    return pl.pallas_call(
        _inproj_small_kernel,
        grid=(m // bm,),
        in_specs=[pl.BlockSpec((bm, k), lambda i: (i, 0)),
                  pl.BlockSpec((k, LANES), lambda i: (0, 0))] + [tab_spec] * 3,
        out_specs=pl.BlockSpec((bm, LANES), lambda i: (i, 0)),
        out_shape=jax.ShapeDtypeStruct((m, LANES), F32),
        compiler_params=_params("parallel"),
    )(h, w, *tables64)


def _split3(x):
    hi = x.astype(BF16)
    rem = x - hi.astype(F32)
    mid = rem.astype(BF16)
    lo = (rem - mid.astype(F32)).astype(BF16)
    return hi, mid, lo


def _forget_cumsum_kernel(fa_ref, b_ref, hi_ref, mid_ref, lo_ref):
    x = fa_ref[0] + b_ref[...]
    lf = jnp.minimum(x, 0.0) - jnp.log1p(jnp.exp(-jnp.abs(x)))
    row = lax.broadcasted_iota(jnp.int32, (LANES, LANES), 0)
    col = lax.broadcasted_iota(jnp.int32, (LANES, LANES), 1)
    tri = (row <= col).astype(BF16)
    carry = jnp.zeros((lf.shape[0], 1), F32)
    for c in range(lf.shape[1] // LANES):
        sl = slice(c * LANES, (c + 1) * LANES)
        loc = sum(jnp.dot(piece, tri, preferred_element_type=F32) for piece in _split3(lf[:, sl]))
        hi_ref[0, :, sl], mid_ref[0, :, sl], lo_ref[0, :, sl] = _split3((loc + carry) * LOG2E)
        carry = carry + loc[:, LANES - 1:LANES]


def _forget_cumsum(fa_t, b_forget):
    b, hn, s = fa_t.shape
    out_spec = pl.BlockSpec((1, hn, s), lambda i: (i, 0, 0))
    return pl.pallas_call(
        _forget_cumsum_kernel,
        grid=(b,),
        in_specs=[pl.BlockSpec((1, hn, s), lambda i: (i, 0, 0)), pl.BlockSpec((hn, 1), lambda i: (0, 0))],
        out_specs=[out_spec] * 3,
        out_shape=[jax.ShapeDtypeStruct((b, hn, s), BF16)] * 3,
        compiler_params=_params("parallel"),
    )(fa_t, b_forget.reshape(hn, 1).astype(F32))


def _online_update(s, v, m, l, acc):
    m_new = jnp.maximum(m, jnp.max(s, axis=1, keepdims=True))
    alpha = jnp.exp2(m - m_new)
    p = jnp.exp2(s - m_new)
    l_new = alpha * l + jnp.sum(p, axis=1, keepdims=True)
    acc_new = alpha * acc + jnp.dot(p.astype(v.dtype), v, preferred_element_type=F32)
    return m_new, l_new, acc_new


def _softmax_init(tq, dv):
    return (jnp.full((tq, 1), MASK_VALUE, F32), jnp.zeros((tq, 1), F32), jnp.zeros((tq, dv), F32))


def _fox_kernel(q_ref, qx_ref, k_ref, kx_ref, v_ref, o_ref, *, tq, tk, group):
    i = pl.program_id(2)
    n_full = (i * tq) // tk
    ones = jnp.ones((tk, HEAD_DIM), v_ref.dtype)

    def tile(j, carry, diagonal):
        off = pl.multiple_of(j * tk, tk)
        out = []
        for g in range(group):
            hs = slice(g * HEAD_DIM, (g + 1) * HEAD_DIM)
            m, l, acc = carry[g]
            q = jnp.concatenate([q_ref[0, :, hs], qx_ref[0, g]], axis=1)
            k = jnp.concatenate([k_ref[0, pl.ds(off, tk), hs], kx_ref[0, g, pl.ds(off, tk), :]], axis=1)
            s = lax.dot_general(q, k, _NT, preferred_element_type=F32)
            if diagonal:
                row = i * tq + lax.broadcasted_iota(jnp.int32, s.shape, 0)
                col = off + lax.broadcasted_iota(jnp.int32, s.shape, 1)
                s = jnp.where(row >= col, s, MASK_VALUE)
            m_new = jnp.maximum(m, jnp.max(s, axis=1, keepdims=True))
            alpha = jnp.exp2(m - m_new)
            p = jnp.exp2(s - m_new).astype(v_ref.dtype)
            v = jnp.concatenate([v_ref[0, pl.ds(off, tk), hs], ones], axis=1)
            pv = jnp.dot(p, v, preferred_element_type=F32)
            out.append((m_new, alpha * l + pv[:, HEAD_DIM:], alpha * acc + pv[:, :HEAD_DIM]))
        return tuple(out)

    init = tuple((jnp.full((tq, 1), MASK_VALUE, F32), jnp.zeros((tq, HEAD_DIM), F32), jnp.zeros((tq, HEAD_DIM), F32))
                 for _ in range(group))
    carry = lax.fori_loop(0, n_full, lambda j, c: tile(j, c, False), init)
    final = tile(n_full, carry, True)
    for g in range(group):
        _, l, acc = final[g]
        o_ref[0, :, g * HEAD_DIM:(g + 1) * HEAD_DIM] = (acc / l).astype(o_ref.dtype)


def _fox_attention(p3, f_cum3, *, heads, q_off, k_off, v_off):
    b, s, _ = p3.shape
    tq = _tile(s, 512)
    tk = _tile(s, 1024)
    group = math.gcd(heads, 2)
    gw = group * HEAD_DIM
    assert q_off % group == 0 and k_off % group == 0 and v_off % group == 0
    hi, mid, lo = f_cum3
    one = jnp.ones_like(hi)
    pad = jnp.zeros((b, heads, s, HEAD_DIM - 6), BF16)
    qx = jnp.concatenate([jnp.stack([hi, mid, lo, one, one, one], axis=-1), pad], axis=-1)
    kx = jnp.concatenate([jnp.stack([one, one, one, -hi, -mid, -lo], axis=-1), pad], axis=-1)
    return pl.pallas_call(
        functools.partial(_fox_kernel, tq=tq, tk=tk, group=group),
        grid=(b, heads // group, s // tq),
        in_specs=[pl.BlockSpec((1, tq, gw), lambda bi, h, i: (bi, i, q_off // group + h)),
                  pl.BlockSpec((1, group, tq, HEAD_DIM), lambda bi, h, i: (bi, h, i, 0)),
                  pl.BlockSpec((1, s, gw), lambda bi, h, i: (bi, 0, k_off // group + h)),
                  pl.BlockSpec((1, group, s, HEAD_DIM), lambda bi, h, i: (bi, h, 0, 0)),
                  pl.BlockSpec((1, s, gw), lambda bi, h, i: (bi, 0, v_off // group + h))],
        out_specs=pl.BlockSpec((1, tq, gw), lambda bi, h, i: (bi, i, h)),
        out_shape=jax.ShapeDtypeStruct((b, s, heads * HEAD_DIM), BF16),
        compiler_params=_params("parallel", "parallel", "arbitrary"),
    )(p3, qx, p3, kx, p3)


def _diff_kernel(q_ref, k_ref, v_ref, lq1_ref, lk1_ref, lq2_ref, lk2_ref, g_ref, o_ref, *, tq, tk, group, lam_init):
    i = pl.program_id(2)
    n_full = (i * tq) // tk

    def tile(j, carry, diagonal):
        off = pl.multiple_of(j * tk, tk)
        if diagonal:
            row = (i * tq + lax.broadcasted_iota(jnp.int32, (tq, tk), 0)) // CHUNK
            col = (off + lax.broadcasted_iota(jnp.int32, (tq, tk), 1)) // CHUNK
            keep = row >= col
        out = []
        for g in range(group):
            v = v_ref[0, pl.ds(off, tk), g * DIFF_V_DIM:(g + 1) * DIFF_V_DIM]
            for half in range(2):
                hs = slice((2 * g + half) * HEAD_DIM, (2 * g + half + 1) * HEAD_DIM)
                s = lax.dot_general(q_ref[0, :, hs], k_ref[0, pl.ds(off, tk), hs], _NT, preferred_element_type=F32)
                if diagonal:
                    s = jnp.where(keep, s, MASK_VALUE)
                out.append(_online_update(s, v, *carry[2 * g + half]))
        return tuple(out)

    init = tuple(_softmax_init(tq, DIFF_V_DIM) for _ in range(2 * group))
    carry = lax.fori_loop(0, n_full, lambda j, c: tile(j, c, False), init)
    final = tile(n_full, carry, True)

    lam = (jnp.exp(jnp.sum(lq1_ref[...] * lk1_ref[...], axis=1, keepdims=True))
           - jnp.exp(jnp.sum(lq2_ref[...] * lk2_ref[...], axis=1, keepdims=True)) + lam_init)
    for g in range(group):
        (_, l1, a1), (_, l2, a2) = final[2 * g], final[2 * g + 1]
        o = a1 / l1 - lam * (a2 / l2)
        ms = jnp.mean(o * o, axis=-1, keepdims=True)
        o = o * lax.rsqrt(ms + NORM_EPS) * g_ref[...]
        o_ref[0, :, g * DIFF_V_DIM:(g + 1) * DIFF_V_DIM] = (o * (1.0 - lam_init)).astype(o_ref.dtype)


def _diff_attention(p3, lam_vecs, g_subln, *, heads, q_off, k_off, v_off, lam_init):
    b, s, _ = p3.shape
    tq = _tile(s, 512)
    tk = _tile(s, 1024)
    group = 1
    gw = group * DIFF_V_DIM
    assert q_off % gw == 0 and k_off % gw == 0 and v_off % gw == 0
    vec_spec = pl.BlockSpec((1, HEAD_DIM), lambda bi, h, i: (0, 0))
    return pl.pallas_call(
        functools.partial(_diff_kernel, tq=tq, tk=tk, group=group, lam_init=lam_init),
        grid=(b, heads // group, s // tq),
        in_specs=[pl.BlockSpec((1, tq, gw), lambda bi, h, i: (bi, i, q_off // gw + h)),
                  pl.BlockSpec((1, s, gw), lambda bi, h, i: (bi, 0, k_off // gw + h)),
                  pl.BlockSpec((1, s, gw), lambda bi, h, i: (bi, 0, v_off // gw + h)),
                  vec_spec, vec_spec, vec_spec, vec_spec,
                  pl.BlockSpec((1, DIFF_V_DIM), lambda bi, h, i: (0, 0))],
        out_specs=pl.BlockSpec((1, tq, gw), lambda bi, h, i: (bi, i, h)),
        out_shape=jax.ShapeDtypeStruct((b, s, heads * DIFF_V_DIM), BF16),
        compiler_params=_params("parallel", "parallel", "arbitrary"),
    )(p3, p3, p3, *[v.reshape(1, HEAD_DIM).astype(F32) for v in lam_vecs],
      g_subln.reshape(1, DIFF_V_DIM).astype(F32))


def _ordered_key(x):
    bits = pltpu.bitcast(x, jnp.int32)
    return bits ^ (lax.shift_right_arithmetic(bits, 31) & 0x7FFFFFFF)


def _dsa_kernel(iq_ref, iw_ref, iklo_ref, ikhi_ref, q_ref, k_ref, v_ref, o_ref, keys_ref, *,
                tq, tk, idx_heads, heads, topk, idx_scale):
    i = pl.program_id(1)
    n_tiles = (i * tq) // tk + 1
    iw = iw_ref[0][:, IDX_HEAD_DIM:IDX_HEAD_DIM + idx_heads] * idx_scale
    row_chunk = (i * tq + lax.broadcasted_iota(jnp.int32, (tq, tk), 0)) // CHUNK
    col_iota = lax.broadcasted_iota(jnp.int32, (tq, tk), 1)

    def score_tile(j, _):
        off = pl.multiple_of(j * tk, tk)
        ik_lo = iklo_ref[0, pl.ds(off, tk), :]
        ik_hi = ikhi_ref[0, pl.ds(off, tk), :]
        sc = jnp.zeros((tq, tk), F32)
        for p in range(idx_heads // 2):
            a = iq_ref[0, :, p * LANES:(p + 1) * LANES]
            even = lax.dot_general(a, ik_lo, _NT, preferred_element_type=F32)
            odd = lax.dot_general(a, ik_hi, _NT, preferred_element_type=F32)
            sc = sc + jnp.maximum(even, 0.0) * iw[:, 2 * p:2 * p + 1]
            sc = sc + jnp.maximum(odd, 0.0) * iw[:, 2 * p + 1:2 * p + 2]
        valid = row_chunk >= (off + col_iota) // CHUNK
        keys_ref[:, pl.ds(off, tk)] = jnp.where(valid, _ordered_key(sc), INT32_MIN)
        return 0

    lax.fori_loop(0, n_tiles, score_tile, 0)

    def count_ge(cand):
        def body(j, part):
            off = pl.multiple_of(j * tk, tk)
            ge = (keys_ref[:, pl.ds(off, tk)] >= cand).astype(jnp.int32)
            for c in range(tk // LANES):
                part = part + ge[:, c * LANES:(c + 1) * LANES]
            return part
        part = lax.fori_loop(0, n_tiles, body, jnp.zeros((tq, LANES), jnp.int32))
        return jnp.sum(part, axis=1, keepdims=True)

    thr = jnp.where(count_ge(jnp.zeros((tq, 1), jnp.int32)) >= topk, 0, INT32_MIN).astype(jnp.int32)

    def bit_step(t, thr):
        cand = thr + lax.shift_left(jnp.int32(1), 30 - t)
        return jnp.where(count_ge(cand) >= topk, cand, thr)

    thr = lax.fori_loop(0, 31, bit_step, thr)
    thr = jnp.maximum(thr, INT32_MIN + 1)

    def attend(j, carry):
        off = pl.multiple_of(j * tk, tk)
        selected = keys_ref[:, pl.ds(off, tk)] >= thr
        out = []
        for h in range(heads):
            hs = slice(h * HEAD_DIM, (h + 1) * HEAD_DIM)
            k = k_ref[0, pl.ds(off, tk), hs]
            v = v_ref[0, pl.ds(off, tk), hs]
            s = lax.dot_general(q_ref[0, :, hs], k, _NT, preferred_element_type=F32)
            out.append(_online_update(jnp.where(selected, s, MASK_VALUE), v, *carry[h]))
        return tuple(out)

    final = lax.fori_loop(0, n_tiles, attend, tuple(_softmax_init(tq, HEAD_DIM) for _ in range(heads)))
    for h in range(heads):
        _, l, acc = final[h]
        o_ref[0, :, h * HEAD_DIM:(h + 1) * HEAD_DIM] = (acc / l).astype(o_ref.dtype)


def _dsa_attention(p3, small3, ik_lo, ik_hi, *, heads, idx_heads, iq_off, q_off, k_off, v_off, topk):
    b, s, _ = p3.shape
    tq = _tile(s, 256)
    tk = _tile(s, 512)
    width = heads * HEAD_DIM
    iq_width = idx_heads * IDX_HEAD_DIM
    resident = dict(pipeline_mode=pl.Buffered(1))
    kern = functools.partial(_dsa_kernel, tq=tq, tk=tk, idx_heads=idx_heads, heads=heads, topk=topk,
                             idx_scale=(IDX_HEAD_DIM ** -0.5) * (idx_heads ** -0.5))
    return pl.pallas_call(
        kern,
        grid=(b, s // tq),
        in_specs=[pl.BlockSpec((1, tq, iq_width), lambda bi, i: (bi, i, iq_off)),
                  pl.BlockSpec((1, tq, LANES), lambda bi, i: (bi, i, 0)),
                  pl.BlockSpec((1, s, LANES), lambda bi, i: (bi, 0, 0), **resident),
                  pl.BlockSpec((1, s, LANES), lambda bi, i: (bi, 0, 0), **resident),
                  pl.BlockSpec((1, tq, width), lambda bi, i: (bi, i, q_off)),
                  pl.BlockSpec((1, s, width), lambda bi, i: (bi, 0, k_off), **resident),
                  pl.BlockSpec((1, s, width), lambda bi, i: (bi, 0, v_off), **resident)],
        out_specs=pl.BlockSpec((1, tq, width), lambda bi, i: (bi, i, 0)),
        out_shape=jax.ShapeDtypeStruct((b, s, width), BF16),
        scratch_shapes=[pltpu.VMEM((tq, s), jnp.int32)],
        compiler_params=_params("parallel", "arbitrary"),
    )(p3, small3, ik_lo, ik_hi, p3, p3, p3)


def _merge_kernel(oa_ref, ob_ref, oc_ref, wa_ref, wb_ref, wc_ref, g0_ref, g1_ref, g2_ref, bg_ref, o_ref):
    def branch(o_r, w_r, g_r, n):
        y = jnp.dot(o_r[...], w_r[...], preferred_element_type=F32)
        return jax.nn.sigmoid(g_r[...].astype(F32) + bg_ref[n:n + 1, :]) * y

    out = branch(oa_ref, wa_ref, g0_ref, 0) + branch(ob_ref, wb_ref, g1_ref, 1) + branch(oc_ref, wc_ref, g2_ref, 2)
    o_ref[...] = out.astype(o_ref.dtype)


def _merge(o_a, o_b, o_c, w_a, w_b, w_c, p, b_gate, *, gate_off, d):
    m = o_a.shape[0]
    bm = _tile(m, 512)
    bn = _tile(d, 1024)
    nj = d // bn

    def lhs(o):
        return pl.BlockSpec((bm, o.shape[1]), lambda i, j: (i, 0))

    def rhs(w):
        return pl.BlockSpec((w.shape[0], bn), lambda i, j: (0, j))

    def gate(n):
        return pl.BlockSpec((bm, bn), lambda i, j: (i, gate_off // bn + n * nj + j))

    assert gate_off % bn == 0
    return pl.pallas_call(
        _merge_kernel,
        grid=(m // bm, nj),
        in_specs=[lhs(o_a), lhs(o_b), lhs(o_c), rhs(w_a), rhs(w_b), rhs(w_c), gate(0), gate(1), gate(2),
                  pl.BlockSpec((N_BRANCH, bn), lambda i, j: (0, j))],
        out_specs=pl.BlockSpec((bm, bn), lambda i, j: (i, j)),
        out_shape=jax.ShapeDtypeStruct((m, d), BF16),
        compiler_params=_params("parallel", "arbitrary"),
    )(o_a, o_b, o_c, w_a, w_b, w_c, p, p, p, b_gate.astype(F32))


def _proj_norm_resid_kernel(a_ref, w_ref, g_ref, r_ref, gn_ref, x_ref, h_ref, *, n_k, bn):
    kk = pl.program_id(1)
    n = x_ref.shape[1]

    @pl.when(kk == 0)
    def _():
        x_ref[...] = jnp.zeros_like(x_ref)

    for c in range(n // bn):
        sl = slice(c * bn, (c + 1) * bn)
        x_ref[:, sl] += jnp.dot(a_ref[...], w_ref[:, sl], preferred_element_type=F32)

    @pl.when(kk == n_k - 1)
    def _():
        rows = min(x_ref.shape[0], EPILOGUE_ROWS)

        def chunk(c, _):
            rs = pl.ds(pl.multiple_of(c * rows, rows), rows)
            y = x_ref[rs, :]
            ms = jnp.mean(y * y, axis=-1, keepdims=True)
            x_new = r_ref[rs, :] + y * lax.rsqrt(ms + NORM_EPS) * g_ref[...]
            x_ref[rs, :] = x_new
            ms2 = jnp.mean(x_new * x_new, axis=-1, keepdims=True)
            h_ref[rs, :] = (x_new * lax.rsqrt(ms2 + NORM_EPS) * gn_ref[...]).astype(h_ref.dtype)
            return 0

        lax.fori_loop(0, x_ref.shape[0] // rows, chunk, 0)


def _proj_norm_resid(a, w, g_post, resid, g_next):
    m, k = a.shape
    n = w.shape[1]
    bm = _tile(m, 512)
    bk = _tile(k, 512)
    n_k = k // bk
    vec = pl.BlockSpec((1, n), lambda i, kk: (0, 0))
    row = pl.BlockSpec((bm, n), lambda i, kk: (i, 0))
    return pl.pallas_call(
        functools.partial(_proj_norm_resid_kernel, n_k=n_k, bn=_tile(n, 1024)),
        grid=(m // bm, n_k),
        in_specs=[pl.BlockSpec((bm, bk), lambda i, kk: (i, kk)),
                  pl.BlockSpec((bk, n), lambda i, kk: (kk, 0)),
                  vec, row, vec],
        out_specs=[row, row],
        out_shape=[jax.ShapeDtypeStruct((m, n), F32), jax.ShapeDtypeStruct((m, n), BF16)],
        compiler_params=_params("parallel", "arbitrary"),
    )(a, w, g_post.reshape(1, n).astype(F32), resid, g_next.reshape(1, n).astype(F32))


def _gate_up_kernel(a_ref, wg_ref, wu_ref, z_ref, u_ref):
    a = a_ref[...]
    z_ref[...] = jnp.dot(a, wg_ref[...], preferred_element_type=F32).astype(z_ref.dtype)
    u_ref[...] = jnp.dot(a, wu_ref[...], preferred_element_type=F32).astype(u_ref.dtype)


def _gate_up(h, w_gate, w_up):
    m, k = h.shape
    n = w_gate.shape[1]
    bm = _tile(m, 1024)
    bn = _tile(n, 512)
    wspec = pl.BlockSpec((k, bn), lambda i, j: (0, j))
    ospec = pl.BlockSpec((bm, bn), lambda i, j: (i, j))
    return pl.pallas_call(
        _gate_up_kernel,
        grid=(m // bm, n // bn),
        in_specs=[pl.BlockSpec((bm, k), lambda i, j: (i, 0)), wspec, wspec],
        out_specs=[ospec, ospec],
        out_shape=[jax.ShapeDtypeStruct((m, n), BF16)] * 2,
        compiler_params=_params("parallel", "arbitrary"),
    )(h, w_gate, w_up)


HALO_ROWS = 16


def _conv_gate_kernel(z_ref, halo_ref, u_ref, cw_ref, cb_ref, o_ref, *, blocks_per_seq):
    i = pl.program_id(0)
    z = z_ref[...].astype(F32)
    halo = halo_ref[...].astype(F32)
    at_start = (i % blocks_per_seq) == 0
    prev1 = jnp.where(at_start, 0.0, halo[HALO_ROWS - 1:HALO_ROWS, :])
    prev2 = jnp.where(at_start, 0.0, halo[HALO_ROWS - 2:HALO_ROWS - 1, :])
    row = lax.broadcasted_iota(jnp.int32, z.shape, 0)
    z1 = jnp.where(row == 0, prev1, pltpu.roll(z, 1, 0))
    z2 = jnp.where(row == 0, prev2, jnp.where(row == 1, prev1, pltpu.roll(z, 2, 0)))
    zc = cw_ref[0:1, :] * z2 + cw_ref[1:2, :] * z1 + cw_ref[2:3, :] * z + cb_ref[...]
    gelu = 0.5 * zc * (1.0 + jnp.tanh(math.sqrt(2.0 / math.pi) * (zc + 0.044715 * (zc * zc * zc))))
    o_ref[...] = (gelu * u_ref[...].astype(F32)).astype(o_ref.dtype)


def _conv_gate(z, u, conv_w, conv_b, seq):
    m, n = z.shape
    bm = _tile(seq, 512)
    bn = _tile(n, 512)
    halo_per_block = bm // HALO_ROWS
    spec = pl.BlockSpec((bm, bn), lambda i, j: (i, j))
    return pl.pallas_call(
        functools.partial(_conv_gate_kernel, blocks_per_seq=seq // bm),
        grid=(m // bm, n // bn),
        in_specs=[spec,
                  pl.BlockSpec((HALO_ROWS, bn), lambda i, j: (jnp.maximum(i * halo_per_block - 1, 0), j)),
                  spec,
                  pl.BlockSpec((CONV_WIDTH, bn), lambda i, j: (0, j)),
                  pl.BlockSpec((1, bn), lambda i, j: (0, j))],
        out_specs=spec,
        out_shape=jax.ShapeDtypeStruct((m, n), BF16),
        compiler_params=_params("parallel", "parallel"),
    )(z, z, u, conv_w, conv_b)


def _rope_tables(s):
    pos = jnp.arange(s, dtype=F32)

    def cos_sin(d):
        inv_freq = ROPE_THETA ** (-jnp.arange(0, d, 2, dtype=F32) / d)
        ang = pos[:, None] * inv_freq[None, :]
        return jnp.cos(ang), jnp.sin(ang)

    c, sn = cos_sin(HEAD_DIM)
    c128 = jnp.concatenate([c, c], axis=1)
    s128 = jnp.concatenate([-sn, sn], axis=1)
    ci, si = cos_sin(IDX_HEAD_DIM)
    zero = jnp.zeros_like(si)
    c64 = jnp.concatenate([ci, ci, ci, ci], axis=1)
    s_lo = jnp.concatenate([-si, zero, -si, zero], axis=1)
    s_hi = jnp.concatenate([zero, si, zero, si], axis=1)
    return c128, s128, c64, s_lo, s_hi


def _pad_cols(w, n):
    return jnp.pad(w, ((0, 0), (0, n - w.shape[1])))


def kernel(x, g_mix_pre, g_mix_post, w_in, b_forget, b_gate, lam_q1, lam_k1, lam_q2, lam_k2, g_subln, w_oa, w_ob,
           w_oc, w_out, g_ffn_pre, g_ffn_post, w_ffn_gate, w_ffn_up, conv_w, conv_b, w_ffn_down):
    b, s, d = x.shape
    depth = w_in.shape[0]
    m = b * s
    fox_w, dsa_w, diffv_w = w_oa.shape[1], w_ob.shape[1], w_oc.shape[1]
    fox_h, dsa_h, diff_h = fox_w // HEAD_DIM, dsa_w // HEAD_DIM, diffv_w // DIFF_V_DIM
    diffqk_w = diff_h * 2 * HEAD_DIM
    known = 3 * fox_w + fox_h + 3 * dsa_w + IDX_HEAD_DIM + 2 * diffqk_w + diffv_w + N_BRANCH * d
    idx_h = (w_in.shape[2] - known) // (IDX_HEAD_DIM + 1)
    iq_w = idx_h * IDX_HEAD_DIM
    d_ff = w_ffn_gate.shape[2]
    topk = min(INDEX_TOPK, s // 4)
    assert known + idx_h * (IDX_HEAD_DIM + 1) == w_in.shape[2]
    assert idx_h % 2 == 0 and IDX_HEAD_DIM + idx_h + fox_h <= LANES and s % CHUNK == 0

    sizes = dict(qa=fox_w, ka=fox_w, va=fox_w, fa=fox_h, qb=dsa_w, kb=dsa_w, vb=dsa_w, iq=iq_w, ik=IDX_HEAD_DIM,
                 iw=idx_h, qc=diffqk_w, kc=diffqk_w, vc=diffv_w, gl=N_BRANCH * d)
    src, pos = {}, 0
    for name, width in sizes.items():
        src[name] = (pos, pos + width)
        pos += width
    big_order = ("qa", "ka", "va", "qb", "kb", "vb", "iq", "qc", "kc", "vc", "gl")
    off, pos = {}, 0
    for name in big_order:
        off[name] = pos
        pos += sizes[name]
    n_big = pos
    bn = math.gcd(1024, *[sizes[name] for name in big_order])
    assert bn % (2 * LANES) == 0
    assert off["iq"] % iq_w == 0 and off["qb"] % dsa_w == 0

    def tiles_of(*names):
        return tuple(t for name in names for t in range(off[name] // bn, (off[name] + sizes[name]) // bn))

    tables = _rope_tables(s)
    ffp = -(-d_ff // 1024) * 1024 if d_ff > 1024 else d_ff

    x2 = x.reshape(m, d)
    hcur = _rmsnorm(x2, g_mix_pre[0])
    for l in range(depth):
        wl = w_in[l]
        w_big = jnp.concatenate([wl[:, src[n][0]:src[n][1]] for n in big_order], axis=1).astype(BF16)
        w_small = _pad_cols(jnp.concatenate([wl[:, src[n][0]:src[n][1]] for n in ("ik", "iw", "fa")], axis=1),
                            LANES).astype(BF16)

        p = _inproj(hcur, w_big, tables, s, bn=bn, q_tiles=tiles_of("qa", "qb", "qc"),
                    rope128_tiles=tiles_of("qb", "kb", "qc", "kc"), rope64_tiles=tiles_of("iq"),
                    q_scale=HEAD_DIM ** -0.5 * LOG2E)
        small = _inproj_small(hcur, w_small, tables[2:], s)
        p3 = p.reshape(b, s, n_big)
        small3 = small.reshape(b, s, LANES)

        fa_t = jnp.swapaxes(small3[:, :, IDX_HEAD_DIM + idx_h:IDX_HEAD_DIM + idx_h + fox_h], 1, 2)
        f_cum = _forget_cumsum(fa_t, b_forget[l])
        o_a = _fox_attention(p3, f_cum, heads=fox_h, q_off=off["qa"] // HEAD_DIM, k_off=off["ka"] // HEAD_DIM,
                             v_off=off["va"] // HEAD_DIM)

        ik = small3[:, :, :IDX_HEAD_DIM].astype(BF16)
        zeros = jnp.zeros_like(ik)
        ik_lo = jnp.concatenate([ik, zeros], axis=2)
        ik_hi = jnp.concatenate([zeros, ik], axis=2)
        o_b = _dsa_attention(p3, small3, ik_lo, ik_hi, heads=dsa_h, idx_heads=idx_h, iq_off=off["iq"] // iq_w,
                             q_off=off["qb"] // dsa_w, k_off=off["kb"] // dsa_w, v_off=off["vb"] // dsa_w, topk=topk)

        lam_init = 0.8 - 0.6 * math.exp(-0.3 * l)
        o_c = _diff_attention(p3, (lam_q1[l], lam_k1[l], lam_q2[l], lam_k2[l]), g_subln[l], heads=diff_h,
                              q_off=off["qc"], k_off=off["kc"], v_off=off["vc"], lam_init=lam_init)

        merged = _merge(o_a.reshape(m, fox_w), o_b.reshape(m, dsa_w), o_c.reshape(m, diffv_w),
                        w_oa[l].astype(BF16), w_ob[l].astype(BF16), w_oc[l].astype(BF16), p, b_gate[l],
                        gate_off=off["gl"], d=d)
        x2, hcur = _proj_norm_resid(merged, w_out[l].astype(BF16), g_mix_post[l], x2, g_ffn_pre[l])

        z, u = _gate_up(hcur, _pad_cols(w_ffn_gate[l], ffp).astype(BF16), _pad_cols(w_ffn_up[l], ffp).astype(BF16))
        act = _conv_gate(z, u, _pad_cols(conv_w[l], ffp).astype(F32), _pad_cols(conv_b[l][None, :], ffp).astype(F32), s)
        w_down = jnp.pad(w_ffn_down[l], ((0, ffp - d_ff), (0, 0))).astype(BF16)
        g_next = g_mix_pre[l + 1] if l + 1 < depth else g_mix_pre[0]
        x2, hcur = _proj_norm_resid(act, w_down, g_ffn_post[l], x2, g_next)
    return x2.reshape(b, s, d)
```

```python
import functools
import math

import jax
import jax.numpy as jnp
from jax import lax
from jax.experimental import pallas as pl
from jax.experimental.pallas import tpu as pltpu

CHUNK = 64
ROPE_THETA = 10000.0
NORM_EPS = 1e-6
HEAD_DIM = 128
IDX_HEAD_DIM = 64
DIFF_V_DIM = 256
INDEX_TOPK = 256
N_BRANCH = 3
CONV_WIDTH = 3

LANES = 128
VMEM_LIMIT_BYTES = 56 * 1024 * 1024
EPILOGUE_ROWS = 64
MASK_VALUE = -1e30
INT32_MIN = -(2 ** 31)
LOG2E = math.log2(math.e)

F32 = jnp.float32
BF16 = jnp.bfloat16
_NT = (((1,), (1,)), ((), ()))


def _params(*semantics):
    return pltpu.CompilerParams(dimension_semantics=semantics, vmem_limit_bytes=VMEM_LIMIT_BYTES)


def _tile(dim, want):
    t = min(dim, want)
    assert dim % t == 0, (dim, want)
    return t


def _rmsnorm_kernel(x_ref, g_ref, o_ref):
    x = x_ref[...]
    ms = jnp.mean(x * x, axis=-1, keepdims=True)
    o_ref[...] = (x * lax.rsqrt(ms + NORM_EPS) * g_ref[...]).astype(o_ref.dtype)


def _rmsnorm(x, g):
    m, d = x.shape
    bm = _tile(m, 512)
    return pl.pallas_call(
        _rmsnorm_kernel,
        grid=(m // bm,),
        in_specs=[pl.BlockSpec((bm, d), lambda i: (i, 0)), pl.BlockSpec((1, d), lambda i: (0, 0))],
        out_specs=pl.BlockSpec((bm, d), lambda i: (i, 0)),
        out_shape=jax.ShapeDtypeStruct((m, d), BF16),
        compiler_params=_params("parallel"),
    )(x, g.reshape(1, d))


def _rope128(x, cos, sin):
    return x * cos + pltpu.roll(x, HEAD_DIM // 2, 1) * sin


def _rope64(x, cos, sin_lo, sin_hi):
    return x * cos + pltpu.roll(x, LANES - 32, 1) * sin_lo + pltpu.roll(x, 32, 1) * sin_hi


def _inproj_kernel(a_ref, w_ref, c128_ref, s128_ref, c64_ref, slo_ref, shi_ref, o_ref, *,
                   q_tiles, rope128_tiles, rope64_tiles, q_scale):
    j = pl.program_id(1)

    def member(tiles):
        hit = j < 0
        for t in tiles:
            hit = hit | (j == t)
        return hit

    acc = jnp.dot(a_ref[...], w_ref[...], preferred_element_type=F32)
    acc = acc * jnp.where(member(q_tiles), q_scale, 1.0).astype(F32)
    is128 = member(rope128_tiles)
    is64 = member(rope64_tiles)
    n_sub = acc.shape[1] // LANES

    @pl.when(is128)
    def _():
        for c in range(n_sub):
            sl = slice(c * LANES, (c + 1) * LANES)
            o_ref[:, sl] = _rope128(acc[:, sl], c128_ref[...], s128_ref[...]).astype(o_ref.dtype)

    @pl.when(is64)
    def _():
        for c in range(n_sub):
            sl = slice(c * LANES, (c + 1) * LANES)
            o_ref[:, sl] = _rope64(acc[:, sl], c64_ref[...], slo_ref[...], shi_ref[...]).astype(o_ref.dtype)

    @pl.when(jnp.logical_not(is128 | is64))
    def _():
        o_ref[...] = acc.astype(o_ref.dtype)


def _inproj(h, w, tables, seq, *, bn, q_tiles, rope128_tiles, rope64_tiles, q_scale):
    m, k = h.shape
    n = w.shape[1]
    bm = _tile(seq, 1024)
    pos_blocks = seq // bm
    tab_spec = pl.BlockSpec((bm, LANES), lambda i, j: (i % pos_blocks, 0))
    kern = functools.partial(_inproj_kernel, q_tiles=q_tiles, rope128_tiles=rope128_tiles,
                             rope64_tiles=rope64_tiles, q_scale=q_scale)
    return pl.pallas_call(
        kern,
        grid=(m // bm, n // bn),
        in_specs=[pl.BlockSpec((bm, k), lambda i, j: (i, 0)),
                  pl.BlockSpec((k, bn), lambda i, j: (0, j))] + [tab_spec] * 5,
        out_specs=pl.BlockSpec((bm, bn), lambda i, j: (i, j)),
        out_shape=jax.ShapeDtypeStruct((m, n), BF16),
        compiler_params=_params("parallel", "arbitrary"),
    )(h, w, *tables)


def _inproj_small_kernel(a_ref, w_ref, c64_ref, slo_ref, shi_ref, o_ref):
    acc = jnp.dot(a_ref[...], w_ref[...], preferred_element_type=F32)
    roped = _rope64(acc, c64_ref[...], slo_ref[...], shi_ref[...])
    lane = lax.broadcasted_iota(jnp.int32, acc.shape, 1)
    o_ref[...] = jnp.where(lane < IDX_HEAD_DIM, roped, acc)


def _inproj_small(h, w, tables64, seq):
    m, k = h.shape
    bm = _tile(seq, 1024)
    pos_blocks = seq // bm
    tab_spec = pl.BlockSpec((bm, LANES), lambda i: (i % pos_blocks, 0))
    return pl.pallas_call(
        _inproj_small_kernel,
        grid=(m // bm,),
        in_specs=[pl.BlockSpec((bm, k), lambda i: (i, 0)),
                  pl.BlockSpec((k, LANES), lambda i: (0, 0))] + [tab_spec] * 3,
        out_specs=pl.BlockSpec((bm, LANES), lambda i: (i, 0)),
        out_shape=jax.ShapeDtypeStruct((m, LANES), F32),
        compiler_params=_params("parallel"),
    )(h, w, *tables64)


def _split3(x):
    hi = x.astype(BF16)
    rem = x - hi.astype(F32)
    mid = rem.astype(BF16)
    lo = (rem - mid.astype(F32)).astype(BF16)
    return hi, mid, lo


def _forget_cumsum_kernel(fa_ref, b_ref, hi_ref, mid_ref, lo_ref):
    x = fa_ref[0] + b_ref[...]
    lf = jnp.minimum(x, 0.0) - jnp.log1p(jnp.exp(-jnp.abs(x)))
    row = lax.broadcasted_iota(jnp.int32, (LANES, LANES), 0)
    col = lax.broadcasted_iota(jnp.int32, (LANES, LANES), 1)
    tri = (row <= col).astype(BF16)
    carry = jnp.zeros((lf.shape[0], 1), F32)
    for c in range(lf.shape[1] // LANES):
        sl = slice(c * LANES, (c + 1) * LANES)
        loc = sum(jnp.dot(piece, tri, preferred_element_type=F32) for piece in _split3(lf[:, sl]))
        hi_ref[0, :, sl], mid_ref[0, :, sl], lo_ref[0, :, sl] = _split3((loc + carry) * LOG2E)
        carry = carry + loc[:, LANES - 1:LANES]


def _forget_cumsum(fa_t, b_forget):
    b, hn, s = fa_t.shape
    out_spec = pl.BlockSpec((1, hn, s), lambda i: (i, 0, 0))
    return pl.pallas_call(
        _forget_cumsum_kernel,
        grid=(b,),
        in_specs=[pl.BlockSpec((1, hn, s), lambda i: (i, 0, 0)), pl.BlockSpec((hn, 1), lambda i: (0, 0))],
        out_specs=[out_spec] * 3,
        out_shape=[jax.ShapeDtypeStruct((b, hn, s), BF16)] * 3,
        compiler_params=_params("parallel"),
    )(fa_t, b_forget.reshape(hn, 1).astype(F32))


def _online_update(s, v, m, l, acc):
    m_new = jnp.maximum(m, jnp.max(s, axis=1, keepdims=True))
    alpha = jnp.exp2(m - m_new)
    p = jnp.exp2(s - m_new)
    l_new = alpha * l + jnp.sum(p, axis=1, keepdims=True)
    acc_new = alpha * acc + jnp.dot(p.astype(v.dtype), v, preferred_element_type=F32)
    return m_new, l_new, acc_new


def _softmax_init(tq, dv):
    return (jnp.full((tq, 1), MASK_VALUE, F32), jnp.zeros((tq, 1), F32), jnp.zeros((tq, dv), F32))


def _fox_kernel(q_ref, qx_ref, k_ref, kx_ref, v_ref, o_ref, *, tq, tk, group):
    i = pl.program_id(2)
    n_full = (i * tq) // tk
    ones = jnp.ones((tk, HEAD_DIM), v_ref.dtype)

    def tile(j, carry, diagonal):
        off = pl.multiple_of(j * tk, tk)
        out = []
        for g in range(group):
            hs = slice(g * HEAD_DIM, (g + 1) * HEAD_DIM)
            m, l, acc = carry[g]
            q = jnp.concatenate([q_ref[0, :, hs], qx_ref[0, g]], axis=1)
            k = jnp.concatenate([k_ref[0, pl.ds(off, tk), hs], kx_ref[0, g, pl.ds(off, tk), :]], axis=1)
            s = lax.dot_general(q, k, _NT, preferred_element_type=F32)
            if diagonal:
                row = i * tq + lax.broadcasted_iota(jnp.int32, s.shape, 0)
                col = off + lax.broadcasted_iota(jnp.int32, s.shape, 1)
                s = jnp.where(row >= col, s, MASK_VALUE)
            m_new = jnp.maximum(m, jnp.max(s, axis=1, keepdims=True))
            alpha = jnp.exp2(m - m_new)
            p = jnp.exp2(s - m_new).astype(v_ref.dtype)
            v = jnp.concatenate([v_ref[0, pl.ds(off, tk), hs], ones], axis=1)
            pv = jnp.dot(p, v, preferred_element_type=F32)
            out.append((m_new, alpha * l + pv[:, HEAD_DIM:], alpha * acc + pv[:, :HEAD_DIM]))
        return tuple(out)

    init = tuple((jnp.full((tq, 1), MASK_VALUE, F32), jnp.zeros((tq, HEAD_DIM), F32), jnp.zeros((tq, HEAD_DIM), F32))
                 for _ in range(group))
    carry = lax.fori_loop(0, n_full, lambda j, c: tile(j, c, False), init)
    final = tile(n_full, carry, True)
    for g in range(group):
        _, l, acc = final[g]
        o_ref[0, :, g * HEAD_DIM:(g + 1) * HEAD_DIM] = (acc / l).astype(o_ref.dtype)


def _fox_attention(p3, f_cum3, *, heads, q_off, k_off, v_off):
    b, s, _ = p3.shape
    tq = _tile(s, 512)
    tk = _tile(s, 1024)
    group = math.gcd(heads, 2)
    gw = group * HEAD_DIM
    assert q_off % group == 0 and k_off % group == 0 and v_off % group == 0
    hi, mid, lo = f_cum3
    one = jnp.ones_like(hi)
    lane = jnp.arange(HEAD_DIM)

    def columns(parts):
        out = jnp.zeros((b, heads, s, HEAD_DIM), BF16)
        for n, part in enumerate(parts):
            out = jnp.where(lane == n, part[..., None], out)
        return out

    qx = columns([hi, mid, lo, one, one, one])
    kx = columns([one, one, one, -hi, -mid, -lo])
    return pl.pallas_call(
        functools.partial(_fox_kernel, tq=tq, tk=tk, group=group),
        grid=(b, heads // group, s // tq),
        in_specs=[pl.BlockSpec((1, tq, gw), lambda bi, h, i: (bi, i, q_off // group + h)),
                  pl.BlockSpec((1, group, tq, HEAD_DIM), lambda bi, h, i: (bi, h, i, 0)),
                  pl.BlockSpec((1, s, gw), lambda bi, h, i: (bi, 0, k_off // group + h)),
                  pl.BlockSpec((1, group, s, HEAD_DIM), lambda bi, h, i: (bi, h, 0, 0)),
                  pl.BlockSpec((1, s, gw), lambda bi, h, i: (bi, 0, v_off // group + h))],
        out_specs=pl.BlockSpec((1, tq, gw), lambda bi, h, i: (bi, i, h)),
        out_shape=jax.ShapeDtypeStruct((b, s, heads * HEAD_DIM), BF16),
        compiler_params=_params("parallel", "parallel", "arbitrary"),
    )(p3, qx, p3, kx, p3)


def _diff_kernel(q_ref, k_ref, v_ref, lq1_ref, lk1_ref, lq2_ref, lk2_ref, g_ref, o_ref, *, tq, tk, group, lam_init):
    i = pl.program_id(2)
    n_full = (i * tq) // tk

    def tile(j, carry, diagonal):
        off = pl.multiple_of(j * tk, tk)
        if diagonal:
            row = (i * tq + lax.broadcasted_iota(jnp.int32, (tq, tk), 0)) // CHUNK
            col = (off + lax.broadcasted_iota(jnp.int32, (tq, tk), 1)) // CHUNK
            keep = row >= col
        out = []
        for g in range(group):
            v = v_ref[0, pl.ds(off, tk), g * DIFF_V_DIM:(g + 1) * DIFF_V_DIM]
            for half in range(2):
                hs = slice((2 * g + half) * HEAD_DIM, (2 * g + half + 1) * HEAD_DIM)
                s = lax.dot_general(q_ref[0, :, hs], k_ref[0, pl.ds(off, tk), hs], _NT, preferred_element_type=F32)
                if diagonal:
                    s = jnp.where(keep, s, MASK_VALUE)
                out.append(_online_update(s, v, *carry[2 * g + half]))
        return tuple(out)

    init = tuple(_softmax_init(tq, DIFF_V_DIM) for _ in range(2 * group))
    carry = lax.fori_loop(0, n_full, lambda j, c: tile(j, c, False), init)
    final = tile(n_full, carry, True)

    lam = (jnp.exp(jnp.sum(lq1_ref[...] * lk1_ref[...], axis=1, keepdims=True))
           - jnp.exp(jnp.sum(lq2_ref[...] * lk2_ref[...], axis=1, keepdims=True)) + lam_init)
    for g in range(group):
        (_, l1, a1), (_, l2, a2) = final[2 * g], final[2 * g + 1]
        o = a1 / l1 - lam * (a2 / l2)
        ms = jnp.mean(o * o, axis=-1, keepdims=True)
        o = o * lax.rsqrt(ms + NORM_EPS) * g_ref[...]
        o_ref[0, :, g * DIFF_V_DIM:(g + 1) * DIFF_V_DIM] = (o * (1.0 - lam_init)).astype(o_ref.dtype)


def _diff_attention(p3, lam_vecs, g_subln, *, heads, q_off, k_off, v_off, lam_init):
    b, s, _ = p3.shape
    tq = _tile(s, 512)
    tk = _tile(s, 1024)
    group = 1
    gw = group * DIFF_V_DIM
    assert q_off % gw == 0 and k_off % gw == 0 and v_off % gw == 0
    vec_spec = pl.BlockSpec((1, HEAD_DIM), lambda bi, h, i: (0, 0))
    return pl.pallas_call(
        functools.partial(_diff_kernel, tq=tq, tk=tk, group=group, lam_init=lam_init),
        grid=(b, heads // group, s // tq),
        in_specs=[pl.BlockSpec((1, tq, gw), lambda bi, h, i: (bi, i, q_off // gw + h)),
                  pl.BlockSpec((1, s, gw), lambda bi, h, i: (bi, 0, k_off // gw + h)),
                  pl.BlockSpec((1, s, gw), lambda bi, h, i: (bi, 0, v_off // gw + h)),
                  vec_spec, vec_spec, vec_spec, vec_spec,
                  pl.BlockSpec((1, DIFF_V_DIM), lambda bi, h, i: (0, 0))],
        out_specs=pl.BlockSpec((1, tq, gw), lambda bi, h, i: (bi, i, h)),
        out_shape=jax.ShapeDtypeStruct((b, s, heads * DIFF_V_DIM), BF16),
        compiler_params=_params("parallel", "parallel", "arbitrary"),
    )(p3, p3, p3, *[v.reshape(1, HEAD_DIM).astype(F32) for v in lam_vecs],
      g_subln.reshape(1, DIFF_V_DIM).astype(F32))


def _ordered_key(x):
    bits = pltpu.bitcast(x, jnp.int32)
    return bits ^ (lax.shift_right_arithmetic(bits, 31) & 0x7FFFFFFF)


def _dsa_kernel(iq_ref, iw_ref, iklo_ref, ikhi_ref, q_ref, k_ref, v_ref, o_ref, keys_ref, *,
                tq, ts, tw, idx_heads, heads, topk, idx_scale):
    i = pl.program_id(1)
    n_score = (i * tq) // ts + 1
    n_wide = (i * tq) // tw + 1
    iw = iw_ref[0][:, IDX_HEAD_DIM:IDX_HEAD_DIM + idx_heads] * idx_scale
    row_chunk = (i * tq + lax.broadcasted_iota(jnp.int32, (tq, ts), 0)) // CHUNK
    col_iota = lax.broadcasted_iota(jnp.int32, (tq, ts), 1)

    def score_tile(j, _):
        off = pl.multiple_of(j * ts, ts)
        ik_lo = iklo_ref[0, pl.ds(off, ts), :]
        ik_hi = ikhi_ref[0, pl.ds(off, ts), :]
        sc = jnp.zeros((tq, ts), F32)
        for p in range(idx_heads // 2):
            a = iq_ref[0, :, p * LANES:(p + 1) * LANES]
            even = lax.dot_general(a, ik_lo, _NT, preferred_element_type=F32)
            odd = lax.dot_general(a, ik_hi, _NT, preferred_element_type=F32)
            sc = sc + jnp.maximum(even, 0.0) * iw[:, 2 * p:2 * p + 1]
            sc = sc + jnp.maximum(odd, 0.0) * iw[:, 2 * p + 1:2 * p + 2]
        valid = row_chunk >= (off + col_iota) // CHUNK
        keys_ref[:, pl.ds(off, ts)] = jnp.where(valid, _ordered_key(sc), INT32_MIN)
        return 0

    lax.fori_loop(0, n_score, score_tile, 0)

    def fill_tile(j, _):
        keys_ref[:, pl.ds(pl.multiple_of(j * ts, ts), ts)] = jnp.full((tq, ts), INT32_MIN, jnp.int32)
        return 0

    lax.fori_loop(n_score, n_wide * (tw // ts), fill_tile, 0)

    def count_ge(cand):
        def body(j, part):
            off = pl.multiple_of(j * tw, tw)
            ge = (keys_ref[:, pl.ds(off, tw)] >= cand).astype(jnp.int32)
            for c in range(tw // LANES):
                part = part + ge[:, c * LANES:(c + 1) * LANES]
            return part
        part = lax.fori_loop(0, n_wide, body, jnp.zeros((tq, LANES), jnp.int32))
        return jnp.sum(part, axis=1, keepdims=True)

    thr = jnp.where(count_ge(jnp.zeros((tq, 1), jnp.int32)) >= topk, 0, INT32_MIN).astype(jnp.int32)

    def bit_step(t, thr):
        cand = thr + lax.shift_left(jnp.int32(1), 30 - t)
        return jnp.where(count_ge(cand) >= topk, cand, thr)

    thr = lax.fori_loop(0, 31, bit_step, thr)
    thr = jnp.maximum(thr, INT32_MIN + 1)

    ones = jnp.ones((tw, HEAD_DIM), v_ref.dtype)

    def attend(j, carry):
        off = pl.multiple_of(j * tw, tw)
        out = []
        for h in range(heads):
            hs = slice(h * HEAD_DIM, (h + 1) * HEAD_DIM)
            m, l, acc = carry[h]
            s = lax.dot_general(q_ref[0, :, hs], k_ref[0, pl.ds(off, tw), hs], _NT, preferred_element_type=F32)
            s = jnp.where(keys_ref[:, pl.ds(off, tw)] >= thr, s, MASK_VALUE)
            m_new = jnp.maximum(m, jnp.max(s, axis=1, keepdims=True))
            alpha = jnp.exp2(m - m_new)
            p = jnp.exp2(s - m_new).astype(v_ref.dtype)
            v = jnp.concatenate([v_ref[0, pl.ds(off, tw), hs], ones], axis=1)
            pv = jnp.dot(p, v, preferred_element_type=F32)
            out.append((m_new, alpha * l + pv[:, HEAD_DIM:], alpha * acc + pv[:, :HEAD_DIM]))
        return tuple(out)

    init = tuple((jnp.full((tq, 1), MASK_VALUE, F32), jnp.zeros((tq, HEAD_DIM), F32), jnp.zeros((tq, HEAD_DIM), F32))
                 for _ in range(heads))
    final = lax.fori_loop(0, n_wide, attend, init)
    for h in range(heads):
        _, l, acc = final[h]
        o_ref[0, :, h * HEAD_DIM:(h + 1) * HEAD_DIM] = (acc / l).astype(o_ref.dtype)


def _dsa_attention(p3, small3, ik_lo, ik_hi, *, heads, idx_heads, iq_off, q_off, k_off, v_off, topk):
    b, s, _ = p3.shape
    tq = _tile(s, 256)
    ts = _tile(s, 512)
    tw = _tile(s, 1024)
    width = heads * HEAD_DIM
    iq_width = idx_heads * IDX_HEAD_DIM
    resident = dict(pipeline_mode=pl.Buffered(1))
    kern = functools.partial(_dsa_kernel, tq=tq, ts=ts, tw=tw, idx_heads=idx_heads, heads=heads, topk=topk,
                             idx_scale=(IDX_HEAD_DIM ** -0.5) * (idx_heads ** -0.5))
    return pl.pallas_call(
        kern,
        grid=(b, s // tq),
        in_specs=[pl.BlockSpec((1, tq, iq_width), lambda bi, i: (bi, i, iq_off)),
                  pl.BlockSpec((1, tq, LANES), lambda bi, i: (bi, i, 0)),
                  pl.BlockSpec((1, s, LANES), lambda bi, i: (bi, 0, 0), **resident),
                  pl.BlockSpec((1, s, LANES), lambda bi, i: (bi, 0, 0), **resident),
                  pl.BlockSpec((1, tq, width), lambda bi, i: (bi, i, q_off)),
                  pl.BlockSpec((1, s, width), lambda bi, i: (bi, 0, k_off), **resident),
                  pl.BlockSpec((1, s, width), lambda bi, i: (bi, 0, v_off), **resident)],
        out_specs=pl.BlockSpec((1, tq, width), lambda bi, i: (bi, i, 0)),
        out_shape=jax.ShapeDtypeStruct((b, s, width), BF16),
        scratch_shapes=[pltpu.VMEM((tq, s), jnp.int32)],
        compiler_params=_params("parallel", "arbitrary"),
    )(p3, small3, ik_lo, ik_hi, p3, p3, p3)


def _merge_kernel(oa_ref, ob_ref, oc_ref, wa_ref, wb_ref, wc_ref, g0_ref, g1_ref, g2_ref, bg_ref, o_ref):
    def branch(o_r, w_r, g_r, n):
        y = jnp.dot(o_r[...], w_r[...], preferred_element_type=F32)
        return jax.nn.sigmoid(g_r[...].astype(F32) + bg_ref[n:n + 1, :]) * y

    out = branch(oa_ref, wa_ref, g0_ref, 0) + branch(ob_ref, wb_ref, g1_ref, 1) + branch(oc_ref, wc_ref, g2_ref, 2)
    o_ref[...] = out.astype(o_ref.dtype)


def _merge(o_a, o_b, o_c, w_a, w_b, w_c, p, b_gate, *, gate_off, d):
    m = o_a.shape[0]
    bm = _tile(m, 512)
    bn = _tile(d, 1024)
    nj = d // bn

    def lhs(o):
        return pl.BlockSpec((bm, o.shape[1]), lambda i, j: (i, 0))

    def rhs(w):
        return pl.BlockSpec((w.shape[0], bn), lambda i, j: (0, j))

    def gate(n):
        return pl.BlockSpec((bm, bn), lambda i, j: (i, gate_off // bn + n * nj + j))

    assert gate_off % bn == 0
    return pl.pallas_call(
        _merge_kernel,
        grid=(m // bm, nj),
        in_specs=[lhs(o_a), lhs(o_b), lhs(o_c), rhs(w_a), rhs(w_b), rhs(w_c), gate(0), gate(1), gate(2),
                  pl.BlockSpec((N_BRANCH, bn), lambda i, j: (0, j))],
        out_specs=pl.BlockSpec((bm, bn), lambda i, j: (i, j)),
        out_shape=jax.ShapeDtypeStruct((m, d), BF16),
        compiler_params=_params("parallel", "arbitrary"),
    )(o_a, o_b, o_c, w_a, w_b, w_c, p, p, p, b_gate.astype(F32))


def _proj_norm_resid_kernel(a_ref, w_ref, g_ref, r_ref, gn_ref, x_ref, h_ref, *, n_k, bn):
    kk = pl.program_id(1)
    n = x_ref.shape[1]

    @pl.when(kk == 0)
    def _():
        x_ref[...] = jnp.zeros_like(x_ref)

    for c in range(n // bn):
        sl = slice(c * bn, (c + 1) * bn)
        x_ref[:, sl] += jnp.dot(a_ref[...], w_ref[:, sl], preferred_element_type=F32)

    @pl.when(kk == n_k - 1)
    def _():
        rows = min(x_ref.shape[0], EPILOGUE_ROWS)

        def chunk(c, _):
            rs = pl.ds(pl.multiple_of(c * rows, rows), rows)
            y = x_ref[rs, :]
            ms = jnp.mean(y * y, axis=-1, keepdims=True)
            x_new = r_ref[rs, :] + y * lax.rsqrt(ms + NORM_EPS) * g_ref[...]
            x_ref[rs, :] = x_new
            ms2 = jnp.mean(x_new * x_new, axis=-1, keepdims=True)
            h_ref[rs, :] = (x_new * lax.rsqrt(ms2 + NORM_EPS) * gn_ref[...]).astype(h_ref.dtype)
            return 0

        lax.fori_loop(0, x_ref.shape[0] // rows, chunk, 0)


def _proj_norm_resid(a, w, g_post, resid, g_next):
    m, k = a.shape
    n = w.shape[1]
    bm = _tile(m, 512)
    bk = _tile(k, 512)
    n_k = k // bk
    vec = pl.BlockSpec((1, n), lambda i, kk: (0, 0))
    row = pl.BlockSpec((bm, n), lambda i, kk: (i, 0))
    return pl.pallas_call(
        functools.partial(_proj_norm_resid_kernel, n_k=n_k, bn=_tile(n, 1024)),
        grid=(m // bm, n_k),
        in_specs=[pl.BlockSpec((bm, bk), lambda i, kk: (i, kk)),
                  pl.BlockSpec((bk, n), lambda i, kk: (kk, 0)),
                  vec, row, vec],
        out_specs=[row, row],
        out_shape=[jax.ShapeDtypeStruct((m, n), F32), jax.ShapeDtypeStruct((m, n), BF16)],
        compiler_params=_params("parallel", "arbitrary"),
    )(a, w, g_post.reshape(1, n).astype(F32), resid, g_next.reshape(1, n).astype(F32))


def _gate_up_conv_kernel(a_ref, wg_ref, wu_ref, cw_ref, cb_ref, o_ref, prev_ref, *, blocks_per_seq):
    i = pl.program_id(0)
    j = pl.program_id(1)
    a = a_ref[...]
    bm = a.shape[0]
    z = jnp.dot(a, wg_ref[...], preferred_element_type=F32)
    u = jnp.dot(a, wu_ref[...], preferred_element_type=F32)

    @pl.when((i % blocks_per_seq) == 0)
    def _():
        prev_ref[j] = jnp.zeros(prev_ref.shape[1:], F32)

    prev2 = prev_ref[j, 0:1, :]
    prev1 = prev_ref[j, 1:2, :]
    prev_ref[j, 0:2, :] = z[bm - 2:bm, :]
    row = lax.broadcasted_iota(jnp.int32, z.shape, 0)
    z1 = jnp.where(row == 0, prev1, pltpu.roll(z, 1, 0))
    z2 = jnp.where(row == 0, prev2, jnp.where(row == 1, prev1, pltpu.roll(z, 2, 0)))
    zc = cw_ref[0:1, :] * z2 + cw_ref[1:2, :] * z1 + cw_ref[2:3, :] * z + cb_ref[...]
    gelu = 0.5 * zc * (1.0 + jnp.tanh(math.sqrt(2.0 / math.pi) * (zc + 0.044715 * (zc * zc * zc))))
    o_ref[...] = (gelu * u).astype(o_ref.dtype)


def _gate_up_conv(h, w_gate, w_up, conv_w, conv_b, seq):
    m, k = h.shape
    n = w_gate.shape[1]
    bm = _tile(seq, 1024)
    bn = _tile(n, 512)
    wspec = pl.BlockSpec((k, bn), lambda i, j: (0, j))
    return pl.pallas_call(
        functools.partial(_gate_up_conv_kernel, blocks_per_seq=seq // bm),
        grid=(m // bm, n // bn),
        in_specs=[pl.BlockSpec((bm, k), lambda i, j: (i, 0)), wspec, wspec,
                  pl.BlockSpec((CONV_WIDTH, bn), lambda i, j: (0, j)),
                  pl.BlockSpec((1, bn), lambda i, j: (0, j))],
        out_specs=pl.BlockSpec((bm, bn), lambda i, j: (i, j)),
        out_shape=jax.ShapeDtypeStruct((m, n), BF16),
        scratch_shapes=[pltpu.VMEM((n // bn, 8, bn), F32)],
        compiler_params=_params("arbitrary", "arbitrary"),
    )(h, w_gate, w_up, conv_w, conv_b)


def _rope_tables(s):
    pos = jnp.arange(s, dtype=F32)

    def cos_sin(d):
        inv_freq = ROPE_THETA ** (-jnp.arange(0, d, 2, dtype=F32) / d)
        ang = pos[:, None] * inv_freq[None, :]
        return jnp.cos(ang), jnp.sin(ang)

    c, sn = cos_sin(HEAD_DIM)
    c128 = jnp.concatenate([c, c], axis=1)
    s128 = jnp.concatenate([-sn, sn], axis=1)
    ci, si = cos_sin(IDX_HEAD_DIM)
    zero = jnp.zeros_like(si)
    c64 = jnp.concatenate([ci, ci, ci, ci], axis=1)
    s_lo = jnp.concatenate([-si, zero, -si, zero], axis=1)
    s_hi = jnp.concatenate([zero, si, zero, si], axis=1)
    return c128, s128, c64, s_lo, s_hi


def _pad_cols(w, n):
    return jnp.pad(w, ((0, 0), (0, n - w.shape[1])))


def kernel(x, g_mix_pre, g_mix_post, w_in, b_forget, b_gate, lam_q1, lam_k1, lam_q2, lam_k2, g_subln, w_oa, w_ob,
           w_oc, w_out, g_ffn_pre, g_ffn_post, w_ffn_gate, w_ffn_up, conv_w, conv_b, w_ffn_down):
    b, s, d = x.shape
    depth = w_in.shape[0]
    m = b * s
    fox_w, dsa_w, diffv_w = w_oa.shape[1], w_ob.shape[1], w_oc.shape[1]
    fox_h, dsa_h, diff_h = fox_w // HEAD_DIM, dsa_w // HEAD_DIM, diffv_w // DIFF_V_DIM
    diffqk_w = diff_h * 2 * HEAD_DIM
    known = 3 * fox_w + fox_h + 3 * dsa_w + IDX_HEAD_DIM + 2 * diffqk_w + diffv_w + N_BRANCH * d
    idx_h = (w_in.shape[2] - known) // (IDX_HEAD_DIM + 1)
    iq_w = idx_h * IDX_HEAD_DIM
    d_ff = w_ffn_gate.shape[2]
    topk = min(INDEX_TOPK, s // 4)
    assert known + idx_h * (IDX_HEAD_DIM + 1) == w_in.shape[2]
    assert idx_h % 2 == 0 and IDX_HEAD_DIM + idx_h + fox_h <= LANES and s % CHUNK == 0

    sizes = dict(qa=fox_w, ka=fox_w, va=fox_w, fa=fox_h, qb=dsa_w, kb=dsa_w, vb=dsa_w, iq=iq_w, ik=IDX_HEAD_DIM,
                 iw=idx_h, qc=diffqk_w, kc=diffqk_w, vc=diffv_w, gl=N_BRANCH * d)
    src, pos = {}, 0
    for name, width in sizes.items():
        src[name] = (pos, pos + width)
        pos += width
    big_order = ("qa", "ka", "va", "qb", "kb", "vb", "iq", "qc", "kc", "vc", "gl")
    off, pos = {}, 0
    for name in big_order:
        off[name] = pos
        pos += sizes[name]
    n_big = pos
    bn = math.gcd(1024, *[sizes[name] for name in big_order])
    assert bn % (2 * LANES) == 0
    assert off["iq"] % iq_w == 0 and off["qb"] % dsa_w == 0

    def tiles_of(*names):
        return tuple(t for name in names for t in range(off[name] // bn, (off[name] + sizes[name]) // bn))

    tables = _rope_tables(s)
    ffp = -(-d_ff // 1024) * 1024 if d_ff > 1024 else d_ff

    x2 = x.reshape(m, d)
    hcur = _rmsnorm(x2, g_mix_pre[0])
    for l in range(depth):
        wl = w_in[l]
        w_big = jnp.concatenate([wl[:, src[n][0]:src[n][1]] for n in big_order], axis=1).astype(BF16)
        w_small = _pad_cols(jnp.concatenate([wl[:, src[n][0]:src[n][1]] for n in ("ik", "iw", "fa")], axis=1),
                            LANES).astype(BF16)

        p = _inproj(hcur, w_big, tables, s, bn=bn, q_tiles=tiles_of("qa", "qb", "qc"),
                    rope128_tiles=tiles_of("qb", "kb", "qc", "kc"), rope64_tiles=tiles_of("iq"),
                    q_scale=HEAD_DIM ** -0.5 * LOG2E)
        small = _inproj_small(hcur, w_small, tables[2:], s)
        p3 = p.reshape(b, s, n_big)
        small3 = small.reshape(b, s, LANES)

        fa_t = jnp.swapaxes(small3[:, :, IDX_HEAD_DIM + idx_h:IDX_HEAD_DIM + idx_h + fox_h], 1, 2)
        f_cum = _forget_cumsum(fa_t, b_forget[l])
        o_a = _fox_attention(p3, f_cum, heads=fox_h, q_off=off["qa"] // HEAD_DIM, k_off=off["ka"] // HEAD_DIM,
                             v_off=off["va"] // HEAD_DIM)

        ik = small3[:, :, :IDX_HEAD_DIM].astype(BF16)
        zeros = jnp.zeros_like(ik)
        ik_lo = jnp.concatenate([ik, zeros], axis=2)
        ik_hi = jnp.concatenate([zeros, ik], axis=2)
        o_b = _dsa_attention(p3, small3, ik_lo, ik_hi, heads=dsa_h, idx_heads=idx_h, iq_off=off["iq"] // iq_w,
                             q_off=off["qb"] // dsa_w, k_off=off["kb"] // dsa_w, v_off=off["vb"] // dsa_w, topk=topk)

        lam_init = 0.8 - 0.6 * math.exp(-0.3 * l)
        o_c = _diff_attention(p3, (lam_q1[l], lam_k1[l], lam_q2[l], lam_k2[l]), g_subln[l], heads=diff_h,
                              q_off=off["qc"], k_off=off["kc"], v_off=off["vc"], lam_init=lam_init)

        merged = _merge(o_a.reshape(m, fox_w), o_b.reshape(m, dsa_w), o_c.reshape(m, diffv_w),
                        w_oa[l].astype(BF16), w_ob[l].astype(BF16), w_oc[l].astype(BF16), p, b_gate[l],
                        gate_off=off["gl"], d=d)
        x2, hcur = _proj_norm_resid(merged, w_out[l].astype(BF16), g_mix_post[l], x2, g_ffn_pre[l])

        act = _gate_up_conv(hcur, _pad_cols(w_ffn_gate[l], ffp).astype(BF16), _pad_cols(w_ffn_up[l], ffp).astype(BF16),
                            _pad_cols(conv_w[l], ffp).astype(F32), _pad_cols(conv_b[l][None, :], ffp).astype(F32), s)
        w_down = jnp.pad(w_ffn_down[l], ((0, ffp - d_ff), (0, 0))).astype(BF16)
        g_next = g_mix_pre[l + 1] if l + 1 < depth else g_mix_pre[0]
        x2, hcur = _proj_norm_resid(act, w_down, g_ffn_post[l], x2, g_next)
    return x2.reshape(b, s, d)
```

```python
import functools
import math

import jax
import jax.numpy as jnp
from jax import lax
from jax.experimental import pallas as pl
from jax.experimental.pallas import tpu as pltpu

CHUNK = 64
ROPE_THETA = 10000.0
NORM_EPS = 1e-6
HEAD_DIM = 128
IDX_HEAD_DIM = 64
DIFF_V_DIM = 256
INDEX_TOPK = 256
N_BRANCH = 3
CONV_WIDTH = 3

LANES = 128
VMEM_LIMIT_BYTES = 56 * 1024 * 1024
EPILOGUE_ROWS = 64
MASK_VALUE = -1e30
INT32_MIN = -(2 ** 31)
LOG2E = math.log2(math.e)

F32 = jnp.float32
BF16 = jnp.bfloat16
_NT = (((1,), (1,)), ((), ()))


def _params(*semantics):
    return pltpu.CompilerParams(dimension_semantics=semantics, vmem_limit_bytes=VMEM_LIMIT_BYTES)


def _tile(dim, want):
    t = min(dim, want)
    assert dim % t == 0, (dim, want)
    return t


def _rmsnorm_kernel(x_ref, g_ref, o_ref):
    x = x_ref[...]
    ms = jnp.mean(x * x, axis=-1, keepdims=True)
    o_ref[...] = (x * lax.rsqrt(ms + NORM_EPS) * g_ref[...]).astype(o_ref.dtype)


def _rmsnorm(x, g):
    m, d = x.shape
    bm = _tile(m, 512)
    return pl.pallas_call(
        _rmsnorm_kernel,
        grid=(m // bm,),
        in_specs=[pl.BlockSpec((bm, d), lambda i: (i, 0)), pl.BlockSpec((1, d), lambda i: (0, 0))],
        out_specs=pl.BlockSpec((bm, d), lambda i: (i, 0)),
        out_shape=jax.ShapeDtypeStruct((m, d), BF16),
        compiler_params=_params("parallel"),
    )(x, g.reshape(1, d))


def _rope128(x, cos, sin):
    return x * cos + pltpu.roll(x, HEAD_DIM // 2, 1) * sin


def _rope64(x, cos, sin_lo, sin_hi):
    return x * cos + pltpu.roll(x, LANES - 32, 1) * sin_lo + pltpu.roll(x, 32, 1) * sin_hi


def _inproj_kernel(a_ref, w_ref, c128_ref, s128_ref, c64_ref, slo_ref, shi_ref, o_ref, *,
                   q_tiles, rope128_tiles, rope64_tiles, q_scale):
    j = pl.program_id(1)

    def member(tiles):
        hit = j < 0
        for t in tiles:
            hit = hit | (j == t)
        return hit

    acc = jnp.dot(a_ref[...], w_ref[...], preferred_element_type=F32)
    acc = acc * jnp.where(member(q_tiles), q_scale, 1.0).astype(F32)
    is128 = member(rope128_tiles)
    is64 = member(rope64_tiles)
    n_sub = acc.shape[1] // LANES

    @pl.when(is128)
    def _():
        for c in range(n_sub):
            sl = slice(c * LANES, (c + 1) * LANES)
            o_ref[:, sl] = _rope128(acc[:, sl], c128_ref[...], s128_ref[...]).astype(o_ref.dtype)

    @pl.when(is64)
    def _():
        for c in range(n_sub):
            sl = slice(c * LANES, (c + 1) * LANES)
            o_ref[:, sl] = _rope64(acc[:, sl], c64_ref[...], slo_ref[...], shi_ref[...]).astype(o_ref.dtype)

    @pl.when(jnp.logical_not(is128 | is64))
    def _():
        o_ref[...] = acc.astype(o_ref.dtype)


def _inproj(h, w, tables, seq, *, bn, q_tiles, rope128_tiles, rope64_tiles, q_scale):
    m, k = h.shape
    n = w.shape[1]
    bm = _tile(seq, 1024)
    pos_blocks = seq // bm
    tab_spec = pl.BlockSpec((bm, LANES), lambda i, j: (i % pos_blocks, 0))
    kern = functools.partial(_inproj_kernel, q_tiles=q_tiles, rope128_tiles=rope128_tiles,
                             rope64_tiles=rope64_tiles, q_scale=q_scale)
    return pl.pallas_call(
        kern,
        grid=(m // bm, n // bn),
        in_specs=[pl.BlockSpec((bm, k), lambda i, j: (i, 0)),
                  pl.BlockSpec((k, bn), lambda i, j: (0, j))] + [tab_spec] * 5,
        out_specs=pl.BlockSpec((bm, bn), lambda i, j: (i, j)),
        out_shape=jax.ShapeDtypeStruct((m, n), BF16),
        compiler_params=_params("parallel", "arbitrary"),
    )(h, w, *tables)


def _inproj_small_kernel(a_ref, w_ref, c64_ref, slo_ref, shi_ref, o_ref):
    acc = jnp.dot(a_ref[...], w_ref[...], preferred_element_type=F32)
    roped = _rope64(acc, c64_ref[...], slo_ref[...], shi_ref[...])
    lane = lax.broadcasted_iota(jnp.int32, acc.shape, 1)
    o_ref[...] = jnp.where(lane < IDX_HEAD_DIM, roped, acc)


def _inproj_small(h, w, tables64, seq):
    m, k = h.shape
    bm = _tile(seq, 1024)
    pos_blocks = seq // bm
    tab_spec = pl.BlockSpec((bm, LANES), lambda i: (i % pos_blocks, 0))
    return pl.pallas_call(
        _inproj_small_kernel,
        grid=(m // bm,),
        in_specs=[pl.BlockSpec((bm, k), lambda i: (i, 0)),
                  pl.BlockSpec((k, LANES), lambda i: (0, 0))] + [tab_spec] * 3,
        out_specs=pl.BlockSpec((bm, LANES), lambda i: (i, 0)),
        out_shape=jax.ShapeDtypeStruct((m, LANES), F32),
        compiler_params=_params("parallel"),
    )(h, w, *tables64)


def _split3(x):
    hi = x.astype(BF16)
    rem = x - hi.astype(F32)
    mid = rem.astype(BF16)
    lo = (rem - mid.astype(F32)).astype(BF16)
    return hi, mid, lo


def _forget_cumsum_kernel(fa_ref, b_ref, hi_ref, mid_ref, lo_ref):
    x = fa_ref[0] + b_ref[...]
    lf = jnp.minimum(x, 0.0) - jnp.log1p(jnp.exp(-jnp.abs(x)))
    row = lax.broadcasted_iota(jnp.int32, (LANES, LANES), 0)
    col = lax.broadcasted_iota(jnp.int32, (LANES, LANES), 1)
    tri = (row <= col).astype(BF16)
    carry = jnp.zeros((lf.shape[0], 1), F32)
    for c in range(lf.shape[1] // LANES):
        sl = slice(c * LANES, (c + 1) * LANES)
        loc = sum(jnp.dot(piece, tri, preferred_element_type=F32) for piece in _split3(lf[:, sl]))
        hi_ref[0, :, sl], mid_ref[0, :, sl], lo_ref[0, :, sl] = _split3((loc + carry) * LOG2E)
        carry = carry + loc[:, LANES - 1:LANES]


def _forget_cumsum(fa_t, b_forget):
    b, hn, s = fa_t.shape
    out_spec = pl.BlockSpec((1, hn, s), lambda i: (i, 0, 0))
    return pl.pallas_call(
        _forget_cumsum_kernel,
        grid=(b,),
        in_specs=[pl.BlockSpec((1, hn, s), lambda i: (i, 0, 0)), pl.BlockSpec((hn, 1), lambda i: (0, 0))],
        out_specs=[out_spec] * 3,
        out_shape=[jax.ShapeDtypeStruct((b, hn, s), BF16)] * 3,
        compiler_params=_params("parallel"),
    )(fa_t, b_forget.reshape(hn, 1).astype(F32))


def _dot_row_halves(p, v):
    h = p.shape[0] // 2
    return jnp.concatenate([jnp.dot(p[:h], v, preferred_element_type=F32),
                            jnp.dot(p[h:], v, preferred_element_type=F32)], axis=0)


def _online_update(s, v, m, l, acc):
    m_new = jnp.maximum(m, jnp.max(s, axis=1, keepdims=True))
    alpha = jnp.exp2(m - m_new)
    p = jnp.exp2(s - m_new)
    l_new = alpha * l + jnp.sum(p, axis=1, keepdims=True)
    acc_new = alpha * acc + _dot_row_halves(p.astype(v.dtype), v)
    return m_new, l_new, acc_new


def _softmax_init(tq, dv):
    return (jnp.full((tq, 1), MASK_VALUE, F32), jnp.zeros((tq, 1), F32), jnp.zeros((tq, dv), F32))


def _fox_kernel(q_ref, qx_ref, k_ref, kx_ref, v_ref, o_ref, *, tq, tk, group):
    i = pl.program_id(2)
    n_full = (i * tq) // tk
    ones = jnp.ones((tk, HEAD_DIM), v_ref.dtype)

    def tile(j, carry, diagonal):
        off = pl.multiple_of(j * tk, tk)
        out = []
        for g in range(group):
            hs = slice(g * HEAD_DIM, (g + 1) * HEAD_DIM)
            m, l, acc = carry[g]
            q = jnp.concatenate([q_ref[0, :, hs], qx_ref[0, g]], axis=1)
            k = jnp.concatenate([k_ref[0, pl.ds(off, tk), hs], kx_ref[0, g, pl.ds(off, tk), :]], axis=1)
            s = lax.dot_general(q, k, _NT, preferred_element_type=F32)
            if diagonal:
                row = i * tq + lax.broadcasted_iota(jnp.int32, s.shape, 0)
                col = off + lax.broadcasted_iota(jnp.int32, s.shape, 1)
                s = jnp.where(row >= col, s, MASK_VALUE)
            m_new = jnp.maximum(m, jnp.max(s, axis=1, keepdims=True))
            alpha = jnp.exp2(m - m_new)
            p = jnp.exp2(s - m_new).astype(v_ref.dtype)
            v = jnp.concatenate([v_ref[0, pl.ds(off, tk), hs], ones], axis=1)
            pv = _dot_row_halves(p, v)
            out.append((m_new, alpha * l + pv[:, HEAD_DIM:], alpha * acc + pv[:, :HEAD_DIM]))
        return tuple(out)

    init = tuple((jnp.full((tq, 1), MASK_VALUE, F32), jnp.zeros((tq, HEAD_DIM), F32), jnp.zeros((tq, HEAD_DIM), F32))
                 for _ in range(group))
    carry = lax.fori_loop(0, n_full, lambda j, c: tile(j, c, False), init)
    final = tile(n_full, carry, True)
    for g in range(group):
        _, l, acc = final[g]
        o_ref[0, :, g * HEAD_DIM:(g + 1) * HEAD_DIM] = (acc / l).astype(o_ref.dtype)


def _fox_attention(p3, f_cum3, *, heads, q_off, k_off, v_off):
    b, s, _ = p3.shape
    tq = _tile(s, 1024)
    tk = _tile(s, 1024)
    group = 1
    gw = group * HEAD_DIM
    assert q_off % group == 0 and k_off % group == 0 and v_off % group == 0
    hi, mid, lo = f_cum3
    one = jnp.ones_like(hi)
    lane = jnp.arange(HEAD_DIM)

    def columns(parts):
        out = jnp.zeros((b, heads, s, HEAD_DIM), BF16)
        for n, part in enumerate(parts):
            out = jnp.where(lane == n, part[..., None], out)
        return out

    qx = columns([hi, mid, lo, one, one, one])
    kx = columns([one, one, one, -hi, -mid, -lo])
    return pl.pallas_call(
        functools.partial(_fox_kernel, tq=tq, tk=tk, group=group),
        grid=(b, heads // group, s // tq),
        in_specs=[pl.BlockSpec((1, tq, gw), lambda bi, h, i: (bi, i, q_off // group + h)),
                  pl.BlockSpec((1, group, tq, HEAD_DIM), lambda bi, h, i: (bi, h, i, 0)),
                  pl.BlockSpec((1, s, gw), lambda bi, h, i: (bi, 0, k_off // group + h)),
                  pl.BlockSpec((1, group, s, HEAD_DIM), lambda bi, h, i: (bi, h, 0, 0)),
                  pl.BlockSpec((1, s, gw), lambda bi, h, i: (bi, 0, v_off // group + h))],
        out_specs=pl.BlockSpec((1, tq, gw), lambda bi, h, i: (bi, i, h)),
        out_shape=jax.ShapeDtypeStruct((b, s, heads * HEAD_DIM), BF16),
        compiler_params=_params("parallel", "parallel", "arbitrary"),
    )(p3, qx, p3, kx, p3)


def _diff_kernel(q_ref, k_ref, v_ref, lq1_ref, lk1_ref, lq2_ref, lk2_ref, g_ref, o_ref, *, tq, tk, group, lam_init):
    i = pl.program_id(2)
    n_full = (i * tq) // tk

    def tile(j, carry, diagonal):
        off = pl.multiple_of(j * tk, tk)
        if diagonal:
            row = (i * tq + lax.broadcasted_iota(jnp.int32, (tq, tk), 0)) // CHUNK
            col = (off + lax.broadcasted_iota(jnp.int32, (tq, tk), 1)) // CHUNK
            keep = row >= col
        out = []
        for g in range(group):
            v = v_ref[0, pl.ds(off, tk), g * DIFF_V_DIM:(g + 1) * DIFF_V_DIM]
            for half in range(2):
                hs = slice((2 * g + half) * HEAD_DIM, (2 * g + half + 1) * HEAD_DIM)
                s = lax.dot_general(q_ref[0, :, hs], k_ref[0, pl.ds(off, tk), hs], _NT, preferred_element_type=F32)
                if diagonal:
                    s = jnp.where(keep, s, MASK_VALUE)
                out.append(_online_update(s, v, *carry[2 * g + half]))
        return tuple(out)

    init = tuple(_softmax_init(tq, DIFF_V_DIM) for _ in range(2 * group))
    carry = lax.fori_loop(0, n_full, lambda j, c: tile(j, c, False), init)
    final = tile(n_full, carry, True)

    lam = (jnp.exp(jnp.sum(lq1_ref[...] * lk1_ref[...], axis=1, keepdims=True))
           - jnp.exp(jnp.sum(lq2_ref[...] * lk2_ref[...], axis=1, keepdims=True)) + lam_init)
    for g in range(group):
        (_, l1, a1), (_, l2, a2) = final[2 * g], final[2 * g + 1]
        o = a1 / l1 - lam * (a2 / l2)
        ms = jnp.mean(o * o, axis=-1, keepdims=True)
        o = o * lax.rsqrt(ms + NORM_EPS) * g_ref[...]
        o_ref[0, :, g * DIFF_V_DIM:(g + 1) * DIFF_V_DIM] = (o * (1.0 - lam_init)).astype(o_ref.dtype)


def _diff_attention(p3, lam_vecs, g_subln, *, heads, q_off, k_off, v_off, lam_init):
    b, s, _ = p3.shape
    tq = _tile(s, 1024)
    tk = _tile(s, 1024)
    group = 1
    gw = group * DIFF_V_DIM
    assert q_off % gw == 0 and k_off % gw == 0 and v_off % gw == 0
    vec_spec = pl.BlockSpec((1, HEAD_DIM), lambda bi, h, i: (0, 0))
    return pl.pallas_call(
        functools.partial(_diff_kernel, tq=tq, tk=tk, group=group, lam_init=lam_init),
        grid=(b, heads // group, s // tq),
        in_specs=[pl.BlockSpec((1, tq, gw), lambda bi, h, i: (bi, i, q_off // gw + h)),
                  pl.BlockSpec((1, s, gw), lambda bi, h, i: (bi, 0, k_off // gw + h)),
                  pl.BlockSpec((1, s, gw), lambda bi, h, i: (bi, 0, v_off // gw + h)),
                  vec_spec, vec_spec, vec_spec, vec_spec,
                  pl.BlockSpec((1, DIFF_V_DIM), lambda bi, h, i: (0, 0))],
        out_specs=pl.BlockSpec((1, tq, gw), lambda bi, h, i: (bi, i, h)),
        out_shape=jax.ShapeDtypeStruct((b, s, heads * DIFF_V_DIM), BF16),
        compiler_params=_params("parallel", "parallel", "arbitrary"),
    )(p3, p3, p3, *[v.reshape(1, HEAD_DIM).astype(F32) for v in lam_vecs],
      g_subln.reshape(1, DIFF_V_DIM).astype(F32))


def _ordered_key(x):
    bits = pltpu.bitcast(x, jnp.int32)
    return bits ^ (lax.shift_right_arithmetic(bits, 31) & 0x7FFFFFFF)


def _dsa_kernel(iq_ref, iw_ref, iklo_ref, ikhi_ref, q_ref, k_ref, v_ref, o_ref, keys_ref, *,
                tq, ts, tw, idx_heads, heads, topk, idx_scale):
    i = pl.program_id(1)
    n_score = (i * tq) // ts + 1
    n_wide = (i * tq) // tw + 1
    iw = iw_ref[0][:, IDX_HEAD_DIM:IDX_HEAD_DIM + idx_heads] * idx_scale
    row_chunk = (i * tq + lax.broadcasted_iota(jnp.int32, (tq, ts), 0)) // CHUNK
    col_iota = lax.broadcasted_iota(jnp.int32, (tq, ts), 1)

    def score_tile(j, _):
        off = pl.multiple_of(j * ts, ts)
        ik_lo = iklo_ref[0, pl.ds(off, ts), :]
        ik_hi = ikhi_ref[0, pl.ds(off, ts), :]
        sc = jnp.zeros((tq, ts), F32)
        for p in range(idx_heads // 2):
            a = iq_ref[0, :, p * LANES:(p + 1) * LANES]
            even = lax.dot_general(a, ik_lo, _NT, preferred_element_type=F32)
            odd = lax.dot_general(a, ik_hi, _NT, preferred_element_type=F32)
            sc = sc + jnp.maximum(even, 0.0) * iw[:, 2 * p:2 * p + 1]
            sc = sc + jnp.maximum(odd, 0.0) * iw[:, 2 * p + 1:2 * p + 2]
        valid = row_chunk >= (off + col_iota) // CHUNK
        keys_ref[:, pl.ds(off, ts)] = jnp.where(valid, _ordered_key(sc), INT32_MIN)
        return 0

    lax.fori_loop(0, n_score, score_tile, 0)

    def fill_tile(j, _):
        keys_ref[:, pl.ds(pl.multiple_of(j * ts, ts), ts)] = jnp.full((tq, ts), INT32_MIN, jnp.int32)
        return 0

    lax.fori_loop(n_score, n_wide * (tw // ts), fill_tile, 0)

    def count_ge(cand):
        def body(j, part):
            off = pl.multiple_of(j * tw, tw)
            ge = (keys_ref[:, pl.ds(off, tw)] >= cand).astype(jnp.int32)
            for c in range(tw // LANES):
                part = part + ge[:, c * LANES:(c + 1) * LANES]
            return part
        part = lax.fori_loop(0, n_wide, body, jnp.zeros((tq, LANES), jnp.int32))
        return jnp.sum(part, axis=1, keepdims=True)

    thr = jnp.where(count_ge(jnp.zeros((tq, 1), jnp.int32)) >= topk, 0, INT32_MIN).astype(jnp.int32)

    def bit_step(t, thr):
        cand = thr + lax.shift_left(jnp.int32(1), 30 - t)
        return jnp.where(count_ge(cand) >= topk, cand, thr)

    thr = lax.fori_loop(0, 31, bit_step, thr)
    thr = jnp.maximum(thr, INT32_MIN + 1)

    ones = jnp.ones((tw, HEAD_DIM), v_ref.dtype)

    def attend(j, carry):
        off = pl.multiple_of(j * tw, tw)
        out = []
        for h in range(heads):
            hs = slice(h * HEAD_DIM, (h + 1) * HEAD_DIM)
            m, l, acc = carry[h]
            s = lax.dot_general(q_ref[0, :, hs], k_ref[0, pl.ds(off, tw), hs], _NT, preferred_element_type=F32)
            s = jnp.where(keys_ref[:, pl.ds(off, tw)] >= thr, s, MASK_VALUE)
            m_new = jnp.maximum(m, jnp.max(s, axis=1, keepdims=True))
            alpha = jnp.exp2(m - m_new)
            p = jnp.exp2(s - m_new).astype(v_ref.dtype)
            v = jnp.concatenate([v_ref[0, pl.ds(off, tw), hs], ones], axis=1)
            pv = jnp.dot(p, v, preferred_element_type=F32)
            out.append((m_new, alpha * l + pv[:, HEAD_DIM:], alpha * acc + pv[:, :HEAD_DIM]))
        return tuple(out)

    init = tuple((jnp.full((tq, 1), MASK_VALUE, F32), jnp.zeros((tq, HEAD_DIM), F32), jnp.zeros((tq, HEAD_DIM), F32))
                 for _ in range(heads))
    final = lax.fori_loop(0, n_wide, attend, init)
    for h in range(heads):
        _, l, acc = final[h]
        o_ref[0, :, h * HEAD_DIM:(h + 1) * HEAD_DIM] = (acc / l).astype(o_ref.dtype)


def _dsa_attention(p3, small3, ik_lo, ik_hi, *, heads, idx_heads, iq_off, q_off, k_off, v_off, topk):
    b, s, _ = p3.shape
    tq = _tile(s, 256)
    ts = _tile(s, 512)
    tw = _tile(s, 1024)
    width = heads * HEAD_DIM
    iq_width = idx_heads * IDX_HEAD_DIM
    resident = dict(pipeline_mode=pl.Buffered(1))
    kern = functools.partial(_dsa_kernel, tq=tq, ts=ts, tw=tw, idx_heads=idx_heads, heads=heads, topk=topk,
                             idx_scale=(IDX_HEAD_DIM ** -0.5) * (idx_heads ** -0.5))
    return pl.pallas_call(
        kern,
        grid=(b, s // tq),
        in_specs=[pl.BlockSpec((1, tq, iq_width), lambda bi, i: (bi, i, iq_off)),
                  pl.BlockSpec((1, tq, LANES), lambda bi, i: (bi, i, 0)),
                  pl.BlockSpec((1, s, LANES), lambda bi, i: (bi, 0, 0), **resident),
                  pl.BlockSpec((1, s, LANES), lambda bi, i: (bi, 0, 0), **resident),
                  pl.BlockSpec((1, tq, width), lambda bi, i: (bi, i, q_off)),
                  pl.BlockSpec((1, s, width), lambda bi, i: (bi, 0, k_off), **resident),
                  pl.BlockSpec((1, s, width), lambda bi, i: (bi, 0, v_off), **resident)],
        out_specs=pl.BlockSpec((1, tq, width), lambda bi, i: (bi, i, 0)),
        out_shape=jax.ShapeDtypeStruct((b, s, width), BF16),
        scratch_shapes=[pltpu.VMEM((tq, s), jnp.int32)],
        compiler_params=_params("parallel", "arbitrary"),
    )(p3, small3, ik_lo, ik_hi, p3, p3, p3)


def _merge_kernel(oa_ref, ob_ref, oc_ref, wa_ref, wb_ref, wc_ref, g0_ref, g1_ref, g2_ref, bg_ref, o_ref):
    def branch(o_r, w_r, g_r, n):
        y = jnp.dot(o_r[...], w_r[...], preferred_element_type=F32)
        return jax.nn.sigmoid(g_r[...].astype(F32) + bg_ref[n:n + 1, :]) * y

    out = branch(oa_ref, wa_ref, g0_ref, 0) + branch(ob_ref, wb_ref, g1_ref, 1) + branch(oc_ref, wc_ref, g2_ref, 2)
    o_ref[...] = out.astype(o_ref.dtype)


def _merge(o_a, o_b, o_c, w_a, w_b, w_c, p, b_gate, *, gate_off, d):
    m = o_a.shape[0]
    bm = _tile(m, 512)
    bn = _tile(d, 1024)
    nj = d // bn

    def lhs(o):
        return pl.BlockSpec((bm, o.shape[1]), lambda i, j: (i, 0))

    def rhs(w):
        return pl.BlockSpec((w.shape[0], bn), lambda i, j: (0, j))

    def gate(n):
        return pl.BlockSpec((bm, bn), lambda i, j: (i, gate_off // bn + n * nj + j))

    assert gate_off % bn == 0
    return pl.pallas_call(
        _merge_kernel,
        grid=(m // bm, nj),
        in_specs=[lhs(o_a), lhs(o_b), lhs(o_c), rhs(w_a), rhs(w_b), rhs(w_c), gate(0), gate(1), gate(2),
                  pl.BlockSpec((N_BRANCH, bn), lambda i, j: (0, j))],
        out_specs=pl.BlockSpec((bm, bn), lambda i, j: (i, j)),
        out_shape=jax.ShapeDtypeStruct((m, d), BF16),
        compiler_params=_params("parallel", "arbitrary"),
    )(o_a, o_b, o_c, w_a, w_b, w_c, p, p, p, b_gate.astype(F32))


def _proj_norm_resid_kernel(a_ref, w_ref, g_ref, r_ref, gn_ref, x_ref, h_ref, *, n_k, bn, k_tail):
    kk = pl.program_id(1)
    n = x_ref.shape[1]

    @pl.when(kk == 0)
    def _():
        x_ref[...] = jnp.zeros_like(x_ref)

    def accumulate(tail):
        a = a_ref[...]
        if tail:
            col = lax.broadcasted_iota(jnp.int32, a.shape, 1)
            a = jnp.where(col < k_tail, a.astype(F32), 0.0).astype(a.dtype)
        for c in range(n // bn):
            sl = slice(c * bn, (c + 1) * bn)
            w = w_ref[:, sl]
            if tail:
                row = lax.broadcasted_iota(jnp.int32, w.shape, 0)
                w = jnp.where(row < k_tail, w.astype(F32), 0.0).astype(w.dtype)
            x_ref[:, sl] += jnp.dot(a, w, preferred_element_type=F32)

    if k_tail == 0:
        accumulate(False)
    else:
        pl.when(kk < n_k - 1)(lambda: accumulate(False))
        pl.when(kk == n_k - 1)(lambda: accumulate(True))

    @pl.when(kk == n_k - 1)
    def _():
        rows = min(x_ref.shape[0], EPILOGUE_ROWS)

        def chunk(c, _):
            rs = pl.ds(pl.multiple_of(c * rows, rows), rows)
            y = x_ref[rs, :]
            ms = jnp.mean(y * y, axis=-1, keepdims=True)
            x_new = r_ref[rs, :] + y * lax.rsqrt(ms + NORM_EPS) * g_ref[...]
            x_ref[rs, :] = x_new
            ms2 = jnp.mean(x_new * x_new, axis=-1, keepdims=True)
            h_ref[rs, :] = (x_new * lax.rsqrt(ms2 + NORM_EPS) * gn_ref[...]).astype(h_ref.dtype)
            return 0

        lax.fori_loop(0, x_ref.shape[0] // rows, chunk, 0)


def _proj_norm_resid(a, w, g_post, resid, g_next):
    m, k = a.shape
    n = w.shape[1]
    bm = _tile(m, 512)
    bk = min(k, 512)
    n_k = pl.cdiv(k, bk)
    vec = pl.BlockSpec((1, n), lambda i, kk: (0, 0))
    row = pl.BlockSpec((bm, n), lambda i, kk: (i, 0))
    return pl.pallas_call(
        functools.partial(_proj_norm_resid_kernel, n_k=n_k, bn=_tile(n, 1024), k_tail=k % bk),
        grid=(m // bm, n_k),
        in_specs=[pl.BlockSpec((bm, bk), lambda i, kk: (i, kk)),
                  pl.BlockSpec((bk, n), lambda i, kk: (kk, 0)),
                  vec, row, vec],
        out_specs=[row, row],
        out_shape=[jax.ShapeDtypeStruct((m, n), F32), jax.ShapeDtypeStruct((m, n), BF16)],
        compiler_params=_params("parallel", "arbitrary"),
    )(a, w, g_post.reshape(1, n).astype(F32), resid, g_next.reshape(1, n).astype(F32))


def _gate_up_conv_kernel(a_ref, wg_ref, wu_ref, cw_ref, cb_ref, o_ref, prev_ref, *, blocks_per_seq):
    i = pl.program_id(0)
    j = pl.program_id(1)
    a = a_ref[...]
    bm = a.shape[0]
    z = jnp.dot(a, wg_ref[...], preferred_element_type=F32)
    u = jnp.dot(a, wu_ref[...], preferred_element_type=F32)

    @pl.when((i % blocks_per_seq) == 0)
    def _():
        prev_ref[j] = jnp.zeros(prev_ref.shape[1:], F32)

    prev2 = prev_ref[j, 0:1, :]
    prev1 = prev_ref[j, 1:2, :]
    prev_ref[j, 0:2, :] = z[bm - 2:bm, :]
    row = lax.broadcasted_iota(jnp.int32, z.shape, 0)
    z1 = jnp.where(row == 0, prev1, pltpu.roll(z, 1, 0))
    z2 = jnp.where(row == 0, prev2, jnp.where(row == 1, prev1, pltpu.roll(z, 2, 0)))
    zc = cw_ref[0:1, :] * z2 + cw_ref[1:2, :] * z1 + cw_ref[2:3, :] * z + cb_ref[...]
    gelu = 0.5 * zc * (1.0 + jnp.tanh(math.sqrt(2.0 / math.pi) * (zc + 0.044715 * (zc * zc * zc))))
    o_ref[...] = (gelu * u).astype(o_ref.dtype)


def _gate_up_conv(h, w_gate, w_up, conv_w, conv_b, seq):
    m, k = h.shape
    n = w_gate.shape[1]
    bm = _tile(seq, 1024)
    bn = min(n, 512)
    n_j = pl.cdiv(n, bn)
    wspec = pl.BlockSpec((k, bn), lambda i, j: (0, j))
    return pl.pallas_call(
        functools.partial(_gate_up_conv_kernel, blocks_per_seq=seq // bm),
        grid=(m // bm, n_j),
        in_specs=[pl.BlockSpec((bm, k), lambda i, j: (i, 0)), wspec, wspec,
                  pl.BlockSpec((CONV_WIDTH, bn), lambda i, j: (0, j)),
                  pl.BlockSpec((1, bn), lambda i, j: (0, j))],
        out_specs=pl.BlockSpec((bm, bn), lambda i, j: (i, j)),
        out_shape=jax.ShapeDtypeStruct((m, n), BF16),
        scratch_shapes=[pltpu.VMEM((n_j, 8, bn), F32)],
        compiler_params=_params("arbitrary", "arbitrary"),
    )(h, w_gate, w_up, conv_w, conv_b)


def _rope_tables(s):
    pos = jnp.arange(s, dtype=F32)

    def cos_sin(d):
        inv_freq = ROPE_THETA ** (-jnp.arange(0, d, 2, dtype=F32) / d)
        ang = pos[:, None] * inv_freq[None, :]
        return jnp.cos(ang), jnp.sin(ang)

    c, sn = cos_sin(HEAD_DIM)
    c128 = jnp.concatenate([c, c], axis=1)
    s128 = jnp.concatenate([-sn, sn], axis=1)
    ci, si = cos_sin(IDX_HEAD_DIM)
    zero = jnp.zeros_like(si)
    c64 = jnp.concatenate([ci, ci, ci, ci], axis=1)
    s_lo = jnp.concatenate([-si, zero, -si, zero], axis=1)
    s_hi = jnp.concatenate([zero, si, zero, si], axis=1)
    return c128, s128, c64, s_lo, s_hi


def _pad_cols(w, n):
    return jnp.pad(w, ((0, 0), (0, n - w.shape[1])))


def kernel(x, g_mix_pre, g_mix_post, w_in, b_forget, b_gate, lam_q1, lam_k1, lam_q2, lam_k2, g_subln, w_oa, w_ob,
           w_oc, w_out, g_ffn_pre, g_ffn_post, w_ffn_gate, w_ffn_up, conv_w, conv_b, w_ffn_down):
    b, s, d = x.shape
    depth = w_in.shape[0]
    m = b * s
    fox_w, dsa_w, diffv_w = w_oa.shape[1], w_ob.shape[1], w_oc.shape[1]
    fox_h, dsa_h, diff_h = fox_w // HEAD_DIM, dsa_w // HEAD_DIM, diffv_w // DIFF_V_DIM
    diffqk_w = diff_h * 2 * HEAD_DIM
    known = 3 * fox_w + fox_h + 3 * dsa_w + IDX_HEAD_DIM + 2 * diffqk_w + diffv_w + N_BRANCH * d
    idx_h = (w_in.shape[2] - known) // (IDX_HEAD_DIM + 1)
    iq_w = idx_h * IDX_HEAD_DIM
    d_ff = w_ffn_gate.shape[2]
    topk = min(INDEX_TOPK, s // 4)
    assert known + idx_h * (IDX_HEAD_DIM + 1) == w_in.shape[2]
    assert idx_h % 2 == 0 and IDX_HEAD_DIM + idx_h + fox_h <= LANES and s % CHUNK == 0

    sizes = dict(qa=fox_w, ka=fox_w, va=fox_w, fa=fox_h, qb=dsa_w, kb=dsa_w, vb=dsa_w, iq=iq_w, ik=IDX_HEAD_DIM,
                 iw=idx_h, qc=diffqk_w, kc=diffqk_w, vc=diffv_w, gl=N_BRANCH * d)
    src, pos = {}, 0
    for name, width in sizes.items():
        src[name] = (pos, pos + width)
        pos += width
    big_order = ("qa", "ka", "va", "qb", "kb", "vb", "iq", "qc", "kc", "vc", "gl")
    off, pos = {}, 0
    for name in big_order:
        off[name] = pos
        pos += sizes[name]
    n_big = pos
    bn = math.gcd(1024, *[sizes[name] for name in big_order])
    assert bn % (2 * LANES) == 0
    assert off["iq"] % iq_w == 0 and off["qb"] % dsa_w == 0

    def tiles_of(*names):
        return tuple(t for name in names for t in range(off[name] // bn, (off[name] + sizes[name]) // bn))

    tables = _rope_tables(s)

    x2 = x.reshape(m, d)
    hcur = _rmsnorm(x2, g_mix_pre[0])
    for l in range(depth):
        wl = w_in[l]
        w_big = jnp.concatenate([wl[:, src[n][0]:src[n][1]] for n in big_order], axis=1).astype(BF16)
        w_small = _pad_cols(jnp.concatenate([wl[:, src[n][0]:src[n][1]] for n in ("ik", "iw", "fa")], axis=1),
                            LANES).astype(BF16)

        p = _inproj(hcur, w_big, tables, s, bn=bn, q_tiles=tiles_of("qa", "qb", "qc"),
                    rope128_tiles=tiles_of("qb", "kb", "qc", "kc"), rope64_tiles=tiles_of("iq"),
                    q_scale=HEAD_DIM ** -0.5 * LOG2E)
        small = _inproj_small(hcur, w_small, tables[2:], s)
        p3 = p.reshape(b, s, n_big)
        small3 = small.reshape(b, s, LANES)

        fa_t = jnp.swapaxes(small3[:, :, IDX_HEAD_DIM + idx_h:IDX_HEAD_DIM + idx_h + fox_h], 1, 2)
        f_cum = _forget_cumsum(fa_t, b_forget[l])
        o_a = _fox_attention(p3, f_cum, heads=fox_h, q_off=off["qa"] // HEAD_DIM, k_off=off["ka"] // HEAD_DIM,
                             v_off=off["va"] // HEAD_DIM)

        ik = small3[:, :, :IDX_HEAD_DIM].astype(BF16)
        zeros = jnp.zeros_like(ik)
        ik_lo = jnp.concatenate([ik, zeros], axis=2)
        ik_hi = jnp.concatenate([zeros, ik], axis=2)
        o_b = _dsa_attention(p3, small3, ik_lo, ik_hi, heads=dsa_h, idx_heads=idx_h, iq_off=off["iq"] // iq_w,
                             q_off=off["qb"] // dsa_w, k_off=off["kb"] // dsa_w, v_off=off["vb"] // dsa_w, topk=topk)

        lam_init = 0.8 - 0.6 * math.exp(-0.3 * l)
        o_c = _diff_attention(p3, (lam_q1[l], lam_k1[l], lam_q2[l], lam_k2[l]), g_subln[l], heads=diff_h,
                              q_off=off["qc"], k_off=off["kc"], v_off=off["vc"], lam_init=lam_init)

        merged = _merge(o_a.reshape(m, fox_w), o_b.reshape(m, dsa_w), o_c.reshape(m, diffv_w),
                        w_oa[l].astype(BF16), w_ob[l].astype(BF16), w_oc[l].astype(BF16), p, b_gate[l],
                        gate_off=off["gl"], d=d)
        x2, hcur = _proj_norm_resid(merged, w_out[l].astype(BF16), g_mix_post[l], x2, g_ffn_pre[l])

        act = _gate_up_conv(hcur, w_ffn_gate[l].astype(BF16), w_ffn_up[l].astype(BF16),
                            conv_w[l].astype(F32), conv_b[l][None, :].astype(F32), s)
        w_down = w_ffn_down[l].astype(BF16)
        g_next = g_mix_pre[l + 1] if l + 1 < depth else g_mix_pre[0]
        x2, hcur = _proj_norm_resid(act, w_down, g_ffn_post[l], x2, g_next)
    return x2.reshape(b, s, d)
```

```python
import functools
import math

import jax
import jax.numpy as jnp
from jax import lax
from jax.experimental import pallas as pl
from jax.experimental.pallas import tpu as pltpu

CHUNK = 64
ROPE_THETA = 10000.0
NORM_EPS = 1e-6
HEAD_DIM = 128
IDX_HEAD_DIM = 64
DIFF_V_DIM = 256
INDEX_TOPK = 256
N_BRANCH = 3
CONV_WIDTH = 3

LANES = 128
VMEM_LIMIT_BYTES = 56 * 1024 * 1024
EPILOGUE_ROWS = 64
MASK_VALUE = -1e30
INT32_MIN = -(2 ** 31)
LOG2E = math.log2(math.e)

F32 = jnp.float32
BF16 = jnp.bfloat16
_NT = (((1,), (1,)), ((), ()))


def _params(*semantics):
    return pltpu.CompilerParams(dimension_semantics=semantics, vmem_limit_bytes=VMEM_LIMIT_BYTES)


def _tile(dim, want):
    t = min(dim, want)
    assert dim % t == 0, (dim, want)
    return t


def _rmsnorm_kernel(x_ref, g_ref, o_ref):
    x = x_ref[...]
    ms = jnp.mean(x * x, axis=-1, keepdims=True)
    o_ref[...] = (x * lax.rsqrt(ms + NORM_EPS) * g_ref[...]).astype(o_ref.dtype)


def _rmsnorm(x, g):
    m, d = x.shape
    bm = _tile(m, 512)
    return pl.pallas_call(
        _rmsnorm_kernel,
        grid=(m // bm,),
        in_specs=[pl.BlockSpec((bm, d), lambda i: (i, 0)), pl.BlockSpec((1, d), lambda i: (0, 0))],
        out_specs=pl.BlockSpec((bm, d), lambda i: (i, 0)),
        out_shape=jax.ShapeDtypeStruct((m, d), BF16),
        compiler_params=_params("parallel"),
    )(x, g.reshape(1, d))


def _rope128(x, cos, sin):
    return x * cos + pltpu.roll(x, HEAD_DIM // 2, 1) * sin


def _rope64(x, cos, sin_lo, sin_hi):
    return x * cos + pltpu.roll(x, LANES - 32, 1) * sin_lo + pltpu.roll(x, 32, 1) * sin_hi


def _inproj_kernel(a_ref, w_ref, c128_ref, s128_ref, c64_ref, slo_ref, shi_ref, o_ref, *,
                   q_tiles, rope128_tiles, rope64_tiles, q_scale):
    j = pl.program_id(1)

    def member(tiles):
        hit = j < 0
        for t in tiles:
            hit = hit | (j == t)
        return hit

    acc = jnp.dot(a_ref[...], w_ref[...], preferred_element_type=F32)
    acc = acc * jnp.where(member(q_tiles), q_scale, 1.0).astype(F32)
    is128 = member(rope128_tiles)
    is64 = member(rope64_tiles)
    n_sub = acc.shape[1] // LANES
    o_ref[...] = acc.astype(o_ref.dtype)

    @pl.when(is128)
    def _():
        for c in range(n_sub):
            sl = slice(c * LANES, (c + 1) * LANES)
            o_ref[:, sl] = _rope128(acc[:, sl], c128_ref[...], s128_ref[...]).astype(o_ref.dtype)

    @pl.when(is64)
    def _():
        for c in range(n_sub):
            sl = slice(c * LANES, (c + 1) * LANES)
            o_ref[:, sl] = _rope64(acc[:, sl], c64_ref[...], slo_ref[...], shi_ref[...]).astype(o_ref.dtype)


def _inproj(h, w, tables, seq, *, bn, q_tiles, rope128_tiles, rope64_tiles, q_scale):
    m, k = h.shape
    n = w.shape[1]
    bm = _tile(seq, 1024)
    pos_blocks = seq // bm
    tab_spec = pl.BlockSpec((bm, LANES), lambda i, j: (i % pos_blocks, 0))
    kern = functools.partial(_inproj_kernel, q_tiles=q_tiles, rope128_tiles=rope128_tiles,
                             rope64_tiles=rope64_tiles, q_scale=q_scale)
    return pl.pallas_call(
        kern,
        grid=(m // bm, n // bn),
        in_specs=[pl.BlockSpec((bm, k), lambda i, j: (i, 0)),
                  pl.BlockSpec((k, bn), lambda i, j: (0, j))] + [tab_spec] * 5,
        out_specs=pl.BlockSpec((bm, bn), lambda i, j: (i, j)),
        out_shape=jax.ShapeDtypeStruct((m, n), BF16),
        compiler_params=_params("parallel", "arbitrary"),
    )(h, w, *tables)


def _inproj_small_kernel(a_ref, w_ref, c64_ref, slo_ref, shi_ref, o_ref):
    acc = jnp.dot(a_ref[...], w_ref[...], preferred_element_type=F32)
    roped = _rope64(acc, c64_ref[...], slo_ref[...], shi_ref[...])
    lane = lax.broadcasted_iota(jnp.int32, acc.shape, 1)
    o_ref[...] = jnp.where(lane < IDX_HEAD_DIM, roped, acc)


def _inproj_small(h, w, tables64, seq):
    m, k = h.shape
    bm = _tile(seq, 1024)
    pos_blocks = seq // bm
    tab_spec = pl.BlockSpec((bm, LANES), lambda i: (i % pos_blocks, 0))
    return pl.pallas_call(
        _inproj_small_kernel,
        grid=(m // bm,),
        in_specs=[pl.BlockSpec((bm, k), lambda i: (i, 0)),
                  pl.BlockSpec((k, LANES), lambda i: (0, 0))] + [tab_spec] * 3,
        out_specs=pl.BlockSpec((bm, LANES), lambda i: (i, 0)),
        out_shape=jax.ShapeDtypeStruct((m, LANES), F32),
        compiler_params=_params("parallel"),
    )(h, w, *tables64)


def _split3(x):
    hi = x.astype(BF16)
    rem = x - hi.astype(F32)
    mid = rem.astype(BF16)
    lo = (rem - mid.astype(F32)).astype(BF16)
    return hi, mid, lo


def _forget_cumsum_kernel(fa_ref, b_ref, hi_ref, mid_ref, lo_ref):
    x = fa_ref[0] + b_ref[...]
    lf = jnp.minimum(x, 0.0) - jnp.log1p(jnp.exp(-jnp.abs(x)))
    row = lax.broadcasted_iota(jnp.int32, (LANES, LANES), 0)
    col = lax.broadcasted_iota(jnp.int32, (LANES, LANES), 1)
    tri = (row <= col).astype(BF16)
    carry = jnp.zeros((lf.shape[0], 1), F32)
    for c in range(lf.shape[1] // LANES):
        sl = slice(c * LANES, (c + 1) * LANES)
        loc = sum(jnp.dot(piece, tri, preferred_element_type=F32) for piece in _split3(lf[:, sl]))
        hi_ref[0, :, sl], mid_ref[0, :, sl], lo_ref[0, :, sl] = _split3((loc + carry) * LOG2E)
        carry = carry + loc[:, LANES - 1:LANES]


def _forget_cumsum(fa_t, b_forget):
    b, hn, s = fa_t.shape
    out_spec = pl.BlockSpec((1, hn, s), lambda i: (i, 0, 0))
    return pl.pallas_call(
        _forget_cumsum_kernel,
        grid=(b,),
        in_specs=[pl.BlockSpec((1, hn, s), lambda i: (i, 0, 0)), pl.BlockSpec((hn, 1), lambda i: (0, 0))],
        out_specs=[out_spec] * 3,
        out_shape=[jax.ShapeDtypeStruct((b, hn, s), BF16)] * 3,
        compiler_params=_params("parallel"),
    )(fa_t, b_forget.reshape(hn, 1).astype(F32))


def _dot_row_halves(p, v):
    h = p.shape[0] // 2
    return jnp.concatenate([jnp.dot(p[:h], v, preferred_element_type=F32),
                            jnp.dot(p[h:], v, preferred_element_type=F32)], axis=0)


def _online_update(s, v, m, l, acc):
    m_new = jnp.maximum(m, jnp.max(s, axis=1, keepdims=True))
    alpha = jnp.exp2(m - m_new)
    p = jnp.exp2(s - m_new)
    l_new = alpha * l + jnp.sum(p, axis=1, keepdims=True)
    acc_new = alpha * acc + _dot_row_halves(p.astype(v.dtype), v)
    return m_new, l_new, acc_new


def _softmax_init(tq, dv):
    return (jnp.full((tq, 1), MASK_VALUE, F32), jnp.zeros((tq, 1), F32), jnp.zeros((tq, dv), F32))


def _fox_kernel(q_ref, qx_ref, k_ref, kx_ref, v_ref, o_ref, *, tq, tk, group):
    i = pl.program_id(2)
    n_full = (i * tq) // tk
    ones = jnp.ones((tk, HEAD_DIM), v_ref.dtype)

    def tile(j, carry, diagonal):
        off = pl.multiple_of(j * tk, tk)
        out = []
        for g in range(group):
            hs = slice(g * HEAD_DIM, (g + 1) * HEAD_DIM)
            m, l, acc = carry[g]
            q = jnp.concatenate([q_ref[0, :, hs], qx_ref[0, g]], axis=1)
            k = jnp.concatenate([k_ref[0, pl.ds(off, tk), hs], kx_ref[0, g, pl.ds(off, tk), :]], axis=1)
            s = lax.dot_general(q, k, _NT, preferred_element_type=F32)
            if diagonal:
                row = i * tq + lax.broadcasted_iota(jnp.int32, s.shape, 0)
                col = off + lax.broadcasted_iota(jnp.int32, s.shape, 1)
                s = jnp.where(row >= col, s, MASK_VALUE)
            m_new = jnp.maximum(m, jnp.max(s, axis=1, keepdims=True))
            alpha = jnp.exp2(m - m_new)
            p = jnp.exp2(s - m_new).astype(v_ref.dtype)
            v = jnp.concatenate([v_ref[0, pl.ds(off, tk), hs], ones], axis=1)
            pv = _dot_row_halves(p, v)
            out.append((m_new, alpha * l + pv[:, HEAD_DIM:], alpha * acc + pv[:, :HEAD_DIM]))
        return tuple(out)

    init = tuple((jnp.full((tq, 1), MASK_VALUE, F32), jnp.zeros((tq, HEAD_DIM), F32), jnp.zeros((tq, HEAD_DIM), F32))
                 for _ in range(group))
    carry = lax.fori_loop(0, n_full, lambda j, c: tile(j, c, False), init)
    final = tile(n_full, carry, True)
    for g in range(group):
        _, l, acc = final[g]
        o_ref[0, :, g * HEAD_DIM:(g + 1) * HEAD_DIM] = (acc / l).astype(o_ref.dtype)


def _fox_attention(p3, f_cum3, *, heads, q_off, k_off, v_off):
    b, s, _ = p3.shape
    tq = _tile(s, 1024)
    tk = _tile(s, 1024)
    group = 1
    gw = group * HEAD_DIM
    assert q_off % group == 0 and k_off % group == 0 and v_off % group == 0
    hi, mid, lo = f_cum3
    one = jnp.ones_like(hi)
    lane = jnp.arange(HEAD_DIM)

    def columns(parts):
        out = jnp.zeros((b, heads, s, HEAD_DIM), BF16)
        for n, part in enumerate(parts):
            out = jnp.where(lane == n, part[..., None], out)
        return out

    qx = columns([hi, mid, lo, one, one, one])
    kx = columns([one, one, one, -hi, -mid, -lo])
    return pl.pallas_call(
        functools.partial(_fox_kernel, tq=tq, tk=tk, group=group),
        grid=(b, heads // group, s // tq),
        in_specs=[pl.BlockSpec((1, tq, gw), lambda bi, h, i: (bi, i, q_off // group + h)),
                  pl.BlockSpec((1, group, tq, HEAD_DIM), lambda bi, h, i: (bi, h, i, 0)),
                  pl.BlockSpec((1, s, gw), lambda bi, h, i: (bi, 0, k_off // group + h)),
                  pl.BlockSpec((1, group, s, HEAD_DIM), lambda bi, h, i: (bi, h, 0, 0)),
                  pl.BlockSpec((1, s, gw), lambda bi, h, i: (bi, 0, v_off // group + h))],
        out_specs=pl.BlockSpec((1, tq, gw), lambda bi, h, i: (bi, i, h)),
        out_shape=jax.ShapeDtypeStruct((b, s, heads * HEAD_DIM), BF16),
        compiler_params=_params("parallel", "parallel", "arbitrary"),
    )(p3, qx, p3, kx, p3)


def _diff_kernel(q_ref, k_ref, v_ref, lq1_ref, lk1_ref, lq2_ref, lk2_ref, g_ref, o_ref, *, tq, tk, group, lam_init):
    i = pl.program_id(2)
    n_full = (i * tq) // tk

    def tile(j, carry, diagonal):
        off = pl.multiple_of(j * tk, tk)
        if diagonal:
            row = (i * tq + lax.broadcasted_iota(jnp.int32, (tq, tk), 0)) // CHUNK
            col = (off + lax.broadcasted_iota(jnp.int32, (tq, tk), 1)) // CHUNK
            keep = row >= col
        out = []
        for g in range(group):
            v = v_ref[0, pl.ds(off, tk), g * DIFF_V_DIM:(g + 1) * DIFF_V_DIM]
            for half in range(2):
                hs = slice((2 * g + half) * HEAD_DIM, (2 * g + half + 1) * HEAD_DIM)
                s = lax.dot_general(q_ref[0, :, hs], k_ref[0, pl.ds(off, tk), hs], _NT, preferred_element_type=F32)
                if diagonal:
                    s = jnp.where(keep, s, MASK_VALUE)
                out.append(_online_update(s, v, *carry[2 * g + half]))
        return tuple(out)

    init = tuple(_softmax_init(tq, DIFF_V_DIM) for _ in range(2 * group))
    carry = lax.fori_loop(0, n_full, lambda j, c: tile(j, c, False), init)
    final = tile(n_full, carry, True)

    lam = (jnp.exp(jnp.sum(lq1_ref[...] * lk1_ref[...], axis=1, keepdims=True))
           - jnp.exp(jnp.sum(lq2_ref[...] * lk2_ref[...], axis=1, keepdims=True)) + lam_init)
    for g in range(group):
        (_, l1, a1), (_, l2, a2) = final[2 * g], final[2 * g + 1]
        o = a1 / l1 - lam * (a2 / l2)
        ms = jnp.mean(o * o, axis=-1, keepdims=True)
        o = o * lax.rsqrt(ms + NORM_EPS) * g_ref[...]
        o_ref[0, :, g * DIFF_V_DIM:(g + 1) * DIFF_V_DIM] = (o * (1.0 - lam_init)).astype(o_ref.dtype)


def _diff_attention(p3, lam_vecs, g_subln, *, heads, q_off, k_off, v_off, lam_init):
    b, s, _ = p3.shape
    tq = _tile(s, 1024)
    tk = _tile(s, 1024)
    group = 1
    gw = group * DIFF_V_DIM
    assert q_off % gw == 0 and k_off % gw == 0 and v_off % gw == 0
    vec_spec = pl.BlockSpec((1, HEAD_DIM), lambda bi, h, i: (0, 0))
    return pl.pallas_call(
        functools.partial(_diff_kernel, tq=tq, tk=tk, group=group, lam_init=lam_init),
        grid=(b, heads // group, s // tq),
        in_specs=[pl.BlockSpec((1, tq, gw), lambda bi, h, i: (bi, i, q_off // gw + h)),
                  pl.BlockSpec((1, s, gw), lambda bi, h, i: (bi, 0, k_off // gw + h)),
                  pl.BlockSpec((1, s, gw), lambda bi, h, i: (bi, 0, v_off // gw + h)),
                  vec_spec, vec_spec, vec_spec, vec_spec,
                  pl.BlockSpec((1, DIFF_V_DIM), lambda bi, h, i: (0, 0))],
        out_specs=pl.BlockSpec((1, tq, gw), lambda bi, h, i: (bi, i, h)),
        out_shape=jax.ShapeDtypeStruct((b, s, heads * DIFF_V_DIM), BF16),
        compiler_params=_params("parallel", "parallel", "arbitrary"),
    )(p3, p3, p3, *[v.reshape(1, HEAD_DIM).astype(F32) for v in lam_vecs],
      g_subln.reshape(1, DIFF_V_DIM).astype(F32))


def _ordered_key(x):
    bits = pltpu.bitcast(x, jnp.int32)
    return bits ^ (lax.shift_right_arithmetic(bits, 31) & 0x7FFFFFFF)


def _dsa_kernel(iq_ref, iw_ref, iklo_ref, ikhi_ref, q_ref, k_ref, v_ref, o_ref, keys_ref, *,
                tq, ts, tw, idx_heads, heads, topk, idx_scale):
    i = pl.program_id(1)
    n_score = (i * tq) // ts + 1
    n_wide = (i * tq) // tw + 1
    iw = iw_ref[0][:, IDX_HEAD_DIM:IDX_HEAD_DIM + idx_heads] * idx_scale
    row_chunk = (i * tq + lax.broadcasted_iota(jnp.int32, (tq, ts), 0)) // CHUNK
    col_iota = lax.broadcasted_iota(jnp.int32, (tq, ts), 1)

    def score_tile(j, _):
        off = pl.multiple_of(j * ts, ts)
        ik_lo = iklo_ref[0, pl.ds(off, ts), :]
        ik_hi = ikhi_ref[0, pl.ds(off, ts), :]
        sc = jnp.zeros((tq, ts), F32)
        for p in range(idx_heads // 2):
            a = iq_ref[0, :, p * LANES:(p + 1) * LANES]
            even = lax.dot_general(a, ik_lo, _NT, preferred_element_type=F32)
            odd = lax.dot_general(a, ik_hi, _NT, preferred_element_type=F32)
            sc = sc + jnp.maximum(even, 0.0) * iw[:, 2 * p:2 * p + 1]
            sc = sc + jnp.maximum(odd, 0.0) * iw[:, 2 * p + 1:2 * p + 2]
        valid = row_chunk >= (off + col_iota) // CHUNK
        keys_ref[:, pl.ds(off, ts)] = jnp.where(valid, _ordered_key(sc), INT32_MIN)
        return 0

    lax.fori_loop(0, n_score, score_tile, 0)

    def fill_tile(j, _):
        keys_ref[:, pl.ds(pl.multiple_of(j * ts, ts), ts)] = jnp.full((tq, ts), INT32_MIN, jnp.int32)
        return 0

    lax.fori_loop(n_score, n_wide * (tw // ts), fill_tile, 0)

    def count_ge(cand):
        def body(j, part):
            off = pl.multiple_of(j * tw, tw)
            ge = (keys_ref[:, pl.ds(off, tw)] >= cand).astype(jnp.int32)
            for c in range(tw // LANES):
                part = part + ge[:, c * LANES:(c + 1) * LANES]
            return part
        part = lax.fori_loop(0, n_wide, body, jnp.zeros((tq, LANES), jnp.int32))
        return jnp.sum(part, axis=1, keepdims=True)

    thr = jnp.where(count_ge(jnp.zeros((tq, 1), jnp.int32)) >= topk, 0, INT32_MIN).astype(jnp.int32)

    def bit_step(t, thr):
        cand = thr + lax.shift_left(jnp.int32(1), 30 - t)
        return jnp.where(count_ge(cand) >= topk, cand, thr)

    thr = lax.fori_loop(0, 31, bit_step, thr)
    thr = jnp.maximum(thr, INT32_MIN + 1)

    ones = jnp.ones((tw, HEAD_DIM), v_ref.dtype)

    def attend(j, carry):
        off = pl.multiple_of(j * tw, tw)
        out = []
        for h in range(heads):
            hs = slice(h * HEAD_DIM, (h + 1) * HEAD_DIM)
            m, l, acc = carry[h]
            s = lax.dot_general(q_ref[0, :, hs], k_ref[0, pl.ds(off, tw), hs], _NT, preferred_element_type=F32)
            s = jnp.where(keys_ref[:, pl.ds(off, tw)] >= thr, s, MASK_VALUE)
            m_new = jnp.maximum(m, jnp.max(s, axis=1, keepdims=True))
            alpha = jnp.exp2(m - m_new)
            p = jnp.exp2(s - m_new).astype(v_ref.dtype)
            v = jnp.concatenate([v_ref[0, pl.ds(off, tw), hs], ones], axis=1)
            pv = jnp.dot(p, v, preferred_element_type=F32)
            out.append((m_new, alpha * l + pv[:, HEAD_DIM:], alpha * acc + pv[:, :HEAD_DIM]))
        return tuple(out)

    init = tuple((jnp.full((tq, 1), MASK_VALUE, F32), jnp.zeros((tq, HEAD_DIM), F32), jnp.zeros((tq, HEAD_DIM), F32))
                 for _ in range(heads))
    final = lax.fori_loop(0, n_wide, attend, init)
    for h in range(heads):
        _, l, acc = final[h]
        o_ref[0, :, h * HEAD_DIM:(h + 1) * HEAD_DIM] = (acc / l).astype(o_ref.dtype)


def _dsa_attention(p3, small3, ik_lo, ik_hi, *, heads, idx_heads, iq_off, q_off, k_off, v_off, topk):
    b, s, _ = p3.shape
    tq = _tile(s, 256)
    ts = _tile(s, 512)
    tw = _tile(s, 1024)
    width = heads * HEAD_DIM
    iq_width = idx_heads * IDX_HEAD_DIM
    resident = dict(pipeline_mode=pl.Buffered(1))
    kern = functools.partial(_dsa_kernel, tq=tq, ts=ts, tw=tw, idx_heads=idx_heads, heads=heads, topk=topk,
                             idx_scale=(IDX_HEAD_DIM ** -0.5) * (idx_heads ** -0.5))
    return pl.pallas_call(
        kern,
        grid=(b, s // tq),
        in_specs=[pl.BlockSpec((1, tq, iq_width), lambda bi, i: (bi, i, iq_off)),
                  pl.BlockSpec((1, tq, LANES), lambda bi, i: (bi, i, 0)),
                  pl.BlockSpec((1, s, LANES), lambda bi, i: (bi, 0, 0), **resident),
                  pl.BlockSpec((1, s, LANES), lambda bi, i: (bi, 0, 0), **resident),
                  pl.BlockSpec((1, tq, width), lambda bi, i: (bi, i, q_off)),
                  pl.BlockSpec((1, s, width), lambda bi, i: (bi, 0, k_off), **resident),
                  pl.BlockSpec((1, s, width), lambda bi, i: (bi, 0, v_off), **resident)],
        out_specs=pl.BlockSpec((1, tq, width), lambda bi, i: (bi, i, 0)),
        out_shape=jax.ShapeDtypeStruct((b, s, width), BF16),
        scratch_shapes=[pltpu.VMEM((tq, s), jnp.int32)],
        compiler_params=_params("parallel", "arbitrary"),
    )(p3, small3, ik_lo, ik_hi, p3, p3, p3)


def _merge_kernel(oa_ref, ob_ref, oc_ref, wa_ref, wb_ref, wc_ref, g0_ref, g1_ref, g2_ref, bg_ref, o_ref):
    def branch(o_r, w_r, g_r, n):
        y = jnp.dot(o_r[...], w_r[...], preferred_element_type=F32)
        return jax.nn.sigmoid(g_r[...].astype(F32) + bg_ref[n:n + 1, :]) * y

    out = branch(oa_ref, wa_ref, g0_ref, 0) + branch(ob_ref, wb_ref, g1_ref, 1) + branch(oc_ref, wc_ref, g2_ref, 2)
    o_ref[...] = out.astype(o_ref.dtype)


def _merge(o_a, o_b, o_c, w_a, w_b, w_c, p, b_gate, *, gate_off, d):
    m = o_a.shape[0]
    bm = _tile(m, 512)
    bn = _tile(d, 1024)
    nj = d // bn

    def lhs(o):
        return pl.BlockSpec((bm, o.shape[1]), lambda i, j: (i, 0))

    def rhs(w):
        return pl.BlockSpec((w.shape[0], bn), lambda i, j: (0, j))

    def gate(n):
        return pl.BlockSpec((bm, bn), lambda i, j: (i, gate_off // bn + n * nj + j))

    assert gate_off % bn == 0
    return pl.pallas_call(
        _merge_kernel,
        grid=(m // bm, nj),
        in_specs=[lhs(o_a), lhs(o_b), lhs(o_c), rhs(w_a), rhs(w_b), rhs(w_c), gate(0), gate(1), gate(2),
                  pl.BlockSpec((N_BRANCH, bn), lambda i, j: (0, j))],
        out_specs=pl.BlockSpec((bm, bn), lambda i, j: (i, j)),
        out_shape=jax.ShapeDtypeStruct((m, d), BF16),
        compiler_params=_params("parallel", "arbitrary"),
    )(o_a, o_b, o_c, w_a, w_b, w_c, p, p, p, b_gate.astype(F32))


def _proj_norm_resid_kernel(a_ref, w_ref, g_ref, r_ref, gn_ref, x_ref, h_ref, *, n_k, bn, k_tail):
    kk = pl.program_id(1)
    n = x_ref.shape[1]

    @pl.when(kk == 0)
    def _():
        x_ref[...] = jnp.zeros_like(x_ref)

    def accumulate(tail):
        a = a_ref[...]
        if tail:
            col = lax.broadcasted_iota(jnp.int32, a.shape, 1)
            a = jnp.where(col < k_tail, a.astype(F32), 0.0).astype(a.dtype)
        for c in range(n // bn):
            sl = slice(c * bn, (c + 1) * bn)
            w = w_ref[:, sl]
            if tail:
                row = lax.broadcasted_iota(jnp.int32, w.shape, 0)
                w = jnp.where(row < k_tail, w.astype(F32), 0.0).astype(w.dtype)
            x_ref[:, sl] += jnp.dot(a, w, preferred_element_type=F32)

    if k_tail == 0:
        accumulate(False)
    else:
        pl.when(kk < n_k - 1)(lambda: accumulate(False))
        pl.when(kk == n_k - 1)(lambda: accumulate(True))

    @pl.when(kk == n_k - 1)
    def _():
        rows = min(x_ref.shape[0], EPILOGUE_ROWS)

        def chunk(c, _):
            rs = pl.ds(pl.multiple_of(c * rows, rows), rows)
            y = x_ref[rs, :]
            ms = jnp.mean(y * y, axis=-1, keepdims=True)
            x_new = r_ref[rs, :] + y * lax.rsqrt(ms + NORM_EPS) * g_ref[...]
            x_ref[rs, :] = x_new
            ms2 = jnp.mean(x_new * x_new, axis=-1, keepdims=True)
            h_ref[rs, :] = (x_new * lax.rsqrt(ms2 + NORM_EPS) * gn_ref[...]).astype(h_ref.dtype)
            return 0

        lax.fori_loop(0, x_ref.shape[0] // rows, chunk, 0)


def _proj_norm_resid(a, w, g_post, resid, g_next):
    m, k = a.shape
    n = w.shape[1]
    bm = _tile(m, 512)
    bk = min(k, 512)
    n_k = pl.cdiv(k, bk)
    vec = pl.BlockSpec((1, n), lambda i, kk: (0, 0))
    row = pl.BlockSpec((bm, n), lambda i, kk: (i, 0))
    return pl.pallas_call(
        functools.partial(_proj_norm_resid_kernel, n_k=n_k, bn=_tile(n, 1024), k_tail=k % bk),
        grid=(m // bm, n_k),
        in_specs=[pl.BlockSpec((bm, bk), lambda i, kk: (i, kk)),
                  pl.BlockSpec((bk, n), lambda i, kk: (kk, 0)),
                  vec, row, vec],
        out_specs=[row, row],
        out_shape=[jax.ShapeDtypeStruct((m, n), F32), jax.ShapeDtypeStruct((m, n), BF16)],
        compiler_params=_params("parallel", "arbitrary"),
    )(a, w, g_post.reshape(1, n).astype(F32), resid, g_next.reshape(1, n).astype(F32))


def _gate_up_conv_kernel(a_ref, wg_ref, wu_ref, cw_ref, cb_ref, o_ref, prev_ref, *, blocks_per_seq):
    i = pl.program_id(0)
    j = pl.program_id(1)
    @pl.when((i % blocks_per_seq) == 0)
    def _():
        prev_ref[j] = jnp.zeros(prev_ref.shape[1:], F32)

    a = a_ref[...]
    bm = a.shape[0]
    z = jnp.dot(a, wg_ref[...], preferred_element_type=F32)
    u = jnp.dot(a, wu_ref[...], preferred_element_type=F32)
    prev2 = prev_ref[j, 0:1, :]
    prev1 = prev_ref[j, 1:2, :]
    prev_ref[j, 0:2, :] = z[bm - 2:bm, :]
    row = lax.broadcasted_iota(jnp.int32, z.shape, 0)
    z1 = jnp.where(row == 0, prev1, pltpu.roll(z, 1, 0))
    z2 = jnp.where(row == 0, prev2, jnp.where(row == 1, prev1, pltpu.roll(z, 2, 0)))
    zc = cw_ref[0:1, :] * z2 + cw_ref[1:2, :] * z1 + cw_ref[2:3, :] * z + cb_ref[...]
    gelu = 0.5 * zc * (1.0 + jnp.tanh(math.sqrt(2.0 / math.pi) * (zc + 0.044715 * (zc * zc * zc))))
    o_ref[...] = (gelu * u).astype(o_ref.dtype)


def _gate_up_conv(h, w_gate, w_up, conv_w, conv_b, seq):
    m, k = h.shape
    n = w_gate.shape[1]
    bm = _tile(seq, 1024)
    bn = min(n, 512)
    n_j = pl.cdiv(n, bn)
    wspec = pl.BlockSpec((k, bn), lambda i, j: (0, j))
    return pl.pallas_call(
        functools.partial(_gate_up_conv_kernel, blocks_per_seq=seq // bm),
        grid=(m // bm, n_j),
        in_specs=[pl.BlockSpec((bm, k), lambda i, j: (i, 0)), wspec, wspec,
                  pl.BlockSpec((CONV_WIDTH, bn), lambda i, j: (0, j)),
                  pl.BlockSpec((1, bn), lambda i, j: (0, j))],
        out_specs=pl.BlockSpec((bm, bn), lambda i, j: (i, j)),
        out_shape=jax.ShapeDtypeStruct((m, n), BF16),
        scratch_shapes=[pltpu.VMEM((n_j, 8, bn), F32)],
        compiler_params=_params("arbitrary", "arbitrary"),
    )(h, w_gate, w_up, conv_w, conv_b)


def _rope_tables(s):
    pos = jnp.arange(s, dtype=F32)

    def cos_sin(d):
        inv_freq = ROPE_THETA ** (-jnp.arange(0, d, 2, dtype=F32) / d)
        ang = pos[:, None] * inv_freq[None, :]
        return jnp.cos(ang), jnp.sin(ang)

    c, sn = cos_sin(HEAD_DIM)
    c128 = jnp.concatenate([c, c], axis=1)
    s128 = jnp.concatenate([-sn, sn], axis=1)
    ci, si = cos_sin(IDX_HEAD_DIM)
    zero = jnp.zeros_like(si)
    c64 = jnp.concatenate([ci, ci, ci, ci], axis=1)
    s_lo = jnp.concatenate([-si, zero, -si, zero], axis=1)
    s_hi = jnp.concatenate([zero, si, zero, si], axis=1)
    return c128, s128, c64, s_lo, s_hi


WEIGHT_BLOCK_BYTES = 4 * 1024 * 1024


def _row_block(rows, row_bytes):
    br = 16
    while br * 2 <= rows and rows % (br * 2) == 0 and br * 2 * row_bytes <= WEIGHT_BLOCK_BYTES:
        br *= 2
    return min(br, rows)


def _cast_kernel(x_ref, o_ref):
    o_ref[...] = x_ref[0].astype(o_ref.dtype)


def _cast_weight(w, layer):
    _, k, n = w.shape
    br = _row_block(k, n * 4)
    return pl.pallas_call(
        _cast_kernel,
        grid=(k // br,),
        in_specs=[pl.BlockSpec((1, br, n), lambda i: (layer, i, 0))],
        out_specs=pl.BlockSpec((br, n), lambda i: (i, 0)),
        out_shape=jax.ShapeDtypeStruct((k, n), BF16),
        compiler_params=_params("parallel"),
    )(w)


def _regroup_kernel(x_ref, big_ref, small_ref, *, big_segments, small_segments):
    for src, dst, width in big_segments:
        big_ref[:, dst:dst + width] = x_ref[0, :, src:src + width].astype(big_ref.dtype)
    small_ref[...] = jnp.zeros_like(small_ref)
    for src, dst, width in small_segments:
        small_ref[:, dst:dst + width] = x_ref[0, :, src:src + width].astype(small_ref.dtype)


def _regroup_in_weight(w_in, layer, big_segments, small_segments, n_big):
    _, k, n = w_in.shape
    br = _row_block(k, n * 4)
    return pl.pallas_call(
        functools.partial(_regroup_kernel, big_segments=big_segments, small_segments=small_segments),
        grid=(k // br,),
        in_specs=[pl.BlockSpec((1, br, n), lambda i: (layer, i, 0))],
        out_specs=[pl.BlockSpec((br, n_big), lambda i: (i, 0)), pl.BlockSpec((br, LANES), lambda i: (i, 0))],
        out_shape=[jax.ShapeDtypeStruct((k, n_big), BF16), jax.ShapeDtypeStruct((k, LANES), BF16)],
        compiler_params=_params("parallel"),
    )(w_in)


def kernel(x, g_mix_pre, g_mix_post, w_in, b_forget, b_gate, lam_q1, lam_k1, lam_q2, lam_k2, g_subln, w_oa, w_ob,
           w_oc, w_out, g_ffn_pre, g_ffn_post, w_ffn_gate, w_ffn_up, conv_w, conv_b, w_ffn_down):
    b, s, d = x.shape
    depth = w_in.shape[0]
    m = b * s
    fox_w, dsa_w, diffv_w = w_oa.shape[1], w_ob.shape[1], w_oc.shape[1]
    fox_h, dsa_h, diff_h = fox_w // HEAD_DIM, dsa_w // HEAD_DIM, diffv_w // DIFF_V_DIM
    diffqk_w = diff_h * 2 * HEAD_DIM
    known = 3 * fox_w + fox_h + 3 * dsa_w + IDX_HEAD_DIM + 2 * diffqk_w + diffv_w + N_BRANCH * d
    idx_h = (w_in.shape[2] - known) // (IDX_HEAD_DIM + 1)
    iq_w = idx_h * IDX_HEAD_DIM
    d_ff = w_ffn_gate.shape[2]
    topk = min(INDEX_TOPK, s // 4)
    assert known + idx_h * (IDX_HEAD_DIM + 1) == w_in.shape[2]
    assert idx_h % 2 == 0 and IDX_HEAD_DIM + idx_h + fox_h <= LANES and s % CHUNK == 0

    sizes = dict(qa=fox_w, ka=fox_w, va=fox_w, fa=fox_h, qb=dsa_w, kb=dsa_w, vb=dsa_w, iq=iq_w, ik=IDX_HEAD_DIM,
                 iw=idx_h, qc=diffqk_w, kc=diffqk_w, vc=diffv_w, gl=N_BRANCH * d)
    src, pos = {}, 0
    for name, width in sizes.items():
        src[name] = (pos, pos + width)
        pos += width
    big_order = ("qa", "ka", "va", "qb", "kb", "vb", "iq", "qc", "kc", "vc", "gl")
    off, pos = {}, 0
    for name in big_order:
        off[name] = pos
        pos += sizes[name]
    n_big = pos
    bn = math.gcd(1024, *[sizes[name] for name in big_order])
    assert bn % (2 * LANES) == 0
    assert off["iq"] % iq_w == 0 and off["qb"] % dsa_w == 0
    big_segments = tuple((src[name][0], off[name], sizes[name]) for name in big_order)
    small_segments = ((src["ik"][0], 0, IDX_HEAD_DIM), (src["iw"][0], IDX_HEAD_DIM, idx_h),
                      (src["fa"][0], IDX_HEAD_DIM + idx_h, fox_h))

    def tiles_of(*names):
        return tuple(t for name in names for t in range(off[name] // bn, (off[name] + sizes[name]) // bn))

    tables = _rope_tables(s)

    x2 = x.reshape(m, d)
    hcur = _rmsnorm(x2, g_mix_pre[0])
    for l in range(depth):
        w_big, w_small = _regroup_in_weight(w_in, l, big_segments, small_segments, n_big)

        p = _inproj(hcur, w_big, tables, s, bn=bn, q_tiles=tiles_of("qa", "qb", "qc"),
                    rope128_tiles=tiles_of("qb", "kb", "qc", "kc"), rope64_tiles=tiles_of("iq"),
                    q_scale=HEAD_DIM ** -0.5 * LOG2E)
        small = _inproj_small(hcur, w_small, tables[2:], s)
        p3 = p.reshape(b, s, n_big)
        small3 = small.reshape(b, s, LANES)

        fa_t = jnp.swapaxes(small3[:, :, IDX_HEAD_DIM + idx_h:IDX_HEAD_DIM + idx_h + fox_h], 1, 2)
        f_cum = _forget_cumsum(fa_t, b_forget[l])
        o_a = _fox_attention(p3, f_cum, heads=fox_h, q_off=off["qa"] // HEAD_DIM, k_off=off["ka"] // HEAD_DIM,
                             v_off=off["va"] // HEAD_DIM)

        ik = small3[:, :, :IDX_HEAD_DIM].astype(BF16)
        zeros = jnp.zeros_like(ik)
        ik_lo = jnp.concatenate([ik, zeros], axis=2)
        ik_hi = jnp.concatenate([zeros, ik], axis=2)
        o_b = _dsa_attention(p3, small3, ik_lo, ik_hi, heads=dsa_h, idx_heads=idx_h, iq_off=off["iq"] // iq_w,
                             q_off=off["qb"] // dsa_w, k_off=off["kb"] // dsa_w, v_off=off["vb"] // dsa_w, topk=topk)

        lam_init = 0.8 - 0.6 * math.exp(-0.3 * l)
        o_c = _diff_attention(p3, (lam_q1[l], lam_k1[l], lam_q2[l], lam_k2[l]), g_subln[l], heads=diff_h,
                              q_off=off["qc"], k_off=off["kc"], v_off=off["vc"], lam_init=lam_init)

        merged = _merge(o_a.reshape(m, fox_w), o_b.reshape(m, dsa_w), o_c.reshape(m, diffv_w),
                        _cast_weight(w_oa, l), _cast_weight(w_ob, l), _cast_weight(w_oc, l), p, b_gate[l],
                        gate_off=off["gl"], d=d)
        x2, hcur = _proj_norm_resid(merged, _cast_weight(w_out, l), g_mix_post[l], x2, g_ffn_pre[l])

        act = _gate_up_conv(hcur, _cast_weight(w_ffn_gate, l), _cast_weight(w_ffn_up, l),
                            conv_w[l].astype(F32), conv_b[l][None, :].astype(F32), s)
        w_down = _cast_weight(w_ffn_down, l)
        g_next = g_mix_pre[l + 1] if l + 1 < depth else g_mix_pre[0]
        x2, hcur = _proj_norm_resid(act, w_down, g_ffn_post[l], x2, g_next)
    return x2.reshape(b, s, d)
```

```python
import functools
import math

import jax
import jax.numpy as jnp
from jax import lax
from jax.experimental import pallas as pl
from jax.experimental.pallas import tpu as pltpu

CHUNK = 64
ROPE_THETA = 10000.0
NORM_EPS = 1e-6
HEAD_DIM = 128
IDX_HEAD_DIM = 64
DIFF_V_DIM = 256
INDEX_TOPK = 256
N_BRANCH = 3
CONV_WIDTH = 3

LANES = 128
VMEM_LIMIT_BYTES = 56 * 1024 * 1024
EPILOGUE_ROWS = 64
MASK_VALUE = -1e30
INT32_MIN = -(2 ** 31)
LOG2E = math.log2(math.e)

F32 = jnp.float32
BF16 = jnp.bfloat16
_NT = (((1,), (1,)), ((), ()))


def _params(*semantics):
    return pltpu.CompilerParams(dimension_semantics=semantics, vmem_limit_bytes=VMEM_LIMIT_BYTES)


def _tile(dim, want):
    t = min(dim, want)
    assert dim % t == 0, (dim, want)
    return t


def _rmsnorm_kernel(x_ref, g_ref, o_ref):
    x = x_ref[...]
    ms = jnp.mean(x * x, axis=-1, keepdims=True)
    o_ref[...] = (x * lax.rsqrt(ms + NORM_EPS) * g_ref[...]).astype(o_ref.dtype)


def _rmsnorm(x, g):
    m, d = x.shape
    bm = _tile(m, 512)
    return pl.pallas_call(
        _rmsnorm_kernel,
        grid=(m // bm,),
        in_specs=[pl.BlockSpec((bm, d), lambda i: (i, 0)), pl.BlockSpec((1, d), lambda i: (0, 0))],
        out_specs=pl.BlockSpec((bm, d), lambda i: (i, 0)),
        out_shape=jax.ShapeDtypeStruct((m, d), BF16),
        compiler_params=_params("parallel"),
    )(x, g.reshape(1, d))


def _rope128(x, cos, sin):
    return x * cos + pltpu.roll(x, HEAD_DIM // 2, 1) * sin


def _rope64(x, cos, sin_lo, sin_hi):
    return x * cos + pltpu.roll(x, LANES - 32, 1) * sin_lo + pltpu.roll(x, 32, 1) * sin_hi


def _inproj_kernel(a_ref, w_ref, c128_ref, s128_ref, c64_ref, slo_ref, shi_ref, o_ref, *,
                   q_tiles, rope128_tiles, rope64_tiles, q_scale):
    j = pl.program_id(1)

    def member(tiles):
        hit = j < 0
        for t in tiles:
            hit = hit | (j == t)
        return hit

    acc = jnp.dot(a_ref[...], w_ref[...], preferred_element_type=F32)
    acc = acc * jnp.where(member(q_tiles), q_scale, 1.0).astype(F32)
    is128 = member(rope128_tiles)
    is64 = member(rope64_tiles)
    n_sub = acc.shape[1] // LANES
    o_ref[...] = acc.astype(o_ref.dtype)

    @pl.when(is128)
    def _():
        for c in range(n_sub):
            sl = slice(c * LANES, (c + 1) * LANES)
            o_ref[:, sl] = _rope128(acc[:, sl], c128_ref[...], s128_ref[...]).astype(o_ref.dtype)

    @pl.when(is64)
    def _():
        for c in range(n_sub):
            sl = slice(c * LANES, (c + 1) * LANES)
            o_ref[:, sl] = _rope64(acc[:, sl], c64_ref[...], slo_ref[...], shi_ref[...]).astype(o_ref.dtype)


def _inproj(h, w, tables, seq, *, bn, q_tiles, rope128_tiles, rope64_tiles, q_scale):
    m, k = h.shape
    n = w.shape[1]
    bm = _tile(seq, 1024)
    pos_blocks = seq // bm
    tab_spec = pl.BlockSpec((bm, LANES), lambda i, j: (i % pos_blocks, 0))
    kern = functools.partial(_inproj_kernel, q_tiles=q_tiles, rope128_tiles=rope128_tiles,
                             rope64_tiles=rope64_tiles, q_scale=q_scale)
    return pl.pallas_call(
        kern,
        grid=(m // bm, n // bn),
        in_specs=[pl.BlockSpec((bm, k), lambda i, j: (i, 0)),
                  pl.BlockSpec((k, bn), lambda i, j: (0, j))] + [tab_spec] * 5,
        out_specs=pl.BlockSpec((bm, bn), lambda i, j: (i, j)),
        out_shape=jax.ShapeDtypeStruct((m, n), BF16),
        compiler_params=_params("parallel", "arbitrary"),
    )(h, w, *tables)


def _inproj_small_kernel(a_ref, w_ref, c64_ref, slo_ref, shi_ref, o_ref):
    acc = jnp.dot(a_ref[...], w_ref[...], preferred_element_type=F32)
    roped = _rope64(acc, c64_ref[...], slo_ref[...], shi_ref[...])
    lane = lax.broadcasted_iota(jnp.int32, acc.shape, 1)
    o_ref[...] = jnp.where(lane < IDX_HEAD_DIM, roped, acc)


def _inproj_small(h, w, tables64, seq):
    m, k = h.shape
    bm = _tile(seq, 1024)
    pos_blocks = seq // bm
    tab_spec = pl.BlockSpec((bm, LANES), lambda i: (i % pos_blocks, 0))
    return pl.pallas_call(
        _inproj_small_kernel,
        grid=(m // bm,),
        in_specs=[pl.BlockSpec((bm, k), lambda i: (i, 0)),
                  pl.BlockSpec((k, LANES), lambda i: (0, 0))] + [tab_spec] * 3,
        out_specs=pl.BlockSpec((bm, LANES), lambda i: (i, 0)),
        out_shape=jax.ShapeDtypeStruct((m, LANES), F32),
        compiler_params=_params("parallel"),
    )(h, w, *tables64)


def _split3(x):
    hi = x.astype(BF16)
    rem = x - hi.astype(F32)
    mid = rem.astype(BF16)
    lo = (rem - mid.astype(F32)).astype(BF16)
    return hi, mid, lo


def _forget_cumsum_kernel(fa_ref, b_ref, hi_ref, mid_ref, lo_ref):
    x = fa_ref[0] + b_ref[...]
    lf = jnp.minimum(x, 0.0) - jnp.log1p(jnp.exp(-jnp.abs(x)))
    row = lax.broadcasted_iota(jnp.int32, (LANES, LANES), 0)
    col = lax.broadcasted_iota(jnp.int32, (LANES, LANES), 1)
    tri = (row <= col).astype(BF16)
    carry = jnp.zeros((lf.shape[0], 1), F32)
    for c in range(lf.shape[1] // LANES):
        sl = slice(c * LANES, (c + 1) * LANES)
        loc = sum(jnp.dot(piece, tri, preferred_element_type=F32) for piece in _split3(lf[:, sl]))
        hi_ref[0, :, sl], mid_ref[0, :, sl], lo_ref[0, :, sl] = _split3((loc + carry) * LOG2E)
        carry = carry + loc[:, LANES - 1:LANES]


def _forget_cumsum(fa_t, b_forget):
    b, hn, s = fa_t.shape
    out_spec = pl.BlockSpec((1, hn, s), lambda i: (i, 0, 0))
    return pl.pallas_call(
        _forget_cumsum_kernel,
        grid=(b,),
        in_specs=[pl.BlockSpec((1, hn, s), lambda i: (i, 0, 0)), pl.BlockSpec((hn, 1), lambda i: (0, 0))],
        out_specs=[out_spec] * 3,
        out_shape=[jax.ShapeDtypeStruct((b, hn, s), BF16)] * 3,
        compiler_params=_params("parallel"),
    )(fa_t, b_forget.reshape(hn, 1).astype(F32))


def _dot_row_halves(p, v):
    h = p.shape[0] // 2
    return jnp.concatenate([jnp.dot(p[:h], v, preferred_element_type=F32),
                            jnp.dot(p[h:], v, preferred_element_type=F32)], axis=0)


def _online_update(s, v, m, l, acc):
    m_new = jnp.maximum(m, jnp.max(s, axis=1, keepdims=True))
    alpha = jnp.exp2(m - m_new)
    p = jnp.exp2(s - m_new)
    l_new = alpha * l + jnp.sum(p, axis=1, keepdims=True)
    acc_new = alpha * acc + _dot_row_halves(p.astype(v.dtype), v)
    return m_new, l_new, acc_new


def _softmax_init(tq, dv):
    return (jnp.full((tq, 1), MASK_VALUE, F32), jnp.zeros((tq, 1), F32), jnp.zeros((tq, dv), F32))


def _fox_kernel(q_ref, qx_ref, k_ref, kx_ref, v_ref, o_ref, *, tq, tk, group):
    i = pl.program_id(2)
    n_full = (i * tq) // tk
    ones = jnp.ones((tk, HEAD_DIM), v_ref.dtype)

    def tile(j, carry, diagonal):
        off = pl.multiple_of(j * tk, tk)
        out = []
        for g in range(group):
            hs = slice(g * HEAD_DIM, (g + 1) * HEAD_DIM)
            m, l, acc = carry[g]
            q = jnp.concatenate([q_ref[0, :, hs], qx_ref[0, g]], axis=1)
            k = jnp.concatenate([k_ref[0, pl.ds(off, tk), hs], kx_ref[0, g, pl.ds(off, tk), :]], axis=1)
            s = lax.dot_general(q, k, _NT, preferred_element_type=F32)
            if diagonal:
                row = i * tq + lax.broadcasted_iota(jnp.int32, s.shape, 0)
                col = off + lax.broadcasted_iota(jnp.int32, s.shape, 1)
                s = jnp.where(row >= col, s, MASK_VALUE)
            m_new = jnp.maximum(m, jnp.max(s, axis=1, keepdims=True))
            alpha = jnp.exp2(m - m_new)
            p = jnp.exp2(s - m_new).astype(v_ref.dtype)
            v = jnp.concatenate([v_ref[0, pl.ds(off, tk), hs], ones], axis=1)
            pv = _dot_row_halves(p, v)
            out.append((m_new, alpha * l + pv[:, HEAD_DIM:], alpha * acc + pv[:, :HEAD_DIM]))
        return tuple(out)

    init = tuple((jnp.full((tq, 1), MASK_VALUE, F32), jnp.zeros((tq, HEAD_DIM), F32), jnp.zeros((tq, HEAD_DIM), F32))
                 for _ in range(group))
    carry = lax.fori_loop(0, n_full, lambda j, c: tile(j, c, False), init)
    final = tile(n_full, carry, True)
    for g in range(group):
        _, l, acc = final[g]
        o_ref[0, :, g * HEAD_DIM:(g + 1) * HEAD_DIM] = (acc / l).astype(o_ref.dtype)


def _fox_attention(p3, f_cum3, *, heads, q_off, k_off, v_off):
    b, s, _ = p3.shape
    tq = _tile(s, 1024)
    tk = _tile(s, 1024)
    group = 1
    gw = group * HEAD_DIM
    assert q_off % group == 0 and k_off % group == 0 and v_off % group == 0
    hi, mid, lo = f_cum3
    one = jnp.ones_like(hi)
    lane = jnp.arange(HEAD_DIM)

    def columns(parts):
        out = jnp.zeros((b, heads, s, HEAD_DIM), BF16)
        for n, part in enumerate(parts):
            out = jnp.where(lane == n, part[..., None], out)
        return out

    qx = columns([hi, mid, lo, one, one, one])
    kx = columns([one, one, one, -hi, -mid, -lo])
    return pl.pallas_call(
        functools.partial(_fox_kernel, tq=tq, tk=tk, group=group),
        grid=(b, heads // group, s // tq),
        in_specs=[pl.BlockSpec((1, tq, gw), lambda bi, h, i: (bi, i, q_off // group + h)),
                  pl.BlockSpec((1, group, tq, HEAD_DIM), lambda bi, h, i: (bi, h, i, 0)),
                  pl.BlockSpec((1, s, gw), lambda bi, h, i: (bi, 0, k_off // group + h)),
                  pl.BlockSpec((1, group, s, HEAD_DIM), lambda bi, h, i: (bi, h, 0, 0)),
                  pl.BlockSpec((1, s, gw), lambda bi, h, i: (bi, 0, v_off // group + h))],
        out_specs=pl.BlockSpec((1, tq, gw), lambda bi, h, i: (bi, i, h)),
        out_shape=jax.ShapeDtypeStruct((b, s, heads * HEAD_DIM), BF16),
        compiler_params=_params("parallel", "parallel", "arbitrary"),
    )(p3, qx, p3, kx, p3)


def _diff_kernel(q_ref, k_ref, v_ref, lq1_ref, lk1_ref, lq2_ref, lk2_ref, g_ref, o_ref, *, tq, tk, group, lam_init):
    i = pl.program_id(2)
    n_full = (i * tq) // tk

    def tile(j, carry, diagonal):
        off = pl.multiple_of(j * tk, tk)
        if diagonal:
            row = (i * tq + lax.broadcasted_iota(jnp.int32, (tq, tk), 0)) // CHUNK
            col = (off + lax.broadcasted_iota(jnp.int32, (tq, tk), 1)) // CHUNK
            keep = row >= col
        out = []
        for g in range(group):
            v = v_ref[0, pl.ds(off, tk), g * DIFF_V_DIM:(g + 1) * DIFF_V_DIM]
            for half in range(2):
                hs = slice((2 * g + half) * HEAD_DIM, (2 * g + half + 1) * HEAD_DIM)
                s = lax.dot_general(q_ref[0, :, hs], k_ref[0, pl.ds(off, tk), hs], _NT, preferred_element_type=F32)
                if diagonal:
                    s = jnp.where(keep, s, MASK_VALUE)
                out.append(_online_update(s, v, *carry[2 * g + half]))
        return tuple(out)

    init = tuple(_softmax_init(tq, DIFF_V_DIM) for _ in range(2 * group))
    carry = lax.fori_loop(0, n_full, lambda j, c: tile(j, c, False), init)
    final = tile(n_full, carry, True)

    lam = (jnp.exp(jnp.sum(lq1_ref[...] * lk1_ref[...], axis=1, keepdims=True))
           - jnp.exp(jnp.sum(lq2_ref[...] * lk2_ref[...], axis=1, keepdims=True)) + lam_init)
    for g in range(group):
        (_, l1, a1), (_, l2, a2) = final[2 * g], final[2 * g + 1]
        o = a1 / l1 - lam * (a2 / l2)
        ms = jnp.mean(o * o, axis=-1, keepdims=True)
        o = o * lax.rsqrt(ms + NORM_EPS) * g_ref[...]
        o_ref[0, :, g * DIFF_V_DIM:(g + 1) * DIFF_V_DIM] = (o * (1.0 - lam_init)).astype(o_ref.dtype)


def _diff_attention(p3, lam_vecs, g_subln, *, heads, q_off, k_off, v_off, lam_init):
    b, s, _ = p3.shape
    tq = _tile(s, 1024)
    tk = _tile(s, 1024)
    group = 1
    gw = group * DIFF_V_DIM
    assert q_off % gw == 0 and k_off % gw == 0 and v_off % gw == 0
    vec_spec = pl.BlockSpec((1, HEAD_DIM), lambda bi, h, i: (0, 0))
    return pl.pallas_call(
        functools.partial(_diff_kernel, tq=tq, tk=tk, group=group, lam_init=lam_init),
        grid=(b, heads // group, s // tq),
        in_specs=[pl.BlockSpec((1, tq, gw), lambda bi, h, i: (bi, i, q_off // gw + h)),
                  pl.BlockSpec((1, s, gw), lambda bi, h, i: (bi, 0, k_off // gw + h)),
                  pl.BlockSpec((1, s, gw), lambda bi, h, i: (bi, 0, v_off // gw + h)),
                  vec_spec, vec_spec, vec_spec, vec_spec,
                  pl.BlockSpec((1, DIFF_V_DIM), lambda bi, h, i: (0, 0))],
        out_specs=pl.BlockSpec((1, tq, gw), lambda bi, h, i: (bi, i, h)),
        out_shape=jax.ShapeDtypeStruct((b, s, heads * DIFF_V_DIM), BF16),
        compiler_params=_params("parallel", "parallel", "arbitrary"),
    )(p3, p3, p3, *[v.reshape(1, HEAD_DIM).astype(F32) for v in lam_vecs],
      g_subln.reshape(1, DIFF_V_DIM).astype(F32))


def _ordered_key(x):
    bits = pltpu.bitcast(x, jnp.int32)
    return bits ^ (lax.shift_right_arithmetic(bits, 31) & 0x7FFFFFFF)


def _dsa_kernel(iq_ref, iw_ref, iklo_ref, ikhi_ref, q_ref, k_ref, v_ref, o_ref, keys_ref, *,
                tq, ts, tw, idx_heads, heads, topk, idx_scale):
    i = pl.program_id(1)
    n_score = (i * tq) // ts + 1
    n_wide = (i * tq) // tw + 1
    iw = iw_ref[0][:, IDX_HEAD_DIM:IDX_HEAD_DIM + idx_heads] * idx_scale
    row_chunk = (i * tq + lax.broadcasted_iota(jnp.int32, (tq, ts), 0)) // CHUNK
    col_iota = lax.broadcasted_iota(jnp.int32, (tq, ts), 1)

    def score_tile(j, _):
        off = pl.multiple_of(j * ts, ts)
        ik_lo = iklo_ref[0, pl.ds(off, ts), :]
        ik_hi = ikhi_ref[0, pl.ds(off, ts), :]
        sc = jnp.zeros((tq, ts), F32)
        for p in range(idx_heads // 2):
            a = iq_ref[0, :, p * LANES:(p + 1) * LANES]
            even = lax.dot_general(a, ik_lo, _NT, preferred_element_type=F32)
            odd = lax.dot_general(a, ik_hi, _NT, preferred_element_type=F32)
            sc = sc + jnp.maximum(even, 0.0) * iw[:, 2 * p:2 * p + 1]
            sc = sc + jnp.maximum(odd, 0.0) * iw[:, 2 * p + 1:2 * p + 2]
        valid = row_chunk >= (off + col_iota) // CHUNK
        keys_ref[:, pl.ds(off, ts)] = jnp.where(valid, _ordered_key(sc), INT32_MIN)
        return 0

    lax.fori_loop(0, n_score, score_tile, 0)

    def fill_tile(j, _):
        keys_ref[:, pl.ds(pl.multiple_of(j * ts, ts), ts)] = jnp.full((tq, ts), INT32_MIN, jnp.int32)
        return 0

    lax.fori_loop(n_score, n_wide * (tw // ts), fill_tile, 0)

    def count_ge(cand):
        def body(j, part):
            off = pl.multiple_of(j * tw, tw)
            ge = (keys_ref[:, pl.ds(off, tw)] >= cand).astype(jnp.int32)
            for c in range(tw // LANES):
                part = part + ge[:, c * LANES:(c + 1) * LANES]
            return part
        part = lax.fori_loop(0, n_wide, body, jnp.zeros((tq, LANES), jnp.int32))
        return jnp.sum(part, axis=1, keepdims=True)

    thr = jnp.where(count_ge(jnp.zeros((tq, 1), jnp.int32)) >= topk, 0, INT32_MIN).astype(jnp.int32)

    def bit_step(t, thr):
        cand = thr + lax.shift_left(jnp.int32(1), 30 - t)
        return jnp.where(count_ge(cand) >= topk, cand, thr)

    thr = lax.fori_loop(0, 31, bit_step, thr)
    thr = jnp.maximum(thr, INT32_MIN + 1)

    ones = jnp.ones((tw, HEAD_DIM), v_ref.dtype)

    def attend(j, carry):
        off = pl.multiple_of(j * tw, tw)
        out = []
        for h in range(heads):
            hs = slice(h * HEAD_DIM, (h + 1) * HEAD_DIM)
            m, l, acc = carry[h]
            s = lax.dot_general(q_ref[0, :, hs], k_ref[0, pl.ds(off, tw), hs], _NT, preferred_element_type=F32)
            s = jnp.where(keys_ref[:, pl.ds(off, tw)] >= thr, s, MASK_VALUE)
            m_new = jnp.maximum(m, jnp.max(s, axis=1, keepdims=True))
            alpha = jnp.exp2(m - m_new)
            p = jnp.exp2(s - m_new).astype(v_ref.dtype)
            v = jnp.concatenate([v_ref[0, pl.ds(off, tw), hs], ones], axis=1)
            pv = jnp.dot(p, v, preferred_element_type=F32)
            out.append((m_new, alpha * l + pv[:, HEAD_DIM:], alpha * acc + pv[:, :HEAD_DIM]))
        return tuple(out)

    init = tuple((jnp.full((tq, 1), MASK_VALUE, F32), jnp.zeros((tq, HEAD_DIM), F32), jnp.zeros((tq, HEAD_DIM), F32))
                 for _ in range(heads))
    final = lax.fori_loop(0, n_wide, attend, init)
    for h in range(heads):
        _, l, acc = final[h]
        o_ref[0, :, h * HEAD_DIM:(h + 1) * HEAD_DIM] = (acc / l).astype(o_ref.dtype)


def _dsa_attention(p3, small3, ik_lo, ik_hi, *, heads, idx_heads, iq_off, q_off, k_off, v_off, topk):
    b, s, _ = p3.shape
    tq = _tile(s, 256)
    ts = _tile(s, 512)
    tw = _tile(s, 1024)
    width = heads * HEAD_DIM
    iq_width = idx_heads * IDX_HEAD_DIM
    resident = dict(pipeline_mode=pl.Buffered(1))
    kern = functools.partial(_dsa_kernel, tq=tq, ts=ts, tw=tw, idx_heads=idx_heads, heads=heads, topk=topk,
                             idx_scale=(IDX_HEAD_DIM ** -0.5) * (idx_heads ** -0.5))
    return pl.pallas_call(
        kern,
        grid=(b, s // tq),
        in_specs=[pl.BlockSpec((1, tq, iq_width), lambda bi, i: (bi, i, iq_off)),
                  pl.BlockSpec((1, tq, LANES), lambda bi, i: (bi, i, 0)),
                  pl.BlockSpec((1, s, LANES), lambda bi, i: (bi, 0, 0), **resident),
                  pl.BlockSpec((1, s, LANES), lambda bi, i: (bi, 0, 0), **resident),
                  pl.BlockSpec((1, tq, width), lambda bi, i: (bi, i, q_off)),
                  pl.BlockSpec((1, s, width), lambda bi, i: (bi, 0, k_off), **resident),
                  pl.BlockSpec((1, s, width), lambda bi, i: (bi, 0, v_off), **resident)],
        out_specs=pl.BlockSpec((1, tq, width), lambda bi, i: (bi, i, 0)),
        out_shape=jax.ShapeDtypeStruct((b, s, width), BF16),
        scratch_shapes=[pltpu.VMEM((tq, s), jnp.int32)],
        compiler_params=_params("parallel", "arbitrary"),
    )(p3, small3, ik_lo, ik_hi, p3, p3, p3)


def _merge_kernel(oa_ref, ob_ref, oc_ref, wa_ref, wb_ref, wc_ref, g0_ref, g1_ref, g2_ref, bg_ref, o_ref):
    def branch(o_r, w_r, g_r, n):
        y = jnp.dot(o_r[...], w_r[...], preferred_element_type=F32)
        return jax.nn.sigmoid(g_r[...].astype(F32) + bg_ref[n:n + 1, :]) * y

    out = branch(oa_ref, wa_ref, g0_ref, 0) + branch(ob_ref, wb_ref, g1_ref, 1) + branch(oc_ref, wc_ref, g2_ref, 2)
    o_ref[...] = out.astype(o_ref.dtype)


def _merge(o_a, o_b, o_c, w_a, w_b, w_c, p, b_gate, *, gate_off, d):
    m = o_a.shape[0]
    bm = _tile(m, 512)
    bn = _tile(d, 1024)
    nj = d // bn

    def lhs(o):
        return pl.BlockSpec((bm, o.shape[1]), lambda i, j: (i, 0))

    def rhs(w):
        return pl.BlockSpec((w.shape[0], bn), lambda i, j: (0, j))

    def gate(n):
        return pl.BlockSpec((bm, bn), lambda i, j: (i, gate_off // bn + n * nj + j))

    assert gate_off % bn == 0
    return pl.pallas_call(
        _merge_kernel,
        grid=(m // bm, nj),
        in_specs=[lhs(o_a), lhs(o_b), lhs(o_c), rhs(w_a), rhs(w_b), rhs(w_c), gate(0), gate(1), gate(2),
                  pl.BlockSpec((N_BRANCH, bn), lambda i, j: (0, j))],
        out_specs=pl.BlockSpec((bm, bn), lambda i, j: (i, j)),
        out_shape=jax.ShapeDtypeStruct((m, d), BF16),
        compiler_params=_params("parallel", "arbitrary"),
    )(o_a, o_b, o_c, w_a, w_b, w_c, p, p, p, b_gate.astype(F32))


def _proj_norm_resid_kernel(a_ref, w_ref, g_ref, r_ref, gn_ref, x_ref, h_ref, *, n_k, bn, k_tail):
    kk = pl.program_id(1)
    n = x_ref.shape[1]

    @pl.when(kk == 0)
    def _():
        x_ref[...] = jnp.zeros_like(x_ref)

    def accumulate(tail):
        a = a_ref[...]
        if tail:
            col = lax.broadcasted_iota(jnp.int32, a.shape, 1)
            a = jnp.where(col < k_tail, a.astype(F32), 0.0).astype(a.dtype)
        for c in range(n // bn):
            sl = slice(c * bn, (c + 1) * bn)
            w = w_ref[:, sl]
            if tail:
                row = lax.broadcasted_iota(jnp.int32, w.shape, 0)
                w = jnp.where(row < k_tail, w.astype(F32), 0.0).astype(w.dtype)
            x_ref[:, sl] += jnp.dot(a, w, preferred_element_type=F32)

    if k_tail == 0:
        accumulate(False)
    else:
        pl.when(kk < n_k - 1)(lambda: accumulate(False))
        pl.when(kk == n_k - 1)(lambda: accumulate(True))

    @pl.when(kk == n_k - 1)
    def _():
        rows = min(x_ref.shape[0], EPILOGUE_ROWS)

        def chunk(c, _):
            rs = pl.ds(pl.multiple_of(c * rows, rows), rows)
            y = x_ref[rs, :]
            ms = jnp.mean(y * y, axis=-1, keepdims=True)
            x_new = r_ref[rs, :] + y * lax.rsqrt(ms + NORM_EPS) * g_ref[...]
            x_ref[rs, :] = x_new
            ms2 = jnp.mean(x_new * x_new, axis=-1, keepdims=True)
            h_ref[rs, :] = (x_new * lax.rsqrt(ms2 + NORM_EPS) * gn_ref[...]).astype(h_ref.dtype)
            return 0

        lax.fori_loop(0, x_ref.shape[0] // rows, chunk, 0)


def _proj_norm_resid(a, w, g_post, resid, g_next):
    m, k = a.shape
    n = w.shape[1]
    bm = _tile(m, 512)
    bk = min(k, 512)
    n_k = pl.cdiv(k, bk)
    vec = pl.BlockSpec((1, n), lambda i, kk: (0, 0))
    row = pl.BlockSpec((bm, n), lambda i, kk: (i, 0))
    return pl.pallas_call(
        functools.partial(_proj_norm_resid_kernel, n_k=n_k, bn=_tile(n, 1024), k_tail=k % bk),
        grid=(m // bm, n_k),
        in_specs=[pl.BlockSpec((bm, bk), lambda i, kk: (i, kk)),
                  pl.BlockSpec((bk, n), lambda i, kk: (kk, 0)),
                  vec, row, vec],
        out_specs=[row, row],
        out_shape=[jax.ShapeDtypeStruct((m, n), F32), jax.ShapeDtypeStruct((m, n), BF16)],
        compiler_params=_params("parallel", "arbitrary"),
    )(a, w, g_post.reshape(1, n).astype(F32), resid, g_next.reshape(1, n).astype(F32))


def _gate_up_conv_kernel(a_ref, wg_ref, wu_ref, cw_ref, cb_ref, o_ref, prev_ref, *, blocks_per_seq):
    i = pl.program_id(0)
    j = pl.program_id(1)
    @pl.when((i % blocks_per_seq) == 0)
    def _():
        prev_ref[j] = jnp.zeros(prev_ref.shape[1:], F32)

    a = a_ref[...]
    bm = a.shape[0]
    z = jnp.dot(a, wg_ref[...], preferred_element_type=F32)
    u = jnp.dot(a, wu_ref[...], preferred_element_type=F32)
    prev2 = prev_ref[j, 0:1, :]
    prev1 = prev_ref[j, 1:2, :]
    prev_ref[j, 0:2, :] = z[bm - 2:bm, :]
    row = lax.broadcasted_iota(jnp.int32, z.shape, 0)
    z1 = jnp.where(row == 0, prev1, pltpu.roll(z, 1, 0))
    z2 = jnp.where(row == 0, prev2, jnp.where(row == 1, prev1, pltpu.roll(z, 2, 0)))
    zc = cw_ref[0:1, :] * z2 + cw_ref[1:2, :] * z1 + cw_ref[2:3, :] * z + cb_ref[...]
    gelu = 0.5 * zc * (1.0 + jnp.tanh(math.sqrt(2.0 / math.pi) * (zc + 0.044715 * (zc * zc * zc))))
    o_ref[...] = (gelu * u).astype(o_ref.dtype)


def _gate_up_conv(h, w_gate, w_up, conv_w, conv_b, seq):
    m, k = h.shape
    n = w_gate.shape[1]
    bm = _tile(seq, 1024)
    bn = min(n, 512)
    n_j = pl.cdiv(n, bn)
    wspec = pl.BlockSpec((k, bn), lambda i, j: (0, j))
    return pl.pallas_call(
        functools.partial(_gate_up_conv_kernel, blocks_per_seq=seq // bm),
        grid=(m // bm, n_j),
        in_specs=[pl.BlockSpec((bm, k), lambda i, j: (i, 0)), wspec, wspec,
                  pl.BlockSpec((CONV_WIDTH, bn), lambda i, j: (0, j)),
                  pl.BlockSpec((1, bn), lambda i, j: (0, j))],
        out_specs=pl.BlockSpec((bm, bn), lambda i, j: (i, j)),
        out_shape=jax.ShapeDtypeStruct((m, n), BF16),
        scratch_shapes=[pltpu.VMEM((n_j, 8, bn), F32)],
        compiler_params=_params("arbitrary", "arbitrary"),
    )(h, w_gate, w_up, conv_w, conv_b)


def _rope_tables(s):
    pos = jnp.arange(s, dtype=F32)

    def cos_sin(d):
        inv_freq = ROPE_THETA ** (-jnp.arange(0, d, 2, dtype=F32) / d)
        ang = pos[:, None] * inv_freq[None, :]
        return jnp.cos(ang), jnp.sin(ang)

    c, sn = cos_sin(HEAD_DIM)
    c128 = jnp.concatenate([c, c], axis=1)
    s128 = jnp.concatenate([-sn, sn], axis=1)
    ci, si = cos_sin(IDX_HEAD_DIM)
    zero = jnp.zeros_like(si)
    c64 = jnp.concatenate([ci, ci, ci, ci], axis=1)
    s_lo = jnp.concatenate([-si, zero, -si, zero], axis=1)
    s_hi = jnp.concatenate([zero, si, zero, si], axis=1)
    return c128, s128, c64, s_lo, s_hi


WEIGHT_BLOCK_BYTES = 4 * 1024 * 1024


def _row_block(rows, row_bytes):
    br = 16
    while br * 2 <= rows and rows % (br * 2) == 0 and br * 2 * row_bytes <= WEIGHT_BLOCK_BYTES:
        br *= 2
    return min(br, rows)


def _cast_kernel(x_ref, o_ref):
    o_ref[...] = x_ref[0].astype(o_ref.dtype)


def _cast_weight(w, layer):
    _, k, n = w.shape
    br = _row_block(k, n * 4)
    return pl.pallas_call(
        _cast_kernel,
        grid=(k // br,),
        in_specs=[pl.BlockSpec((1, br, n), lambda i: (layer, i, 0))],
        out_specs=pl.BlockSpec((br, n), lambda i: (i, 0)),
        out_shape=jax.ShapeDtypeStruct((k, n), BF16),
        compiler_params=_params("parallel"),
    )(w)


F32_SUBLANES = 8


def _transpose_cast_kernel(x_ref, *rest):
    o_ref = rest[-1]
    o_ref[...] = x_ref[0].T.astype(o_ref.dtype)


def _regroup_in_weight(w_in_t, layer, segments, n_big):
    _, _, k = w_in_t.shape
    rb = math.gcd(256, *[width for _, _, width in segments], *[dst for _, dst, _ in segments])
    out = None
    for src, dst, width in segments:
        assert src % F32_SUBLANES == 0, "segment start must sit on an f32 tile row"
        in_specs = [pl.BlockSpec((pl.Element(1), pl.Element(rb), pl.Element(k)),
                                 lambda i, src=src: (layer, pl.multiple_of(src + i * rb, F32_SUBLANES), 0))]
        args = [w_in_t]
        if out is not None:
            in_specs.append(pl.BlockSpec(memory_space=pl.ANY))
            args.append(out)
        out = pl.pallas_call(
            _transpose_cast_kernel,
            grid=(width // rb,),
            in_specs=in_specs,
            out_specs=pl.BlockSpec((k, rb), lambda i, dst=dst: (0, dst // rb + i)),
            out_shape=jax.ShapeDtypeStruct((k, n_big), BF16),
            input_output_aliases={1: 0} if len(args) == 2 else {},
            compiler_params=_params("parallel"),
        )(*args)
    return out


def kernel(x, g_mix_pre, g_mix_post, w_in, b_forget, b_gate, lam_q1, lam_k1, lam_q2, lam_k2, g_subln, w_oa, w_ob,
           w_oc, w_out, g_ffn_pre, g_ffn_post, w_ffn_gate, w_ffn_up, conv_w, conv_b, w_ffn_down):
    b, s, d = x.shape
    depth = w_in.shape[0]
    m = b * s
    fox_w, dsa_w, diffv_w = w_oa.shape[1], w_ob.shape[1], w_oc.shape[1]
    fox_h, dsa_h, diff_h = fox_w // HEAD_DIM, dsa_w // HEAD_DIM, diffv_w // DIFF_V_DIM
    diffqk_w = diff_h * 2 * HEAD_DIM
    known = 3 * fox_w + fox_h + 3 * dsa_w + IDX_HEAD_DIM + 2 * diffqk_w + diffv_w + N_BRANCH * d
    idx_h = (w_in.shape[2] - known) // (IDX_HEAD_DIM + 1)
    iq_w = idx_h * IDX_HEAD_DIM
    d_ff = w_ffn_gate.shape[2]
    topk = min(INDEX_TOPK, s // 4)
    assert known + idx_h * (IDX_HEAD_DIM + 1) == w_in.shape[2]
    assert idx_h % 2 == 0 and IDX_HEAD_DIM + idx_h + fox_h <= LANES and s % CHUNK == 0

    sizes = dict(qa=fox_w, ka=fox_w, va=fox_w, fa=fox_h, qb=dsa_w, kb=dsa_w, vb=dsa_w, iq=iq_w, ik=IDX_HEAD_DIM,
                 iw=idx_h, qc=diffqk_w, kc=diffqk_w, vc=diffv_w, gl=N_BRANCH * d)
    src, pos = {}, 0
    for name, width in sizes.items():
        src[name] = (pos, pos + width)
        pos += width
    big_order = ("qa", "ka", "va", "qb", "kb", "vb", "iq", "qc", "kc", "vc", "gl")
    off, pos = {}, 0
    for name in big_order:
        off[name] = pos
        pos += sizes[name]
    n_big = pos
    bn = math.gcd(1024, *[sizes[name] for name in big_order])
    assert bn % (2 * LANES) == 0
    assert off["iq"] % iq_w == 0 and off["qb"] % dsa_w == 0
    big_segments = []
    for name in big_order:
        if big_segments and (big_segments[-1][0] + big_segments[-1][2] == src[name][0]
                             and big_segments[-1][1] + big_segments[-1][2] == off[name]):
            big_segments[-1] = (big_segments[-1][0], big_segments[-1][1], big_segments[-1][2] + sizes[name])
        else:
            big_segments.append((src[name][0], off[name], sizes[name]))

    def tiles_of(*names):
        return tuple(t for name in names for t in range(off[name] // bn, (off[name] + sizes[name]) // bn))

    tables = _rope_tables(s)

    w_in_t = jnp.swapaxes(w_in, 1, 2)
    x2 = x.reshape(m, d)
    hcur = _rmsnorm(x2, g_mix_pre[0])
    for l in range(depth):
        w_big = _regroup_in_weight(w_in_t, l, big_segments, n_big)
        small_rows = jnp.concatenate([w_in_t[l, src[n][0]:src[n][1]] for n in ("ik", "iw", "fa")], axis=0)
        small_rows = jnp.pad(small_rows, ((0, LANES - small_rows.shape[0]), (0, 0)))
        w_small = pl.pallas_call(
            _transpose_cast_kernel,
            in_specs=[pl.BlockSpec((1, LANES, d), lambda: (0, 0, 0))],
            out_specs=pl.BlockSpec((d, LANES), lambda: (0, 0)),
            out_shape=jax.ShapeDtypeStruct((d, LANES), BF16),
        )(small_rows[None])

        p = _inproj(hcur, w_big, tables, s, bn=bn, q_tiles=tiles_of("qa", "qb", "qc"),
                    rope128_tiles=tiles_of("qb", "kb", "qc", "kc"), rope64_tiles=tiles_of("iq"),
                    q_scale=HEAD_DIM ** -0.5 * LOG2E)
        small = _inproj_small(hcur, w_small, tables[2:], s)
        p3 = p.reshape(b, s, n_big)
        small3 = small.reshape(b, s, LANES)

        fa_t = jnp.swapaxes(small3[:, :, IDX_HEAD_DIM + idx_h:IDX_HEAD_DIM + idx_h + fox_h], 1, 2)
        f_cum = _forget_cumsum(fa_t, b_forget[l])
        o_a = _fox_attention(p3, f_cum, heads=fox_h, q_off=off["qa"] // HEAD_DIM, k_off=off["ka"] // HEAD_DIM,
                             v_off=off["va"] // HEAD_DIM)

        ik = small3[:, :, :IDX_HEAD_DIM].astype(BF16)
        zeros = jnp.zeros_like(ik)
        ik_lo = jnp.concatenate([ik, zeros], axis=2)
        ik_hi = jnp.concatenate([zeros, ik], axis=2)
        o_b = _dsa_attention(p3, small3, ik_lo, ik_hi, heads=dsa_h, idx_heads=idx_h, iq_off=off["iq"] // iq_w,
                             q_off=off["qb"] // dsa_w, k_off=off["kb"] // dsa_w, v_off=off["vb"] // dsa_w, topk=topk)

        lam_init = 0.8 - 0.6 * math.exp(-0.3 * l)
        o_c = _diff_attention(p3, (lam_q1[l], lam_k1[l], lam_q2[l], lam_k2[l]), g_subln[l], heads=diff_h,
                              q_off=off["qc"], k_off=off["kc"], v_off=off["vc"], lam_init=lam_init)

        merged = _merge(o_a.reshape(m, fox_w), o_b.reshape(m, dsa_w), o_c.reshape(m, diffv_w),
                        _cast_weight(w_oa, l), _cast_weight(w_ob, l), _cast_weight(w_oc, l), p, b_gate[l],
                        gate_off=off["gl"], d=d)
        x2, hcur = _proj_norm_resid(merged, _cast_weight(w_out, l), g_mix_post[l], x2, g_ffn_pre[l])

        act = _gate_up_conv(hcur, _cast_weight(w_ffn_gate, l), _cast_weight(w_ffn_up, l),
                            conv_w[l].astype(F32), conv_b[l][None, :].astype(F32), s)
        w_down = _cast_weight(w_ffn_down, l)
        g_next = g_mix_pre[l + 1] if l + 1 < depth else g_mix_pre[0]
        x2, hcur = _proj_norm_resid(act, w_down, g_ffn_post[l], x2, g_next)
    return x2.reshape(b, s, d)
```

```python
import functools
import math

import jax
import jax.numpy as jnp
from jax import lax
from jax.experimental import pallas as pl
from jax.experimental.pallas import tpu as pltpu

CHUNK = 64
ROPE_THETA = 10000.0
NORM_EPS = 1e-6
HEAD_DIM = 128
IDX_HEAD_DIM = 64
DIFF_V_DIM = 256
INDEX_TOPK = 256
N_BRANCH = 3
CONV_WIDTH = 3

LANES = 128
VMEM_LIMIT_BYTES = 56 * 1024 * 1024

NORM_ROWS = 512
MATMUL_ROWS = 1024
INPROJ_COLS = 1024
GATE_UP_COLS = 512
MERGE_ROWS = 512
MERGE_COLS = 1024
PROJ_ROWS = 512
PROJ_K = 512
PROJ_COLS = 1024
EPILOGUE_ROWS = 64
ATTN_TILE = 1024
DSA_Q_TILE = 256
DSA_SCORE_TILE = 512
DSA_WIDE_TILE = 1024
FORGET_TILE = 512
MASK_VALUE = -1e30
INT32_MIN = -(2 ** 31)
LOG2E = math.log2(math.e)

F32 = jnp.float32
BF16 = jnp.bfloat16
_NT = (((1,), (1,)), ((), ()))


def _params(*semantics):
    return pltpu.CompilerParams(dimension_semantics=semantics, vmem_limit_bytes=VMEM_LIMIT_BYTES)


def _tile(dim, want):
    t = min(dim, want)
    assert dim % t == 0, (dim, want)
    return t


def _rmsnorm_kernel(x_ref, g_ref, o_ref):
    x = x_ref[...]
    ms = jnp.mean(x * x, axis=-1, keepdims=True)
    o_ref[...] = (x * lax.rsqrt(ms + NORM_EPS) * g_ref[...]).astype(o_ref.dtype)


def _rmsnorm(x, g):
    m, d = x.shape
    bm = _tile(m, NORM_ROWS)
    return pl.pallas_call(
        _rmsnorm_kernel,
        grid=(m // bm,),
        in_specs=[pl.BlockSpec((bm, d), lambda i: (i, 0)), pl.BlockSpec((1, d), lambda i: (0, 0))],
        out_specs=pl.BlockSpec((bm, d), lambda i: (i, 0)),
        out_shape=jax.ShapeDtypeStruct((m, d), BF16),
        compiler_params=_params("parallel"),
    )(x, g.reshape(1, d))


def _rope128(x, cos, sin):
    return x * cos + pltpu.roll(x, HEAD_DIM // 2, 1) * sin


def _rope64(x, cos, sin_lo, sin_hi):
    return x * cos + pltpu.roll(x, LANES - 32, 1) * sin_lo + pltpu.roll(x, 32, 1) * sin_hi


def _inproj_kernel(a_ref, w_ref, c128_ref, s128_ref, c64_ref, slo_ref, shi_ref, o_ref, *,
                   q_tiles, rope128_tiles, rope64_tiles, q_scale):
    j = pl.program_id(1)

    def member(tiles):
        hit = j < 0
        for t in tiles:
            hit = hit | (j == t)
        return hit

    acc = jnp.dot(a_ref[...], w_ref[...], preferred_element_type=F32)
    acc = acc * jnp.where(member(q_tiles), q_scale, 1.0).astype(F32)
    is128 = member(rope128_tiles)
    is64 = member(rope64_tiles)
    n_sub = acc.shape[1] // LANES
    o_ref[...] = acc.astype(o_ref.dtype)

    @pl.when(is128)
    def _():
        for c in range(n_sub):
            sl = slice(c * LANES, (c + 1) * LANES)
            o_ref[:, sl] = _rope128(acc[:, sl], c128_ref[...], s128_ref[...]).astype(o_ref.dtype)

    @pl.when(is64)
    def _():
        for c in range(n_sub):
            sl = slice(c * LANES, (c + 1) * LANES)
            o_ref[:, sl] = _rope64(acc[:, sl], c64_ref[...], slo_ref[...], shi_ref[...]).astype(o_ref.dtype)


def _inproj(h, w, tables, seq, *, bn, q_tiles, rope128_tiles, rope64_tiles, q_scale):
    m, k = h.shape
    n = w.shape[1]
    bm = _tile(seq, MATMUL_ROWS)
    pos_blocks = seq // bm
    tab_spec = pl.BlockSpec((bm, LANES), lambda i, j: (i % pos_blocks, 0))
    kern = functools.partial(_inproj_kernel, q_tiles=q_tiles, rope128_tiles=rope128_tiles,
                             rope64_tiles=rope64_tiles, q_scale=q_scale)
    return pl.pallas_call(
        kern,
        grid=(m // bm, n // bn),
        in_specs=[pl.BlockSpec((bm, k), lambda i, j: (i, 0)),
                  pl.BlockSpec((k, bn), lambda i, j: (0, j))] + [tab_spec] * 5,
        out_specs=pl.BlockSpec((bm, bn), lambda i, j: (i, j)),
        out_shape=jax.ShapeDtypeStruct((m, n), BF16),
        compiler_params=_params("parallel", "arbitrary"),
    )(h, w, *tables)


def _inproj_small_kernel(a_ref, w_ref, c64_ref, slo_ref, shi_ref, o_ref):
    acc = jnp.dot(a_ref[...], w_ref[...], preferred_element_type=F32)
    roped = _rope64(acc, c64_ref[...], slo_ref[...], shi_ref[...])
    lane = lax.broadcasted_iota(jnp.int32, acc.shape, 1)
    o_ref[...] = jnp.where(lane < IDX_HEAD_DIM, roped, acc)


def _inproj_small(h, w, tables64, seq):
    m, k = h.shape
    bm = _tile(seq, MATMUL_ROWS)
    pos_blocks = seq // bm
    tab_spec = pl.BlockSpec((bm, LANES), lambda i: (i % pos_blocks, 0))
    return pl.pallas_call(
        _inproj_small_kernel,
        grid=(m // bm,),
        in_specs=[pl.BlockSpec((bm, k), lambda i: (i, 0)),
                  pl.BlockSpec((k, LANES), lambda i: (0, 0))] + [tab_spec] * 3,
        out_specs=pl.BlockSpec((bm, LANES), lambda i: (i, 0)),
        out_shape=jax.ShapeDtypeStruct((m, LANES), F32),
        compiler_params=_params("parallel"),
    )(h, w, *tables64)


def _split3(x):
    hi = x.astype(BF16)
    rem = x - hi.astype(F32)
    mid = rem.astype(BF16)
    lo = (rem - mid.astype(F32)).astype(BF16)
    return hi, mid, lo


def _forget_bias_kernel(x_ref, b_ref, qx_ref, kx_ref, carry_ref, *, heads, first_lane):
    @pl.when(pl.program_id(1) == 0)
    def _():
        carry_ref[...] = jnp.zeros_like(carry_ref)

    x = x_ref[0] + b_ref[...]
    lf = jnp.minimum(x, 0.0) - jnp.log1p(jnp.exp(-jnp.abs(x)))
    t = lf.shape[0]
    tri = (lax.broadcasted_iota(jnp.int32, (t, t), 0) >= lax.broadcasted_iota(jnp.int32, (t, t), 1)).astype(BF16)
    loc = sum(jnp.dot(tri, piece, preferred_element_type=F32) for piece in _split3(lf))
    pieces = _split3((loc + carry_ref[...]) * LOG2E)
    carry_ref[...] = carry_ref[...] + loc[t - 1:t, :]

    row = lax.broadcasted_iota(jnp.int32, (LANES, LANES), 0)
    col = lax.broadcasted_iota(jnp.int32, (LANES, LANES), 1)
    lane = lax.broadcasted_iota(jnp.int32, (t, LANES), 1)
    n = len(pieces)
    for h in range(heads):
        def placed(first_col):
            return sum(jnp.dot(piece, ((row == first_lane + h) & (col == first_col + k)).astype(BF16),
                               preferred_element_type=F32) for k, piece in enumerate(pieces))
        qx_ref[0, h] = jnp.where((lane >= n) & (lane < 2 * n), 1.0, placed(0)).astype(qx_ref.dtype)
        kx_ref[0, h] = jnp.where(lane < n, 1.0, -placed(n)).astype(kx_ref.dtype)


def _forget_bias_columns(small3, b_forget, *, heads, first_lane):
    b, s, _ = small3.shape
    t = _tile(s, FORGET_TILE)
    bias = jnp.zeros((1, LANES), F32).at[0, first_lane:first_lane + heads].set(b_forget.astype(F32))
    out_spec = pl.BlockSpec((1, heads, t, LANES), lambda bi, i: (bi, 0, i, 0))
    return pl.pallas_call(
        functools.partial(_forget_bias_kernel, heads=heads, first_lane=first_lane),
        grid=(b, s // t),
        in_specs=[pl.BlockSpec((1, t, LANES), lambda bi, i: (bi, i, 0)), pl.BlockSpec((1, LANES), lambda bi, i: (0, 0))],
        out_specs=[out_spec] * 2,
        out_shape=[jax.ShapeDtypeStruct((b, heads, s, LANES), BF16)] * 2,
        scratch_shapes=[pltpu.VMEM((1, LANES), F32)],
        compiler_params=_params("parallel", "arbitrary"),
    )(small3, bias)


def _dot_row_halves(p, v):
    h = p.shape[0] // 2
    return jnp.concatenate([jnp.dot(p[:h], v, preferred_element_type=F32),
                            jnp.dot(p[h:], v, preferred_element_type=F32)], axis=0)


def _online_update(s, v, m, l, acc):
    m_new = jnp.maximum(m, jnp.max(s, axis=1, keepdims=True))
    alpha = jnp.exp2(m - m_new)
    p = jnp.exp2(s - m_new)
    l_new = alpha * l + jnp.sum(p, axis=1, keepdims=True)
    acc_new = alpha * acc + _dot_row_halves(p.astype(v.dtype), v)
    return m_new, l_new, acc_new


def _softmax_init(tq, dv):
    return (jnp.full((tq, 1), MASK_VALUE, F32), jnp.zeros((tq, 1), F32), jnp.zeros((tq, dv), F32))


def _fox_kernel(q_ref, qx_ref, k_ref, kx_ref, v_ref, o_ref, *, tq, tk, group):
    i = pl.program_id(2)
    n_full = (i * tq) // tk
    ones = jnp.ones((tk, HEAD_DIM), v_ref.dtype)

    def tile(j, carry, diagonal):
        off = pl.multiple_of(j * tk, tk)
        out = []
        for g in range(group):
            hs = slice(g * HEAD_DIM, (g + 1) * HEAD_DIM)
            m, l, acc = carry[g]
            q = jnp.concatenate([q_ref[0, :, hs], qx_ref[0, g]], axis=1)
            k = jnp.concatenate([k_ref[0, pl.ds(off, tk), hs], kx_ref[0, g, pl.ds(off, tk), :]], axis=1)
            s = lax.dot_general(q, k, _NT, preferred_element_type=F32)
            if diagonal:
                row = i * tq + lax.broadcasted_iota(jnp.int32, s.shape, 0)
                col = off + lax.broadcasted_iota(jnp.int32, s.shape, 1)
                s = jnp.where(row >= col, s, MASK_VALUE)
            m_new = jnp.maximum(m, jnp.max(s, axis=1, keepdims=True))
            alpha = jnp.exp2(m - m_new)
            p = jnp.exp2(s - m_new).astype(v_ref.dtype)
            v = jnp.concatenate([v_ref[0, pl.ds(off, tk), hs], ones], axis=1)
            pv = _dot_row_halves(p, v)
            out.append((m_new, alpha * l + pv[:, HEAD_DIM:], alpha * acc + pv[:, :HEAD_DIM]))
        return tuple(out)

    init = tuple((jnp.full((tq, 1), MASK_VALUE, F32), jnp.zeros((tq, HEAD_DIM), F32), jnp.zeros((tq, HEAD_DIM), F32))
                 for _ in range(group))
    carry = lax.fori_loop(0, n_full, lambda j, c: tile(j, c, False), init)
    final = tile(n_full, carry, True)
    for g in range(group):
        _, l, acc = final[g]
        o_ref[0, :, g * HEAD_DIM:(g + 1) * HEAD_DIM] = (acc / l).astype(o_ref.dtype)


def _fox_attention(p3, qx, kx, *, heads, q_off, k_off, v_off):
    b, s, _ = p3.shape
    tq = _tile(s, ATTN_TILE)
    tk = _tile(s, ATTN_TILE)
    group = 1
    gw = group * HEAD_DIM
    assert q_off % group == 0 and k_off % group == 0 and v_off % group == 0
    return pl.pallas_call(
        functools.partial(_fox_kernel, tq=tq, tk=tk, group=group),
        grid=(b, heads // group, s // tq),
        in_specs=[pl.BlockSpec((1, tq, gw), lambda bi, h, i: (bi, i, q_off // group + h)),
                  pl.BlockSpec((1, group, tq, HEAD_DIM), lambda bi, h, i: (bi, h, i, 0)),
                  pl.BlockSpec((1, s, gw), lambda bi, h, i: (bi, 0, k_off // group + h)),
                  pl.BlockSpec((1, group, s, HEAD_DIM), lambda bi, h, i: (bi, h, 0, 0)),
                  pl.BlockSpec((1, s, gw), lambda bi, h, i: (bi, 0, v_off // group + h))],
        out_specs=pl.BlockSpec((1, tq, gw), lambda bi, h, i: (bi, i, h)),
        out_shape=jax.ShapeDtypeStruct((b, s, heads * HEAD_DIM), BF16),
        compiler_params=_params("parallel", "parallel", "arbitrary"),
    )(p3, qx, p3, kx, p3)


def _diff_kernel(q_ref, k_ref, v_ref, lq1_ref, lk1_ref, lq2_ref, lk2_ref, g_ref, o_ref, *, tq, tk, group, lam_init):
    i = pl.program_id(2)
    n_full = (i * tq) // tk

    def tile(j, carry, diagonal):
        off = pl.multiple_of(j * tk, tk)
        if diagonal:
            row = (i * tq + lax.broadcasted_iota(jnp.int32, (tq, tk), 0)) // CHUNK
            col = (off + lax.broadcasted_iota(jnp.int32, (tq, tk), 1)) // CHUNK
            keep = row >= col
        out = []
        for g in range(group):
            v = v_ref[0, pl.ds(off, tk), g * DIFF_V_DIM:(g + 1) * DIFF_V_DIM]
            for half in range(2):
                hs = slice((2 * g + half) * HEAD_DIM, (2 * g + half + 1) * HEAD_DIM)
                s = lax.dot_general(q_ref[0, :, hs], k_ref[0, pl.ds(off, tk), hs], _NT, preferred_element_type=F32)
                if diagonal:
                    s = jnp.where(keep, s, MASK_VALUE)
                out.append(_online_update(s, v, *carry[2 * g + half]))
        return tuple(out)

    init = tuple(_softmax_init(tq, DIFF_V_DIM) for _ in range(2 * group))
    carry = lax.fori_loop(0, n_full, lambda j, c: tile(j, c, False), init)
    final = tile(n_full, carry, True)

    lam = (jnp.exp(jnp.sum(lq1_ref[...] * lk1_ref[...], axis=1, keepdims=True))
           - jnp.exp(jnp.sum(lq2_ref[...] * lk2_ref[...], axis=1, keepdims=True)) + lam_init)
    for g in range(group):
        (_, l1, a1), (_, l2, a2) = final[2 * g], final[2 * g + 1]
        o = a1 / l1 - lam * (a2 / l2)
        ms = jnp.mean(o * o, axis=-1, keepdims=True)
        o = o * lax.rsqrt(ms + NORM_EPS) * g_ref[...]
        o_ref[0, :, g * DIFF_V_DIM:(g + 1) * DIFF_V_DIM] = (o * (1.0 - lam_init)).astype(o_ref.dtype)


def _diff_attention(p3, lam_vecs, g_subln, *, heads, q_off, k_off, v_off, lam_init):
    b, s, _ = p3.shape
    tq = _tile(s, ATTN_TILE)
    tk = _tile(s, ATTN_TILE)
    group = 1
    gw = group * DIFF_V_DIM
    assert q_off % gw == 0 and k_off % gw == 0 and v_off % gw == 0
    vec_spec = pl.BlockSpec((1, HEAD_DIM), lambda bi, h, i: (0, 0))
    return pl.pallas_call(
        functools.partial(_diff_kernel, tq=tq, tk=tk, group=group, lam_init=lam_init),
        grid=(b, heads // group, s // tq),
        in_specs=[pl.BlockSpec((1, tq, gw), lambda bi, h, i: (bi, i, q_off // gw + h)),
                  pl.BlockSpec((1, s, gw), lambda bi, h, i: (bi, 0, k_off // gw + h)),
                  pl.BlockSpec((1, s, gw), lambda bi, h, i: (bi, 0, v_off // gw + h)),
                  vec_spec, vec_spec, vec_spec, vec_spec,
                  pl.BlockSpec((1, DIFF_V_DIM), lambda bi, h, i: (0, 0))],
        out_specs=pl.BlockSpec((1, tq, gw), lambda bi, h, i: (bi, i, h)),
        out_shape=jax.ShapeDtypeStruct((b, s, heads * DIFF_V_DIM), BF16),
        compiler_params=_params("parallel", "parallel", "arbitrary"),
    )(p3, p3, p3, *[v.reshape(1, HEAD_DIM).astype(F32) for v in lam_vecs],
      g_subln.reshape(1, DIFF_V_DIM).astype(F32))


def _ordered_key(x):
    bits = pltpu.bitcast(x, jnp.int32)
    return bits ^ (lax.shift_right_arithmetic(bits, 31) & 0x7FFFFFFF)


def _dsa_kernel(iq_ref, iw_ref, iklo_ref, ikhi_ref, q_ref, k_ref, v_ref, o_ref, keys_ref, *,
                tq, ts, tw, idx_heads, heads, topk, idx_scale):
    i = pl.program_id(1)
    n_score = (i * tq) // ts + 1
    n_wide = (i * tq) // tw + 1
    iw = iw_ref[0][:, IDX_HEAD_DIM:IDX_HEAD_DIM + idx_heads] * idx_scale
    row_chunk = (i * tq + lax.broadcasted_iota(jnp.int32, (tq, ts), 0)) // CHUNK
    col_iota = lax.broadcasted_iota(jnp.int32, (tq, ts), 1)

    def score_tile(j, _):
        off = pl.multiple_of(j * ts, ts)
        ik_lo = iklo_ref[0, pl.ds(off, ts), :]
        ik_hi = ikhi_ref[0, pl.ds(off, ts), :]
        sc = jnp.zeros((tq, ts), F32)
        for p in range(idx_heads // 2):
            a = iq_ref[0, :, p * LANES:(p + 1) * LANES]
            even = lax.dot_general(a, ik_lo, _NT, preferred_element_type=F32)
            odd = lax.dot_general(a, ik_hi, _NT, preferred_element_type=F32)
            sc = sc + jnp.maximum(even, 0.0) * iw[:, 2 * p:2 * p + 1]
            sc = sc + jnp.maximum(odd, 0.0) * iw[:, 2 * p + 1:2 * p + 2]
        valid = row_chunk >= (off + col_iota) // CHUNK
        keys_ref[:, pl.ds(off, ts)] = jnp.where(valid, _ordered_key(sc), INT32_MIN)
        return 0

    lax.fori_loop(0, n_score, score_tile, 0)

    def fill_tile(j, _):
        keys_ref[:, pl.ds(pl.multiple_of(j * ts, ts), ts)] = jnp.full((tq, ts), INT32_MIN, jnp.int32)
        return 0

    lax.fori_loop(n_score, n_wide * (tw // ts), fill_tile, 0)

    def count_ge(cand):
        def body(j, part):
            off = pl.multiple_of(j * tw, tw)
            ge = (keys_ref[:, pl.ds(off, tw)] >= cand).astype(jnp.int32)
            for c in range(tw // LANES):
                part = part + ge[:, c * LANES:(c + 1) * LANES]
            return part
        part = lax.fori_loop(0, n_wide, body, jnp.zeros((tq, LANES), jnp.int32))
        return jnp.sum(part, axis=1, keepdims=True)

    thr = jnp.where(count_ge(jnp.zeros((tq, 1), jnp.int32)) >= topk, 0, INT32_MIN).astype(jnp.int32)

    def bit_step(t, thr):
        cand = thr + lax.shift_left(jnp.int32(1), 30 - t)
        return jnp.where(count_ge(cand) >= topk, cand, thr)

    thr = lax.fori_loop(0, 31, bit_step, thr)
    thr = jnp.maximum(thr, INT32_MIN + 1)

    ones = jnp.ones((tw, HEAD_DIM), v_ref.dtype)

    def attend(j, carry):
        off = pl.multiple_of(j * tw, tw)
        out = []
        for h in range(heads):
            hs = slice(h * HEAD_DIM, (h + 1) * HEAD_DIM)
            m, l, acc = carry[h]
            s = lax.dot_general(q_ref[0, :, hs], k_ref[0, pl.ds(off, tw), hs], _NT, preferred_element_type=F32)
            s = jnp.where(keys_ref[:, pl.ds(off, tw)] >= thr, s, MASK_VALUE)
            m_new = jnp.maximum(m, jnp.max(s, axis=1, keepdims=True))
            alpha = jnp.exp2(m - m_new)
            p = jnp.exp2(s - m_new).astype(v_ref.dtype)
            v = jnp.concatenate([v_ref[0, pl.ds(off, tw), hs], ones], axis=1)
            pv = jnp.dot(p, v, preferred_element_type=F32)
            out.append((m_new, alpha * l + pv[:, HEAD_DIM:], alpha * acc + pv[:, :HEAD_DIM]))
        return tuple(out)

    init = tuple((jnp.full((tq, 1), MASK_VALUE, F32), jnp.zeros((tq, HEAD_DIM), F32), jnp.zeros((tq, HEAD_DIM), F32))
                 for _ in range(heads))
    final = lax.fori_loop(0, n_wide, attend, init)
    for h in range(heads):
        _, l, acc = final[h]
        o_ref[0, :, h * HEAD_DIM:(h + 1) * HEAD_DIM] = (acc / l).astype(o_ref.dtype)


def _dsa_attention(p3, small3, ik_lo, ik_hi, *, heads, idx_heads, iq_off, q_off, k_off, v_off, topk):
    b, s, _ = p3.shape
    tq = _tile(s, DSA_Q_TILE)
    ts = _tile(s, DSA_SCORE_TILE)
    tw = _tile(s, DSA_WIDE_TILE)
    width = heads * HEAD_DIM
    iq_width = idx_heads * IDX_HEAD_DIM
    resident = dict(pipeline_mode=pl.Buffered(1))
    kern = functools.partial(_dsa_kernel, tq=tq, ts=ts, tw=tw, idx_heads=idx_heads, heads=heads, topk=topk,
                             idx_scale=(IDX_HEAD_DIM ** -0.5) * (idx_heads ** -0.5))
    return pl.pallas_call(
        kern,
        grid=(b, s // tq),
        in_specs=[pl.BlockSpec((1, tq, iq_width), lambda bi, i: (bi, i, iq_off)),
                  pl.BlockSpec((1, tq, LANES), lambda bi, i: (bi, i, 0)),
                  pl.BlockSpec((1, s, LANES), lambda bi, i: (bi, 0, 0), **resident),
                  pl.BlockSpec((1, s, LANES), lambda bi, i: (bi, 0, 0), **resident),
                  pl.BlockSpec((1, tq, width), lambda bi, i: (bi, i, q_off)),
                  pl.BlockSpec((1, s, width), lambda bi, i: (bi, 0, k_off), **resident),
                  pl.BlockSpec((1, s, width), lambda bi, i: (bi, 0, v_off), **resident)],
        out_specs=pl.BlockSpec((1, tq, width), lambda bi, i: (bi, i, 0)),
        out_shape=jax.ShapeDtypeStruct((b, s, width), BF16),
        scratch_shapes=[pltpu.VMEM((tq, s), jnp.int32)],
        compiler_params=_params("parallel", "arbitrary"),
    )(p3, small3, ik_lo, ik_hi, p3, p3, p3)


def _merge_kernel(oa_ref, ob_ref, oc_ref, wa_ref, wb_ref, wc_ref, g0_ref, g1_ref, g2_ref, bg_ref, o_ref):
    def branch(o_r, w_r, g_r, n):
        y = jnp.dot(o_r[...], w_r[...], preferred_element_type=F32)
        return jax.nn.sigmoid(g_r[...].astype(F32) + bg_ref[n:n + 1, :]) * y

    out = branch(oa_ref, wa_ref, g0_ref, 0) + branch(ob_ref, wb_ref, g1_ref, 1) + branch(oc_ref, wc_ref, g2_ref, 2)
    o_ref[...] = out.astype(o_ref.dtype)


def _merge(o_a, o_b, o_c, w_a, w_b, w_c, p, b_gate, *, gate_off, d):
    m = o_a.shape[0]
    bm = _tile(m, MERGE_ROWS)
    bn = _tile(d, MERGE_COLS)
    nj = d // bn

    def lhs(o):
        return pl.BlockSpec((bm, o.shape[1]), lambda i, j: (i, 0))

    def rhs(w):
        return pl.BlockSpec((w.shape[0], bn), lambda i, j: (0, j))

    def gate(n):
        return pl.BlockSpec((bm, bn), lambda i, j: (i, gate_off // bn + n * nj + j))

    assert gate_off % bn == 0
    return pl.pallas_call(
        _merge_kernel,
        grid=(m // bm, nj),
        in_specs=[lhs(o_a), lhs(o_b), lhs(o_c), rhs(w_a), rhs(w_b), rhs(w_c), gate(0), gate(1), gate(2),
                  pl.BlockSpec((N_BRANCH, bn), lambda i, j: (0, j))],
        out_specs=pl.BlockSpec((bm, bn), lambda i, j: (i, j)),
        out_shape=jax.ShapeDtypeStruct((m, d), BF16),
        compiler_params=_params("parallel", "arbitrary"),
    )(o_a, o_b, o_c, w_a, w_b, w_c, p, p, p, b_gate.astype(F32))


def _proj_norm_resid_kernel(a_ref, w_ref, g_ref, r_ref, gn_ref, x_ref, h_ref, *, n_k, bn, k_tail):
    kk = pl.program_id(1)
    n = x_ref.shape[1]

    @pl.when(kk == 0)
    def _():
        x_ref[...] = jnp.zeros_like(x_ref)

    def accumulate(tail):
        a = a_ref[...]
        if tail:
            col = lax.broadcasted_iota(jnp.int32, a.shape, 1)
            a = jnp.where(col < k_tail, a.astype(F32), 0.0).astype(a.dtype)
        for c in range(n // bn):
            sl = slice(c * bn, (c + 1) * bn)
            w = w_ref[:, sl]
            if tail:
                row = lax.broadcasted_iota(jnp.int32, w.shape, 0)
                w = jnp.where(row < k_tail, w.astype(F32), 0.0).astype(w.dtype)
            x_ref[:, sl] += jnp.dot(a, w, preferred_element_type=F32)

    if k_tail == 0:
        accumulate(False)
    else:
        pl.when(kk < n_k - 1)(lambda: accumulate(False))
        pl.when(kk == n_k - 1)(lambda: accumulate(True))

    @pl.when(kk == n_k - 1)
    def _():
        rows = min(x_ref.shape[0], EPILOGUE_ROWS)

        def chunk(c, _):
            rs = pl.ds(pl.multiple_of(c * rows, rows), rows)
            y = x_ref[rs, :]
            ms = jnp.mean(y * y, axis=-1, keepdims=True)
            x_new = r_ref[rs, :] + y * lax.rsqrt(ms + NORM_EPS) * g_ref[...]
            x_ref[rs, :] = x_new
            ms2 = jnp.mean(x_new * x_new, axis=-1, keepdims=True)
            h_ref[rs, :] = (x_new * lax.rsqrt(ms2 + NORM_EPS) * gn_ref[...]).astype(h_ref.dtype)
            return 0

        lax.fori_loop(0, x_ref.shape[0] // rows, chunk, 0)


def _proj_norm_resid(a, w, g_post, resid, g_next):
    m, k = a.shape
    n = w.shape[1]
    bm = _tile(m, PROJ_ROWS)
    bk = min(k, PROJ_K)
    n_k = pl.cdiv(k, bk)
    vec = pl.BlockSpec((1, n), lambda i, kk: (0, 0))
    row = pl.BlockSpec((bm, n), lambda i, kk: (i, 0))
    return pl.pallas_call(
        functools.partial(_proj_norm_resid_kernel, n_k=n_k, bn=_tile(n, PROJ_COLS), k_tail=k % bk),
        grid=(m // bm, n_k),
        in_specs=[pl.BlockSpec((bm, bk), lambda i, kk: (i, kk)),
                  pl.BlockSpec((bk, n), lambda i, kk: (kk, 0)),
                  vec, row, vec],
        out_specs=[row, row],
        out_shape=[jax.ShapeDtypeStruct((m, n), F32), jax.ShapeDtypeStruct((m, n), BF16)],
        compiler_params=_params("parallel", "arbitrary"),
    )(a, w, g_post.reshape(1, n).astype(F32), resid, g_next.reshape(1, n).astype(F32))


def _gate_up_conv_kernel(a_ref, wg_ref, wu_ref, cw_ref, cb_ref, o_ref, prev_ref, *, blocks_per_seq):
    i = pl.program_id(0)
    j = pl.program_id(1)
    @pl.when((i % blocks_per_seq) == 0)
    def _():
        prev_ref[j] = jnp.zeros(prev_ref.shape[1:], F32)

    a = a_ref[...]
    bm = a.shape[0]
    z = jnp.dot(a, wg_ref[...], preferred_element_type=F32)
    u = jnp.dot(a, wu_ref[...], preferred_element_type=F32)
    prev2 = prev_ref[j, 0:1, :]
    prev1 = prev_ref[j, 1:2, :]
    prev_ref[j, 0:2, :] = z[bm - 2:bm, :]
    row = lax.broadcasted_iota(jnp.int32, z.shape, 0)
    z1 = jnp.where(row == 0, prev1, pltpu.roll(z, 1, 0))
    z2 = jnp.where(row == 0, prev2, jnp.where(row == 1, prev1, pltpu.roll(z, 2, 0)))
    zc = cw_ref[0:1, :] * z2 + cw_ref[1:2, :] * z1 + cw_ref[2:3, :] * z + cb_ref[...]
    gelu = 0.5 * zc * (1.0 + jnp.tanh(math.sqrt(2.0 / math.pi) * (zc + 0.044715 * (zc * zc * zc))))
    o_ref[...] = (gelu * u).astype(o_ref.dtype)


def _gate_up_conv(h, w_gate, w_up, conv_w, conv_b, seq):
    m, k = h.shape
    n = w_gate.shape[1]
    bm = _tile(seq, MATMUL_ROWS)
    bn = min(n, GATE_UP_COLS)
    n_j = pl.cdiv(n, bn)
    wspec = pl.BlockSpec((k, bn), lambda i, j: (0, j))
    return pl.pallas_call(
        functools.partial(_gate_up_conv_kernel, blocks_per_seq=seq // bm),
        grid=(m // bm, n_j),
        in_specs=[pl.BlockSpec((bm, k), lambda i, j: (i, 0)), wspec, wspec,
                  pl.BlockSpec((CONV_WIDTH, bn), lambda i, j: (0, j)),
                  pl.BlockSpec((1, bn), lambda i, j: (0, j))],
        out_specs=pl.BlockSpec((bm, bn), lambda i, j: (i, j)),
        out_shape=jax.ShapeDtypeStruct((m, n), BF16),
        scratch_shapes=[pltpu.VMEM((n_j, 8, bn), F32)],
        compiler_params=_params("arbitrary", "arbitrary"),
    )(h, w_gate, w_up, conv_w, conv_b)


def _rope_tables(s):
    pos = jnp.arange(s, dtype=F32)

    def cos_sin(d):
        inv_freq = ROPE_THETA ** (-jnp.arange(0, d, 2, dtype=F32) / d)
        ang = pos[:, None] * inv_freq[None, :]
        return jnp.cos(ang), jnp.sin(ang)

    c, sn = cos_sin(HEAD_DIM)
    c128 = jnp.concatenate([c, c], axis=1)
    s128 = jnp.concatenate([-sn, sn], axis=1)
    ci, si = cos_sin(IDX_HEAD_DIM)
    zero = jnp.zeros_like(si)
    c64 = jnp.concatenate([ci, ci, ci, ci], axis=1)
    s_lo = jnp.concatenate([-si, zero, -si, zero], axis=1)
    s_hi = jnp.concatenate([zero, si, zero, si], axis=1)
    return c128, s128, c64, s_lo, s_hi


WEIGHT_BLOCK_BYTES = 4 * 1024 * 1024


def _row_block(rows, row_bytes):
    br = 16
    while br * 2 <= rows and rows % (br * 2) == 0 and br * 2 * row_bytes <= WEIGHT_BLOCK_BYTES:
        br *= 2
    return min(br, rows)


def _cast_kernel(x_ref, o_ref):
    o_ref[...] = x_ref[0].astype(o_ref.dtype)


def _cast_weight(w, layer):
    _, k, n = w.shape
    br = _row_block(k, n * 4)
    return pl.pallas_call(
        _cast_kernel,
        grid=(k // br,),
        in_specs=[pl.BlockSpec((1, br, n), lambda i: (layer, i, 0))],
        out_specs=pl.BlockSpec((br, n), lambda i: (i, 0)),
        out_shape=jax.ShapeDtypeStruct((k, n), BF16),
        compiler_params=_params("parallel"),
    )(w)


F32_SUBLANES = 8


def _transpose_cast_kernel(x_ref, *rest):
    o_ref = rest[-1]
    o_ref[...] = x_ref[0].T.astype(o_ref.dtype)


def _regroup_in_weight(w_in_t, layer, segments, n_big):
    _, _, k = w_in_t.shape
    rb = math.gcd(256, *[width for _, _, width in segments], *[dst for _, dst, _ in segments])
    out = None
    for src, dst, width in segments:
        assert src % F32_SUBLANES == 0, "segment start must sit on an f32 tile row"
        in_specs = [pl.BlockSpec((pl.Element(1), pl.Element(rb), pl.Element(k)),
                                 lambda i, src=src: (layer, pl.multiple_of(src + i * rb, F32_SUBLANES), 0))]
        args = [w_in_t]
        if out is not None:
            in_specs.append(pl.BlockSpec(memory_space=pl.ANY))
            args.append(out)
        out = pl.pallas_call(
            _transpose_cast_kernel,
            grid=(width // rb,),
            in_specs=in_specs,
            out_specs=pl.BlockSpec((k, rb), lambda i, dst=dst: (0, dst // rb + i)),
            out_shape=jax.ShapeDtypeStruct((k, n_big), BF16),
            input_output_aliases={1: 0} if len(args) == 2 else {},
            compiler_params=_params("parallel"),
        )(*args)
    return out


def kernel(x, g_mix_pre, g_mix_post, w_in, b_forget, b_gate, lam_q1, lam_k1, lam_q2, lam_k2, g_subln, w_oa, w_ob,
           w_oc, w_out, g_ffn_pre, g_ffn_post, w_ffn_gate, w_ffn_up, conv_w, conv_b, w_ffn_down):
    b, s, d = x.shape
    depth = w_in.shape[0]
    m = b * s
    fox_w, dsa_w, diffv_w = w_oa.shape[1], w_ob.shape[1], w_oc.shape[1]
    fox_h, dsa_h, diff_h = fox_w // HEAD_DIM, dsa_w // HEAD_DIM, diffv_w // DIFF_V_DIM
    diffqk_w = diff_h * 2 * HEAD_DIM
    known = 3 * fox_w + fox_h + 3 * dsa_w + IDX_HEAD_DIM + 2 * diffqk_w + diffv_w + N_BRANCH * d
    idx_h = (w_in.shape[2] - known) // (IDX_HEAD_DIM + 1)
    iq_w = idx_h * IDX_HEAD_DIM
    topk = min(INDEX_TOPK, s // 4)
    assert known + idx_h * (IDX_HEAD_DIM + 1) == w_in.shape[2]
    assert idx_h % 2 == 0 and IDX_HEAD_DIM + idx_h + fox_h <= LANES and s % CHUNK == 0

    sizes = dict(qa=fox_w, ka=fox_w, va=fox_w, fa=fox_h, qb=dsa_w, kb=dsa_w, vb=dsa_w, iq=iq_w, ik=IDX_HEAD_DIM,
                 iw=idx_h, qc=diffqk_w, kc=diffqk_w, vc=diffv_w, gl=N_BRANCH * d)
    src, pos = {}, 0
    for name, width in sizes.items():
        src[name] = (pos, pos + width)
        pos += width
    big_order = ("qa", "ka", "va", "qb", "kb", "vb", "iq", "qc", "kc", "vc", "gl")
    off, pos = {}, 0
    for name in big_order:
        off[name] = pos
        pos += sizes[name]
    n_big = pos
    bn = math.gcd(INPROJ_COLS, *[sizes[name] for name in big_order])
    assert bn % (2 * LANES) == 0
    assert off["iq"] % iq_w == 0 and off["qb"] % dsa_w == 0
    big_segments = []
    for name in big_order:
        if big_segments and (big_segments[-1][0] + big_segments[-1][2] == src[name][0]
                             and big_segments[-1][1] + big_segments[-1][2] == off[name]):
            big_segments[-1] = (big_segments[-1][0], big_segments[-1][1], big_segments[-1][2] + sizes[name])
        else:
            big_segments.append((src[name][0], off[name], sizes[name]))

    def tiles_of(*names):
        return tuple(t for name in names for t in range(off[name] // bn, (off[name] + sizes[name]) // bn))

    tables = _rope_tables(s)

    w_in_t = jnp.swapaxes(w_in, 1, 2)
    x2 = x.reshape(m, d)
    hcur = _rmsnorm(x2, g_mix_pre[0])
    for l in range(depth):
        w_big = _regroup_in_weight(w_in_t, l, big_segments, n_big)
        small_rows = jnp.concatenate([w_in_t[l, src[n][0]:src[n][1]] for n in ("ik", "iw", "fa")], axis=0)
        small_rows = jnp.pad(small_rows, ((0, LANES - small_rows.shape[0]), (0, 0)))
        w_small = pl.pallas_call(
            _transpose_cast_kernel,
            in_specs=[pl.BlockSpec((1, LANES, d), lambda: (0, 0, 0))],
            out_specs=pl.BlockSpec((d, LANES), lambda: (0, 0)),
            out_shape=jax.ShapeDtypeStruct((d, LANES), BF16),
        )(small_rows[None])

        p = _inproj(hcur, w_big, tables, s, bn=bn, q_tiles=tiles_of("qa", "qb", "qc"),
                    rope128_tiles=tiles_of("qb", "kb", "qc", "kc"), rope64_tiles=tiles_of("iq"),
                    q_scale=HEAD_DIM ** -0.5 * LOG2E)
        small = _inproj_small(hcur, w_small, tables[2:], s)
        p3 = p.reshape(b, s, n_big)
        small3 = small.reshape(b, s, LANES)

        qx, kx = _forget_bias_columns(small3, b_forget[l], heads=fox_h, first_lane=IDX_HEAD_DIM + idx_h)
        o_a = _fox_attention(p3, qx, kx, heads=fox_h, q_off=off["qa"] // HEAD_DIM, k_off=off["ka"] // HEAD_DIM,
                             v_off=off["va"] // HEAD_DIM)

        ik = small3[:, :, :IDX_HEAD_DIM].astype(BF16)
        zeros = jnp.zeros_like(ik)
        ik_lo = jnp.concatenate([ik, zeros], axis=2)
        ik_hi = jnp.concatenate([zeros, ik], axis=2)
        o_b = _dsa_attention(p3, small3, ik_lo, ik_hi, heads=dsa_h, idx_heads=idx_h, iq_off=off["iq"] // iq_w,
                             q_off=off["qb"] // dsa_w, k_off=off["kb"] // dsa_w, v_off=off["vb"] // dsa_w, topk=topk)

        lam_init = 0.8 - 0.6 * math.exp(-0.3 * l)
        o_c = _diff_attention(p3, (lam_q1[l], lam_k1[l], lam_q2[l], lam_k2[l]), g_subln[l], heads=diff_h,
                              q_off=off["qc"], k_off=off["kc"], v_off=off["vc"], lam_init=lam_init)

        merged = _merge(o_a.reshape(m, fox_w), o_b.reshape(m, dsa_w), o_c.reshape(m, diffv_w),
                        _cast_weight(w_oa, l), _cast_weight(w_ob, l), _cast_weight(w_oc, l), p, b_gate[l],
                        gate_off=off["gl"], d=d)
        x2, hcur = _proj_norm_resid(merged, _cast_weight(w_out, l), g_mix_post[l], x2, g_ffn_pre[l])

        act = _gate_up_conv(hcur, _cast_weight(w_ffn_gate, l), _cast_weight(w_ffn_up, l),
                            conv_w[l].astype(F32), conv_b[l][None, :].astype(F32), s)
        w_down = _cast_weight(w_ffn_down, l)
        g_next = g_mix_pre[l + 1] if l + 1 < depth else g_mix_pre[0]
        x2, hcur = _proj_norm_resid(act, w_down, g_ffn_post[l], x2, g_next)
    return x2.reshape(b, s, d)
```

```python
import functools
import math

import jax
import jax.numpy as jnp
from jax import lax
from jax.experimental import pallas as pl
from jax.experimental.pallas import tpu as pltpu

CHUNK = 64
ROPE_THETA = 10000.0
NORM_EPS = 1e-6
HEAD_DIM = 128
IDX_HEAD_DIM = 64
DIFF_V_DIM = 256
INDEX_TOPK = 256
N_BRANCH = 3
CONV_WIDTH = 3

LANES = 128
VMEM_LIMIT_BYTES = 56 * 1024 * 1024

NORM_ROWS = 512
MATMUL_ROWS = 1024
INPROJ_COLS = 1024
GATE_UP_COLS = 512
MERGE_ROWS = 512
MERGE_COLS = 1024
PROJ_ROWS = 512
PROJ_K = 512
PROJ_COLS = 1024
EPILOGUE_ROWS = 64
ATTN_TILE = 1024
DSA_Q_TILE = 256
DSA_SCORE_TILE = 512
DSA_WIDE_TILE = 1024
FORGET_TILE = 512
MASK_VALUE = -1e30
INT32_MIN = -(2 ** 31)
LOG2E = math.log2(math.e)

F32 = jnp.float32
BF16 = jnp.bfloat16
_NT = (((1,), (1,)), ((), ()))


def _params(*semantics):
    return pltpu.CompilerParams(dimension_semantics=semantics, vmem_limit_bytes=VMEM_LIMIT_BYTES)


def _tile(dim, want):
    t = min(dim, want)
    assert dim % t == 0, (dim, want)
    return t


def _rmsnorm_kernel(x_ref, g_ref, o_ref):
    x = x_ref[...]
    ms = jnp.mean(x * x, axis=-1, keepdims=True)
    o_ref[...] = (x * lax.rsqrt(ms + NORM_EPS) * g_ref[...]).astype(o_ref.dtype)


def _rmsnorm(x, g):
    m, d = x.shape
    bm = _tile(m, NORM_ROWS)
    return pl.pallas_call(
        _rmsnorm_kernel,
        grid=(m // bm,),
        in_specs=[pl.BlockSpec((bm, d), lambda i: (i, 0)), pl.BlockSpec((1, d), lambda i: (0, 0))],
        out_specs=pl.BlockSpec((bm, d), lambda i: (i, 0)),
        out_shape=jax.ShapeDtypeStruct((m, d), BF16),
        compiler_params=_params("parallel"),
    )(x, g.reshape(1, d))


def _rope128(x, cos, sin):
    return x * cos + pltpu.roll(x, HEAD_DIM // 2, 1) * sin


def _rope64(x, cos, sin_lo, sin_hi):
    return x * cos + pltpu.roll(x, LANES - 32, 1) * sin_lo + pltpu.roll(x, 32, 1) * sin_hi


def _inproj_kernel(a_ref, w_ref, c128_ref, s128_ref, c64_ref, slo_ref, shi_ref, o_ref, *,
                   q_tiles, rope128_tiles, rope64_tiles, q_scale):
    j = pl.program_id(1)

    def member(tiles):
        hit = j < 0
        for t in tiles:
            hit = hit | (j == t)
        return hit

    acc = jnp.dot(a_ref[...], w_ref[...], preferred_element_type=F32)
    acc = acc * jnp.where(member(q_tiles), q_scale, 1.0).astype(F32)
    is128 = member(rope128_tiles)
    is64 = member(rope64_tiles)
    n_sub = acc.shape[1] // LANES
    o_ref[...] = acc.astype(o_ref.dtype)

    @pl.when(is128)
    def _():
        for c in range(n_sub):
            sl = slice(c * LANES, (c + 1) * LANES)
            o_ref[:, sl] = _rope128(acc[:, sl], c128_ref[...], s128_ref[...]).astype(o_ref.dtype)

    @pl.when(is64)
    def _():
        for c in range(n_sub):
            sl = slice(c * LANES, (c + 1) * LANES)
            o_ref[:, sl] = _rope64(acc[:, sl], c64_ref[...], slo_ref[...], shi_ref[...]).astype(o_ref.dtype)


def _inproj(h, w, tables, seq, *, bn, q_tiles, rope128_tiles, rope64_tiles, q_scale):
    m, k = h.shape
    n = w.shape[1]
    bm = _tile(seq, MATMUL_ROWS)
    pos_blocks = seq // bm
    tab_spec = pl.BlockSpec((bm, LANES), lambda i, j: (i % pos_blocks, 0))
    kern = functools.partial(_inproj_kernel, q_tiles=q_tiles, rope128_tiles=rope128_tiles,
                             rope64_tiles=rope64_tiles, q_scale=q_scale)
    return pl.pallas_call(
        kern,
        grid=(m // bm, n // bn),
        in_specs=[pl.BlockSpec((bm, k), lambda i, j: (i, 0)),
                  pl.BlockSpec((k, bn), lambda i, j: (0, j))] + [tab_spec] * 5,
        out_specs=pl.BlockSpec((bm, bn), lambda i, j: (i, j)),
        out_shape=jax.ShapeDtypeStruct((m, n), BF16),
        compiler_params=_params("parallel", "arbitrary"),
    )(h, w, *tables)


def _inproj_small_kernel(a_ref, w_ref, c64_ref, slo_ref, shi_ref, o_ref):
    acc = jnp.dot(a_ref[...], w_ref[...], preferred_element_type=F32)
    roped = _rope64(acc, c64_ref[...], slo_ref[...], shi_ref[...])
    lane = lax.broadcasted_iota(jnp.int32, acc.shape, 1)
    o_ref[...] = jnp.where(lane < IDX_HEAD_DIM, roped, acc)


def _inproj_small(h, w, tables64, seq):
    m, k = h.shape
    bm = _tile(seq, MATMUL_ROWS)
    pos_blocks = seq // bm
    tab_spec = pl.BlockSpec((bm, LANES), lambda i: (i % pos_blocks, 0))
    return pl.pallas_call(
        _inproj_small_kernel,
        grid=(m // bm,),
        in_specs=[pl.BlockSpec((bm, k), lambda i: (i, 0)),
                  pl.BlockSpec((k, LANES), lambda i: (0, 0))] + [tab_spec] * 3,
        out_specs=pl.BlockSpec((bm, LANES), lambda i: (i, 0)),
        out_shape=jax.ShapeDtypeStruct((m, LANES), F32),
        compiler_params=_params("parallel"),
    )(h, w, *tables64)


def _split3(x):
    hi = x.astype(BF16)
    rem = x - hi.astype(F32)
    mid = rem.astype(BF16)
    lo = (rem - mid.astype(F32)).astype(BF16)
    return hi, mid, lo


def _forget_bias_kernel(x_ref, b_ref, qx_ref, kx_ref, carry_ref, *, heads, first_lane):
    @pl.when(pl.program_id(1) == 0)
    def _():
        carry_ref[...] = jnp.zeros_like(carry_ref)

    x = x_ref[0] + b_ref[...]
    lf = jnp.minimum(x, 0.0) - jnp.log1p(jnp.exp(-jnp.abs(x)))
    t = lf.shape[0]
    tri = (lax.broadcasted_iota(jnp.int32, (t, t), 0) >= lax.broadcasted_iota(jnp.int32, (t, t), 1)).astype(BF16)
    loc = sum(jnp.dot(tri, piece, preferred_element_type=F32) for piece in _split3(lf))
    pieces = _split3((loc + carry_ref[...]) * LOG2E)
    carry_ref[...] = carry_ref[...] + loc[t - 1:t, :]

    row = lax.broadcasted_iota(jnp.int32, (LANES, LANES), 0)
    col = lax.broadcasted_iota(jnp.int32, (LANES, LANES), 1)
    lane = lax.broadcasted_iota(jnp.int32, (t, LANES), 1)
    n = len(pieces)
    for h in range(heads):
        def placed(first_col):
            return sum(jnp.dot(piece, ((row == first_lane + h) & (col == first_col + k)).astype(BF16),
                               preferred_element_type=F32) for k, piece in enumerate(pieces))
        qx_ref[0, h] = jnp.where((lane >= n) & (lane < 2 * n), 1.0, placed(0)).astype(qx_ref.dtype)
        kx_ref[0, h] = jnp.where(lane < n, 1.0, -placed(n)).astype(kx_ref.dtype)


def _forget_bias_columns(small3, b_forget, *, heads, first_lane):
    b, s, _ = small3.shape
    t = _tile(s, FORGET_TILE)
    bias = jnp.zeros((1, LANES), F32).at[0, first_lane:first_lane + heads].set(b_forget.astype(F32))
    out_spec = pl.BlockSpec((1, heads, t, LANES), lambda bi, i: (bi, 0, i, 0))
    return pl.pallas_call(
        functools.partial(_forget_bias_kernel, heads=heads, first_lane=first_lane),
        grid=(b, s // t),
        in_specs=[pl.BlockSpec((1, t, LANES), lambda bi, i: (bi, i, 0)), pl.BlockSpec((1, LANES), lambda bi, i: (0, 0))],
        out_specs=[out_spec] * 2,
        out_shape=[jax.ShapeDtypeStruct((b, heads, s, LANES), BF16)] * 2,
        scratch_shapes=[pltpu.VMEM((1, LANES), F32)],
        compiler_params=_params("parallel", "arbitrary"),
    )(small3, bias)


def _dot_row_halves(p, v):
    h = p.shape[0] // 2
    return jnp.concatenate([jnp.dot(p[:h], v, preferred_element_type=F32),
                            jnp.dot(p[h:], v, preferred_element_type=F32)], axis=0)


def _online_update(s, v, m, l, acc):
    m_new = jnp.maximum(m, jnp.max(s, axis=1, keepdims=True))
    alpha = jnp.exp2(m - m_new)
    p = jnp.exp2(s - m_new)
    l_new = alpha * l + jnp.sum(p, axis=1, keepdims=True)
    acc_new = alpha * acc + _dot_row_halves(p.astype(v.dtype), v)
    return m_new, l_new, acc_new


def _softmax_init(tq, dv):
    return (jnp.full((tq, 1), MASK_VALUE, F32), jnp.zeros((tq, 1), F32), jnp.zeros((tq, dv), F32))


def _fox_kernel(q_ref, qx_ref, k_ref, kx_ref, v_ref, o_ref, *, tq, tk, group):
    i = pl.program_id(2)
    n_full = (i * tq) // tk
    ones = jnp.ones((tk, HEAD_DIM), v_ref.dtype)

    def tile(j, carry, diagonal):
        off = pl.multiple_of(j * tk, tk)
        out = []
        for g in range(group):
            hs = slice(g * HEAD_DIM, (g + 1) * HEAD_DIM)
            m, l, acc = carry[g]
            q = jnp.concatenate([q_ref[0, :, hs], qx_ref[0, g]], axis=1)
            k = jnp.concatenate([k_ref[0, pl.ds(off, tk), hs], kx_ref[0, g, pl.ds(off, tk), :]], axis=1)
            s = lax.dot_general(q, k, _NT, preferred_element_type=F32)
            if diagonal:
                row = i * tq + lax.broadcasted_iota(jnp.int32, s.shape, 0)
                col = off + lax.broadcasted_iota(jnp.int32, s.shape, 1)
                s = jnp.where(row >= col, s, MASK_VALUE)
            m_new = jnp.maximum(m, jnp.max(s, axis=1, keepdims=True))
            alpha = jnp.exp2(m - m_new)
            p = jnp.exp2(s - m_new).astype(v_ref.dtype)
            v = jnp.concatenate([v_ref[0, pl.ds(off, tk), hs], ones], axis=1)
            pv = _dot_row_halves(p, v)
            out.append((m_new, alpha * l + pv[:, HEAD_DIM:], alpha * acc + pv[:, :HEAD_DIM]))
        return tuple(out)

    init = tuple((jnp.full((tq, 1), MASK_VALUE, F32), jnp.zeros((tq, HEAD_DIM), F32), jnp.zeros((tq, HEAD_DIM), F32))
                 for _ in range(group))
    carry = lax.fori_loop(0, n_full, lambda j, c: tile(j, c, False), init)
    final = tile(n_full, carry, True)
    for g in range(group):
        _, l, acc = final[g]
        o_ref[0, :, g * HEAD_DIM:(g + 1) * HEAD_DIM] = (acc / l).astype(o_ref.dtype)


def _fox_attention(p3, qx, kx, *, heads, q_off, k_off, v_off):
    b, s, _ = p3.shape
    tq = _tile(s, ATTN_TILE)
    tk = _tile(s, ATTN_TILE)
    group = 1
    gw = group * HEAD_DIM
    assert q_off % group == 0 and k_off % group == 0 and v_off % group == 0
    return pl.pallas_call(
        functools.partial(_fox_kernel, tq=tq, tk=tk, group=group),
        grid=(b, heads // group, s // tq),
        in_specs=[pl.BlockSpec((1, tq, gw), lambda bi, h, i: (bi, i, q_off // group + h)),
                  pl.BlockSpec((1, group, tq, HEAD_DIM), lambda bi, h, i: (bi, h, i, 0)),
                  pl.BlockSpec((1, s, gw), lambda bi, h, i: (bi, 0, k_off // group + h)),
                  pl.BlockSpec((1, group, s, HEAD_DIM), lambda bi, h, i: (bi, h, 0, 0)),
                  pl.BlockSpec((1, s, gw), lambda bi, h, i: (bi, 0, v_off // group + h))],
        out_specs=pl.BlockSpec((1, tq, gw), lambda bi, h, i: (bi, i, h)),
        out_shape=jax.ShapeDtypeStruct((b, s, heads * HEAD_DIM), BF16),
        compiler_params=_params("parallel", "parallel", "arbitrary"),
    )(p3, qx, p3, kx, p3)


def _diff_kernel(q_ref, k_ref, v_ref, lq1_ref, lk1_ref, lq2_ref, lk2_ref, g_ref, o_ref, *, tq, tk, group, lam_init):
    i = pl.program_id(2)
    n_full = (i * tq) // tk

    def tile(j, carry, diagonal):
        off = pl.multiple_of(j * tk, tk)
        if diagonal:
            row = (i * tq + lax.broadcasted_iota(jnp.int32, (tq, tk), 0)) // CHUNK
            col = (off + lax.broadcasted_iota(jnp.int32, (tq, tk), 1)) // CHUNK
            keep = row >= col
        out = []
        for g in range(group):
            v = v_ref[0, pl.ds(off, tk), g * DIFF_V_DIM:(g + 1) * DIFF_V_DIM]
            for half in range(2):
                hs = slice((2 * g + half) * HEAD_DIM, (2 * g + half + 1) * HEAD_DIM)
                s = lax.dot_general(q_ref[0, :, hs], k_ref[0, pl.ds(off, tk), hs], _NT, preferred_element_type=F32)
                if diagonal:
                    s = jnp.where(keep, s, MASK_VALUE)
                out.append(_online_update(s, v, *carry[2 * g + half]))
        return tuple(out)

    init = tuple(_softmax_init(tq, DIFF_V_DIM) for _ in range(2 * group))
    carry = lax.fori_loop(0, n_full, lambda j, c: tile(j, c, False), init)
    final = tile(n_full, carry, True)

    lam = (jnp.exp(jnp.sum(lq1_ref[...] * lk1_ref[...], axis=1, keepdims=True))
           - jnp.exp(jnp.sum(lq2_ref[...] * lk2_ref[...], axis=1, keepdims=True)) + lam_init)
    for g in range(group):
        (_, l1, a1), (_, l2, a2) = final[2 * g], final[2 * g + 1]
        o = a1 / l1 - lam * (a2 / l2)
        ms = jnp.mean(o * o, axis=-1, keepdims=True)
        o = o * lax.rsqrt(ms + NORM_EPS) * g_ref[...]
        o_ref[0, :, g * DIFF_V_DIM:(g + 1) * DIFF_V_DIM] = (o * (1.0 - lam_init)).astype(o_ref.dtype)


def _diff_attention(p3, lam_vecs, g_subln, *, heads, q_off, k_off, v_off, lam_init):
    b, s, _ = p3.shape
    tq = _tile(s, ATTN_TILE)
    tk = _tile(s, ATTN_TILE)
    group = 1
    gw = group * DIFF_V_DIM
    assert q_off % gw == 0 and k_off % gw == 0 and v_off % gw == 0
    vec_spec = pl.BlockSpec((1, HEAD_DIM), lambda bi, h, i: (0, 0))
    return pl.pallas_call(
        functools.partial(_diff_kernel, tq=tq, tk=tk, group=group, lam_init=lam_init),
        grid=(b, heads // group, s // tq),
        in_specs=[pl.BlockSpec((1, tq, gw), lambda bi, h, i: (bi, i, q_off // gw + h)),
                  pl.BlockSpec((1, s, gw), lambda bi, h, i: (bi, 0, k_off // gw + h)),
                  pl.BlockSpec((1, s, gw), lambda bi, h, i: (bi, 0, v_off // gw + h)),
                  vec_spec, vec_spec, vec_spec, vec_spec,
                  pl.BlockSpec((1, DIFF_V_DIM), lambda bi, h, i: (0, 0))],
        out_specs=pl.BlockSpec((1, tq, gw), lambda bi, h, i: (bi, i, h)),
        out_shape=jax.ShapeDtypeStruct((b, s, heads * DIFF_V_DIM), BF16),
        compiler_params=_params("parallel", "parallel", "arbitrary"),
    )(p3, p3, p3, *[v.reshape(1, HEAD_DIM).astype(F32) for v in lam_vecs],
      g_subln.reshape(1, DIFF_V_DIM).astype(F32))


def _ordered_key(x):
    bits = pltpu.bitcast(x, jnp.int32)
    return bits ^ (lax.shift_right_arithmetic(bits, 31) & 0x7FFFFFFF)


def _dsa_kernel(iq_ref, iw_ref, iklo_ref, ikhi_ref, q_ref, k_ref, v_ref, o_ref, keys_ref, *,
                tq, ts, tw, idx_heads, heads, topk, idx_scale):
    i = pl.program_id(1)
    n_score = (i * tq) // ts + 1
    n_wide = (i * tq) // tw + 1
    iw = iw_ref[0][:, IDX_HEAD_DIM:IDX_HEAD_DIM + idx_heads] * idx_scale
    row_chunk = (i * tq + lax.broadcasted_iota(jnp.int32, (tq, ts), 0)) // CHUNK
    col_iota = lax.broadcasted_iota(jnp.int32, (tq, ts), 1)

    def score_tile(j, _):
        off = pl.multiple_of(j * ts, ts)
        ik_lo = iklo_ref[0, pl.ds(off, ts), :]
        ik_hi = ikhi_ref[0, pl.ds(off, ts), :]
        sc = jnp.zeros((tq, ts), F32)
        for p in range(idx_heads // 2):
            a = iq_ref[0, :, p * LANES:(p + 1) * LANES]
            even = lax.dot_general(a, ik_lo, _NT, preferred_element_type=F32)
            odd = lax.dot_general(a, ik_hi, _NT, preferred_element_type=F32)
            sc = sc + jnp.maximum(even, 0.0) * iw[:, 2 * p:2 * p + 1]
            sc = sc + jnp.maximum(odd, 0.0) * iw[:, 2 * p + 1:2 * p + 2]
        valid = row_chunk >= (off + col_iota) // CHUNK
        keys_ref[:, pl.ds(off, ts)] = jnp.where(valid, _ordered_key(sc), INT32_MIN)
        return 0

    lax.fori_loop(0, n_score, score_tile, 0)

    def fill_tile(j, _):
        keys_ref[:, pl.ds(pl.multiple_of(j * ts, ts), ts)] = jnp.full((tq, ts), INT32_MIN, jnp.int32)
        return 0

    lax.fori_loop(n_score, n_wide * (tw // ts), fill_tile, 0)

    def count_ge(cand):
        def body(j, part):
            off = pl.multiple_of(j * tw, tw)
            ge = (keys_ref[:, pl.ds(off, tw)] >= cand).astype(jnp.int32)
            for c in range(tw // LANES):
                part = part + ge[:, c * LANES:(c + 1) * LANES]
            return part
        part = lax.fori_loop(0, n_wide, body, jnp.zeros((tq, LANES), jnp.int32))
        return jnp.sum(part, axis=1, keepdims=True)

    thr = jnp.where(count_ge(jnp.zeros((tq, 1), jnp.int32)) >= topk, 0, INT32_MIN).astype(jnp.int32)

    def bit_step(t, thr):
        cand = thr + lax.shift_left(jnp.int32(1), 30 - t)
        return jnp.where(count_ge(cand) >= topk, cand, thr)

    thr = lax.fori_loop(0, 31, bit_step, thr)
    thr = jnp.maximum(thr, INT32_MIN + 1)

    ones = jnp.ones((tw, HEAD_DIM), v_ref.dtype)

    def attend(j, carry):
        off = pl.multiple_of(j * tw, tw)
        out = []
        for h in range(heads):
            hs = slice(h * HEAD_DIM, (h + 1) * HEAD_DIM)
            m, l, acc = carry[h]
            s = lax.dot_general(q_ref[0, :, hs], k_ref[0, pl.ds(off, tw), hs], _NT, preferred_element_type=F32)
            s = jnp.where(keys_ref[:, pl.ds(off, tw)] >= thr, s, MASK_VALUE)
            m_new = jnp.maximum(m, jnp.max(s, axis=1, keepdims=True))
            alpha = jnp.exp2(m - m_new)
            p = jnp.exp2(s - m_new).astype(v_ref.dtype)
            v = jnp.concatenate([v_ref[0, pl.ds(off, tw), hs], ones], axis=1)
            pv = jnp.dot(p, v, preferred_element_type=F32)
            out.append((m_new, alpha * l + pv[:, HEAD_DIM:], alpha * acc + pv[:, :HEAD_DIM]))
        return tuple(out)

    init = tuple((jnp.full((tq, 1), MASK_VALUE, F32), jnp.zeros((tq, HEAD_DIM), F32), jnp.zeros((tq, HEAD_DIM), F32))
                 for _ in range(heads))
    final = lax.fori_loop(0, n_wide, attend, init)
    for h in range(heads):
        _, l, acc = final[h]
        o_ref[0, :, h * HEAD_DIM:(h + 1) * HEAD_DIM] = (acc / l).astype(o_ref.dtype)


def _dsa_attention(p3, small3, ik_lo, ik_hi, *, heads, idx_heads, iq_off, q_off, k_off, v_off, topk):
    b, s, _ = p3.shape
    tq = _tile(s, DSA_Q_TILE)
    ts = _tile(s, DSA_SCORE_TILE)
    tw = _tile(s, DSA_WIDE_TILE)
    width = heads * HEAD_DIM
    iq_width = idx_heads * IDX_HEAD_DIM
    resident = dict(pipeline_mode=pl.Buffered(1))
    kern = functools.partial(_dsa_kernel, tq=tq, ts=ts, tw=tw, idx_heads=idx_heads, heads=heads, topk=topk,
                             idx_scale=(IDX_HEAD_DIM ** -0.5) * (idx_heads ** -0.5))
    return pl.pallas_call(
        kern,
        grid=(b, s // tq),
        in_specs=[pl.BlockSpec((1, tq, iq_width), lambda bi, i: (bi, i, iq_off)),
                  pl.BlockSpec((1, tq, LANES), lambda bi, i: (bi, i, 0)),
                  pl.BlockSpec((1, s, LANES), lambda bi, i: (bi, 0, 0), **resident),
                  pl.BlockSpec((1, s, LANES), lambda bi, i: (bi, 0, 0), **resident),
                  pl.BlockSpec((1, tq, width), lambda bi, i: (bi, i, q_off)),
                  pl.BlockSpec((1, s, width), lambda bi, i: (bi, 0, k_off), **resident),
                  pl.BlockSpec((1, s, width), lambda bi, i: (bi, 0, v_off), **resident)],
        out_specs=pl.BlockSpec((1, tq, width), lambda bi, i: (bi, i, 0)),
        out_shape=jax.ShapeDtypeStruct((b, s, width), BF16),
        scratch_shapes=[pltpu.VMEM((tq, s), jnp.int32)],
        compiler_params=_params("parallel", "arbitrary"),
    )(p3, small3, ik_lo, ik_hi, p3, p3, p3)


def _merge_kernel(oa_ref, ob_ref, oc_ref, wa_ref, wb_ref, wc_ref, g0_ref, g1_ref, g2_ref, bg_ref, o_ref):
    def branch(o_r, w_r, g_r, n):
        y = jnp.dot(o_r[...], w_r[...], preferred_element_type=F32)
        return jax.nn.sigmoid(g_r[...].astype(F32) + bg_ref[n:n + 1, :]) * y

    out = branch(oa_ref, wa_ref, g0_ref, 0) + branch(ob_ref, wb_ref, g1_ref, 1) + branch(oc_ref, wc_ref, g2_ref, 2)
    o_ref[...] = out.astype(o_ref.dtype)


def _merge(o_a, o_b, o_c, w_a, w_b, w_c, p, b_gate, *, gate_off, d):
    m = o_a.shape[0]
    bm = _tile(m, MERGE_ROWS)
    bn = _tile(d, MERGE_COLS)
    nj = d // bn

    def lhs(o):
        return pl.BlockSpec((bm, o.shape[1]), lambda i, j: (i, 0))

    def rhs(w):
        return pl.BlockSpec((w.shape[0], bn), lambda i, j: (0, j))

    def gate(n):
        return pl.BlockSpec((bm, bn), lambda i, j: (i, gate_off // bn + n * nj + j))

    assert gate_off % bn == 0
    return pl.pallas_call(
        _merge_kernel,
        grid=(m // bm, nj),
        in_specs=[lhs(o_a), lhs(o_b), lhs(o_c), rhs(w_a), rhs(w_b), rhs(w_c), gate(0), gate(1), gate(2),
                  pl.BlockSpec((N_BRANCH, bn), lambda i, j: (0, j))],
        out_specs=pl.BlockSpec((bm, bn), lambda i, j: (i, j)),
        out_shape=jax.ShapeDtypeStruct((m, d), BF16),
        compiler_params=_params("parallel", "arbitrary"),
    )(o_a, o_b, o_c, w_a, w_b, w_c, p, p, p, b_gate.astype(F32))


def _proj_norm_resid_kernel(a_ref, w_ref, g_ref, r_ref, gn_ref, x_ref, h_ref, *, n_k, bn, k_tail):
    kk = pl.program_id(1)
    n = x_ref.shape[1]

    @pl.when(kk == 0)
    def _():
        x_ref[...] = jnp.zeros_like(x_ref)

    def accumulate(tail):
        a = a_ref[...]
        if tail:
            col = lax.broadcasted_iota(jnp.int32, a.shape, 1)
            a = jnp.where(col < k_tail, a.astype(F32), 0.0).astype(a.dtype)
        for c in range(n // bn):
            sl = slice(c * bn, (c + 1) * bn)
            w = w_ref[:, sl]
            if tail:
                row = lax.broadcasted_iota(jnp.int32, w.shape, 0)
                w = jnp.where(row < k_tail, w.astype(F32), 0.0).astype(w.dtype)
            x_ref[:, sl] += jnp.dot(a, w, preferred_element_type=F32)

    if k_tail == 0:
        accumulate(False)
    else:
        pl.when(kk < n_k - 1)(lambda: accumulate(False))
        pl.when(kk == n_k - 1)(lambda: accumulate(True))

    @pl.when(kk == n_k - 1)
    def _():
        rows = min(x_ref.shape[0], EPILOGUE_ROWS)

        def chunk(c, _):
            rs = pl.ds(pl.multiple_of(c * rows, rows), rows)
            y = x_ref[rs, :]
            ms = jnp.mean(y * y, axis=-1, keepdims=True)
            x_new = r_ref[rs, :] + y * lax.rsqrt(ms + NORM_EPS) * g_ref[...]
            x_ref[rs, :] = x_new
            ms2 = jnp.mean(x_new * x_new, axis=-1, keepdims=True)
            h_ref[rs, :] = (x_new * lax.rsqrt(ms2 + NORM_EPS) * gn_ref[...]).astype(h_ref.dtype)
            return 0

        lax.fori_loop(0, x_ref.shape[0] // rows, chunk, 0)


def _proj_norm_resid(a, w, g_post, resid, g_next):
    m, k = a.shape
    n = w.shape[1]
    bm = _tile(m, PROJ_ROWS)
    bk = min(k, PROJ_K)
    n_k = pl.cdiv(k, bk)
    vec = pl.BlockSpec((1, n), lambda i, kk: (0, 0))
    row = pl.BlockSpec((bm, n), lambda i, kk: (i, 0))
    return pl.pallas_call(
        functools.partial(_proj_norm_resid_kernel, n_k=n_k, bn=_tile(n, PROJ_COLS), k_tail=k % bk),
        grid=(m // bm, n_k),
        in_specs=[pl.BlockSpec((bm, bk), lambda i, kk: (i, kk)),
                  pl.BlockSpec((bk, n), lambda i, kk: (kk, 0)),
                  vec, row, vec],
        out_specs=[row, row],
        out_shape=[jax.ShapeDtypeStruct((m, n), F32), jax.ShapeDtypeStruct((m, n), BF16)],
        compiler_params=_params("parallel", "arbitrary"),
    )(a, w, g_post.reshape(1, n).astype(F32), resid, g_next.reshape(1, n).astype(F32))


def _gate_up_conv_kernel(a_ref, wg_ref, wu_ref, cw_ref, cb_ref, o_ref, prev_ref, *, blocks_per_seq):
    i = pl.program_id(0)
    j = pl.program_id(1)
    @pl.when((i % blocks_per_seq) == 0)
    def _():
        prev_ref[j] = jnp.zeros(prev_ref.shape[1:], F32)

    a = a_ref[...]
    bm = a.shape[0]
    z = jnp.dot(a, wg_ref[...], preferred_element_type=F32)
    u = jnp.dot(a, wu_ref[...], preferred_element_type=F32)
    prev2 = prev_ref[j, 0:1, :]
    prev1 = prev_ref[j, 1:2, :]
    prev_ref[j, 0:2, :] = z[bm - 2:bm, :]
    row = lax.broadcasted_iota(jnp.int32, z.shape, 0)
    z1 = jnp.where(row == 0, prev1, pltpu.roll(z, 1, 0))
    z2 = jnp.where(row == 0, prev2, jnp.where(row == 1, prev1, pltpu.roll(z, 2, 0)))
    zc = cw_ref[0:1, :] * z2 + cw_ref[1:2, :] * z1 + cw_ref[2:3, :] * z + cb_ref[...]
    gelu = 0.5 * zc * (1.0 + jnp.tanh(math.sqrt(2.0 / math.pi) * (zc + 0.044715 * (zc * zc * zc))))
    o_ref[...] = (gelu * u).astype(o_ref.dtype)


def _gate_up_conv(h, w_gate, w_up, conv_w, conv_b, seq):
    m, k = h.shape
    n = w_gate.shape[1]
    bm = _tile(seq, MATMUL_ROWS)
    bn = min(n, GATE_UP_COLS)
    n_j = pl.cdiv(n, bn)
    wspec = pl.BlockSpec((k, bn), lambda i, j: (0, j))
    return pl.pallas_call(
        functools.partial(_gate_up_conv_kernel, blocks_per_seq=seq // bm),
        grid=(m // bm, n_j),
        in_specs=[pl.BlockSpec((bm, k), lambda i, j: (i, 0)), wspec, wspec,
                  pl.BlockSpec((CONV_WIDTH, bn), lambda i, j: (0, j)),
                  pl.BlockSpec((1, bn), lambda i, j: (0, j))],
        out_specs=pl.BlockSpec((bm, bn), lambda i, j: (i, j)),
        out_shape=jax.ShapeDtypeStruct((m, n), BF16),
        scratch_shapes=[pltpu.VMEM((n_j, 8, bn), F32)],
        compiler_params=_params("arbitrary", "arbitrary"),
    )(h, w_gate, w_up, conv_w, conv_b)


def _rope_tables(s):
    pos = jnp.arange(s, dtype=F32)

    def cos_sin(d):
        inv_freq = ROPE_THETA ** (-jnp.arange(0, d, 2, dtype=F32) / d)
        ang = pos[:, None] * inv_freq[None, :]
        return jnp.cos(ang), jnp.sin(ang)

    c, sn = cos_sin(HEAD_DIM)
    c128 = jnp.concatenate([c, c], axis=1)
    s128 = jnp.concatenate([-sn, sn], axis=1)
    ci, si = cos_sin(IDX_HEAD_DIM)
    zero = jnp.zeros_like(si)
    c64 = jnp.concatenate([ci, ci, ci, ci], axis=1)
    s_lo = jnp.concatenate([-si, zero, -si, zero], axis=1)
    s_hi = jnp.concatenate([zero, si, zero, si], axis=1)
    return c128, s128, c64, s_lo, s_hi


WEIGHT_BLOCK_BYTES = 4 * 1024 * 1024


def _row_block(rows, row_bytes):
    br = 16
    while br * 2 <= rows and rows % (br * 2) == 0 and br * 2 * row_bytes <= WEIGHT_BLOCK_BYTES:
        br *= 2
    return min(br, rows)


def _cast_kernel(x_ref, o_ref):
    o_ref[...] = x_ref[0].astype(o_ref.dtype)


def _cast_weight(w, layer):
    _, k, n = w.shape
    br = _row_block(k, n * 4)
    return pl.pallas_call(
        _cast_kernel,
        grid=(k // br,),
        in_specs=[pl.BlockSpec((1, br, n), lambda i: (layer, i, 0))],
        out_specs=pl.BlockSpec((br, n), lambda i: (i, 0)),
        out_shape=jax.ShapeDtypeStruct((k, n), BF16),
        compiler_params=_params("parallel"),
    )(w)


F32_SUBLANES = 8


REGROUP_COLS = 256


def _transpose_cast_kernel(x_ref, o_ref):
    o_ref[...] = x_ref[0].T.astype(o_ref.dtype)


def _regroup_in_weight(w_in_t, layer, segments, n_big):
    _, _, k = w_in_t.shape
    rb = math.gcd(REGROUP_COLS, *[width for _, _, width in segments], *[dst for _, dst, _ in segments])
    assert sum(width for _, _, width in segments) == n_big and segments[0][1] == 0
    assert all(src % F32_SUBLANES == 0 for src, _, _ in segments), "segment starts must sit on f32 tile rows"

    def source_row(i):
        row = i * rb
        shift = 0
        for src, dst, _ in segments:
            row = row + jnp.where(i >= dst // rb, (src - dst) - shift, 0)
            shift = src - dst
        return pl.multiple_of(row, F32_SUBLANES)

    return pl.pallas_call(
        _transpose_cast_kernel,
        grid=(n_big // rb,),
        in_specs=[pl.BlockSpec((pl.Element(1), pl.Element(rb), pl.Element(k)),
                               lambda i: (layer, source_row(i), 0))],
        out_specs=pl.BlockSpec((k, rb), lambda i: (0, i)),
        out_shape=jax.ShapeDtypeStruct((k, n_big), BF16),
        compiler_params=_params("parallel"),
    )(w_in_t)


def kernel(x, g_mix_pre, g_mix_post, w_in, b_forget, b_gate, lam_q1, lam_k1, lam_q2, lam_k2, g_subln, w_oa, w_ob,
           w_oc, w_out, g_ffn_pre, g_ffn_post, w_ffn_gate, w_ffn_up, conv_w, conv_b, w_ffn_down):
    b, s, d = x.shape
    depth = w_in.shape[0]
    m = b * s
    fox_w, dsa_w, diffv_w = w_oa.shape[1], w_ob.shape[1], w_oc.shape[1]
    fox_h, dsa_h, diff_h = fox_w // HEAD_DIM, dsa_w // HEAD_DIM, diffv_w // DIFF_V_DIM
    diffqk_w = diff_h * 2 * HEAD_DIM
    known = 3 * fox_w + fox_h + 3 * dsa_w + IDX_HEAD_DIM + 2 * diffqk_w + diffv_w + N_BRANCH * d
    idx_h = (w_in.shape[2] - known) // (IDX_HEAD_DIM + 1)
    iq_w = idx_h * IDX_HEAD_DIM
    topk = min(INDEX_TOPK, s // 4)
    assert known + idx_h * (IDX_HEAD_DIM + 1) == w_in.shape[2]
    assert idx_h % 2 == 0 and IDX_HEAD_DIM + idx_h + fox_h <= LANES and s % CHUNK == 0

    sizes = dict(qa=fox_w, ka=fox_w, va=fox_w, fa=fox_h, qb=dsa_w, kb=dsa_w, vb=dsa_w, iq=iq_w, ik=IDX_HEAD_DIM,
                 iw=idx_h, qc=diffqk_w, kc=diffqk_w, vc=diffv_w, gl=N_BRANCH * d)
    src, pos = {}, 0
    for name, width in sizes.items():
        src[name] = (pos, pos + width)
        pos += width
    big_order = ("qa", "ka", "va", "qb", "kb", "vb", "iq", "qc", "kc", "vc", "gl")
    off, pos = {}, 0
    for name in big_order:
        off[name] = pos
        pos += sizes[name]
    n_big = pos
    bn = math.gcd(INPROJ_COLS, *[sizes[name] for name in big_order])
    assert bn % (2 * LANES) == 0
    assert off["iq"] % iq_w == 0 and off["qb"] % dsa_w == 0
    big_segments = []
    for name in big_order:
        if big_segments and (big_segments[-1][0] + big_segments[-1][2] == src[name][0]
                             and big_segments[-1][1] + big_segments[-1][2] == off[name]):
            big_segments[-1] = (big_segments[-1][0], big_segments[-1][1], big_segments[-1][2] + sizes[name])
        else:
            big_segments.append((src[name][0], off[name], sizes[name]))

    def tiles_of(*names):
        return tuple(t for name in names for t in range(off[name] // bn, (off[name] + sizes[name]) // bn))

    tables = _rope_tables(s)

    w_in_t = jnp.swapaxes(w_in, 1, 2)
    x2 = x.reshape(m, d)
    hcur = _rmsnorm(x2, g_mix_pre[0])
    for l in range(depth):
        w_big = _regroup_in_weight(w_in_t, l, big_segments, n_big)
        small_rows = jnp.concatenate([w_in_t[l, src[n][0]:src[n][1]] for n in ("ik", "iw", "fa")], axis=0)
        small_rows = jnp.pad(small_rows, ((0, LANES - small_rows.shape[0]), (0, 0)))
        w_small = pl.pallas_call(
            _transpose_cast_kernel,
            in_specs=[pl.BlockSpec((1, LANES, d), lambda: (0, 0, 0))],
            out_specs=pl.BlockSpec((d, LANES), lambda: (0, 0)),
            out_shape=jax.ShapeDtypeStruct((d, LANES), BF16),
        )(small_rows[None])

        p = _inproj(hcur, w_big, tables, s, bn=bn, q_tiles=tiles_of("qa", "qb", "qc"),
                    rope128_tiles=tiles_of("qb", "kb", "qc", "kc"), rope64_tiles=tiles_of("iq"),
                    q_scale=HEAD_DIM ** -0.5 * LOG2E)
        small = _inproj_small(hcur, w_small, tables[2:], s)
        p3 = p.reshape(b, s, n_big)
        small3 = small.reshape(b, s, LANES)

        qx, kx = _forget_bias_columns(small3, b_forget[l], heads=fox_h, first_lane=IDX_HEAD_DIM + idx_h)
        o_a = _fox_attention(p3, qx, kx, heads=fox_h, q_off=off["qa"] // HEAD_DIM, k_off=off["ka"] // HEAD_DIM,
                             v_off=off["va"] // HEAD_DIM)

        ik = small3[:, :, :IDX_HEAD_DIM].astype(BF16)
        zeros = jnp.zeros_like(ik)
        ik_lo = jnp.concatenate([ik, zeros], axis=2)
        ik_hi = jnp.concatenate([zeros, ik], axis=2)
        o_b = _dsa_attention(p3, small3, ik_lo, ik_hi, heads=dsa_h, idx_heads=idx_h, iq_off=off["iq"] // iq_w,
                             q_off=off["qb"] // dsa_w, k_off=off["kb"] // dsa_w, v_off=off["vb"] // dsa_w, topk=topk)

        lam_init = 0.8 - 0.6 * math.exp(-0.3 * l)
        o_c = _diff_attention(p3, (lam_q1[l], lam_k1[l], lam_q2[l], lam_k2[l]), g_subln[l], heads=diff_h,
                              q_off=off["qc"], k_off=off["kc"], v_off=off["vc"], lam_init=lam_init)

        merged = _merge(o_a.reshape(m, fox_w), o_b.reshape(m, dsa_w), o_c.reshape(m, diffv_w),
                        _cast_weight(w_oa, l), _cast_weight(w_ob, l), _cast_weight(w_oc, l), p, b_gate[l],
                        gate_off=off["gl"], d=d)
        x2, hcur = _proj_norm_resid(merged, _cast_weight(w_out, l), g_mix_post[l], x2, g_ffn_pre[l])

        act = _gate_up_conv(hcur, _cast_weight(w_ffn_gate, l), _cast_weight(w_ffn_up, l),
                            conv_w[l].astype(F32), conv_b[l][None, :].astype(F32), s)
        w_down = _cast_weight(w_ffn_down, l)
        g_next = g_mix_pre[l + 1] if l + 1 < depth else g_mix_pre[0]
        x2, hcur = _proj_norm_resid(act, w_down, g_ffn_post[l], x2, g_next)
    return x2.reshape(b, s, d)
```

```python
import functools
import math

import jax
import jax.numpy as jnp
from jax import lax
from jax.experimental import pallas as pl
from jax.experimental.pallas import tpu as pltpu

CHUNK = 64
ROPE_THETA = 10000.0
NORM_EPS = 1e-6
HEAD_DIM = 128
IDX_HEAD_DIM = 64
DIFF_V_DIM = 256
INDEX_TOPK = 256
N_BRANCH = 3
CONV_WIDTH = 3

LANES = 128
VMEM_LIMIT_BYTES = 56 * 1024 * 1024

NORM_ROWS = 512
MATMUL_ROWS = 1024
INPROJ_COLS = 1024
GATE_UP_COLS = 512
MERGE_ROWS = 512
MERGE_COLS = 1024
PROJ_ROWS = 512
PROJ_K = 512
PROJ_COLS = 1024
EPILOGUE_ROWS = 64
ATTN_TILE = 1024
DSA_Q_TILE = 256
DSA_SCORE_TILE = 512
DSA_WIDE_TILE = 1024
FORGET_TILE = 512
MASK_VALUE = -1e30
INT32_MIN = -(2 ** 31)
LOG2E = math.log2(math.e)

F32 = jnp.float32
BF16 = jnp.bfloat16
_NT = (((1,), (1,)), ((), ()))


def _params(*semantics):
    return pltpu.CompilerParams(dimension_semantics=semantics, vmem_limit_bytes=VMEM_LIMIT_BYTES)


def _tile(dim, want):
    t = min(dim, want)
    assert dim % t == 0, (dim, want)
    return t


def _rmsnorm_kernel(x_ref, g_ref, o_ref):
    x = x_ref[...]
    ms = jnp.mean(x * x, axis=-1, keepdims=True)
    o_ref[...] = (x * lax.rsqrt(ms + NORM_EPS) * g_ref[...]).astype(o_ref.dtype)


def _rmsnorm(x, g):
    m, d = x.shape
    bm = _tile(m, NORM_ROWS)
    return pl.pallas_call(
        _rmsnorm_kernel,
        grid=(m // bm,),
        in_specs=[pl.BlockSpec((bm, d), lambda i: (i, 0)), pl.BlockSpec((1, d), lambda i: (0, 0))],
        out_specs=pl.BlockSpec((bm, d), lambda i: (i, 0)),
        out_shape=jax.ShapeDtypeStruct((m, d), BF16),
        compiler_params=_params("parallel"),
    )(x, g.reshape(1, d))


def _rope128(x, cos, sin):
    return x * cos + pltpu.roll(x, HEAD_DIM // 2, 1) * sin


def _rope64(x, cos, sin_lo, sin_hi):
    return x * cos + pltpu.roll(x, LANES - 32, 1) * sin_lo + pltpu.roll(x, 32, 1) * sin_hi


def _inproj_kernel(a_ref, w_ref, c128_ref, s128_ref, c64_ref, slo_ref, shi_ref, o_ref, *,
                   q_tiles, rope128_tiles, rope64_tiles, q_scale):
    j = pl.program_id(1)

    def member(tiles):
        hit = j < 0
        for t in tiles:
            hit = hit | (j == t)
        return hit

    acc = jnp.dot(a_ref[...], w_ref[...], preferred_element_type=F32)
    acc = acc * jnp.where(member(q_tiles), q_scale, 1.0).astype(F32)
    is128 = member(rope128_tiles)
    is64 = member(rope64_tiles)
    n_sub = acc.shape[1] // LANES
    o_ref[...] = acc.astype(o_ref.dtype)

    @pl.when(is128)
    def _():
        for c in range(n_sub):
            sl = slice(c * LANES, (c + 1) * LANES)
            o_ref[:, sl] = _rope128(acc[:, sl], c128_ref[...], s128_ref[...]).astype(o_ref.dtype)

    @pl.when(is64)
    def _():
        for c in range(n_sub):
            sl = slice(c * LANES, (c + 1) * LANES)
            o_ref[:, sl] = _rope64(acc[:, sl], c64_ref[...], slo_ref[...], shi_ref[...]).astype(o_ref.dtype)


def _inproj(h, w, tables, seq, *, bn, q_tiles, rope128_tiles, rope64_tiles, q_scale):
    m, k = h.shape
    n = w.shape[1]
    bm = _tile(seq, MATMUL_ROWS)
    pos_blocks = seq // bm
    tab_spec = pl.BlockSpec((bm, LANES), lambda i, j: (i % pos_blocks, 0))
    kern = functools.partial(_inproj_kernel, q_tiles=q_tiles, rope128_tiles=rope128_tiles,
                             rope64_tiles=rope64_tiles, q_scale=q_scale)
    return pl.pallas_call(
        kern,
        grid=(m // bm, n // bn),
        in_specs=[pl.BlockSpec((bm, k), lambda i, j: (i, 0)),
                  pl.BlockSpec((k, bn), lambda i, j: (0, j))] + [tab_spec] * 5,
        out_specs=pl.BlockSpec((bm, bn), lambda i, j: (i, j)),
        out_shape=jax.ShapeDtypeStruct((m, n), BF16),
        compiler_params=_params("parallel", "arbitrary"),
    )(h, w, *tables)


def _inproj_small_kernel(a_ref, w_ref, c64_ref, slo_ref, shi_ref, o_ref):
    acc = jnp.dot(a_ref[...], w_ref[...], preferred_element_type=F32)
    roped = _rope64(acc, c64_ref[...], slo_ref[...], shi_ref[...])
    lane = lax.broadcasted_iota(jnp.int32, acc.shape, 1)
    o_ref[...] = jnp.where(lane < IDX_HEAD_DIM, roped, acc)


def _inproj_small(h, w, tables64, seq):
    m, k = h.shape
    bm = _tile(seq, MATMUL_ROWS)
    pos_blocks = seq // bm
    tab_spec = pl.BlockSpec((bm, LANES), lambda i: (i % pos_blocks, 0))
    return pl.pallas_call(
        _inproj_small_kernel,
        grid=(m // bm,),
        in_specs=[pl.BlockSpec((bm, k), lambda i: (i, 0)),
                  pl.BlockSpec((k, LANES), lambda i: (0, 0))] + [tab_spec] * 3,
        out_specs=pl.BlockSpec((bm, LANES), lambda i: (i, 0)),
        out_shape=jax.ShapeDtypeStruct((m, LANES), F32),
        compiler_params=_params("parallel"),
    )(h, w, *tables64)


def _split3(x):
    hi = x.astype(BF16)
    rem = x - hi.astype(F32)
    mid = rem.astype(BF16)
    lo = (rem - mid.astype(F32)).astype(BF16)
    return hi, mid, lo


def _forget_bias_kernel(x_ref, b_ref, qx_ref, kx_ref, carry_ref, *, heads, first_lane):
    @pl.when(pl.program_id(1) == 0)
    def _():
        carry_ref[...] = jnp.zeros_like(carry_ref)

    x = x_ref[0] + b_ref[...]
    lf = jnp.minimum(x, 0.0) - jnp.log1p(jnp.exp(-jnp.abs(x)))
    t = lf.shape[0]
    tri = (lax.broadcasted_iota(jnp.int32, (t, t), 0) >= lax.broadcasted_iota(jnp.int32, (t, t), 1)).astype(BF16)
    loc = sum(jnp.dot(tri, piece, preferred_element_type=F32) for piece in _split3(lf))
    pieces = _split3((loc + carry_ref[...]) * LOG2E)
    carry_ref[...] = carry_ref[...] + loc[t - 1:t, :]

    row = lax.broadcasted_iota(jnp.int32, (LANES, LANES), 0)
    col = lax.broadcasted_iota(jnp.int32, (LANES, LANES), 1)
    lane = lax.broadcasted_iota(jnp.int32, (t, LANES), 1)
    n = len(pieces)
    for h in range(heads):
        def placed(first_col):
            return sum(jnp.dot(piece, ((row == first_lane + h) & (col == first_col + k)).astype(BF16),
                               preferred_element_type=F32) for k, piece in enumerate(pieces))
        qx_ref[0, h] = jnp.where((lane >= n) & (lane < 2 * n), 1.0, placed(0)).astype(qx_ref.dtype)
        kx_ref[0, h] = jnp.where(lane < n, 1.0, -placed(n)).astype(kx_ref.dtype)


def _forget_bias_columns(small3, b_forget, *, heads, first_lane):
    b, s, _ = small3.shape
    t = _tile(s, FORGET_TILE)
    bias = jnp.zeros((1, LANES), F32).at[0, first_lane:first_lane + heads].set(b_forget.astype(F32))
    out_spec = pl.BlockSpec((1, heads, t, LANES), lambda bi, i: (bi, 0, i, 0))
    return pl.pallas_call(
        functools.partial(_forget_bias_kernel, heads=heads, first_lane=first_lane),
        grid=(b, s // t),
        in_specs=[pl.BlockSpec((1, t, LANES), lambda bi, i: (bi, i, 0)), pl.BlockSpec((1, LANES), lambda bi, i: (0, 0))],
        out_specs=[out_spec] * 2,
        out_shape=[jax.ShapeDtypeStruct((b, heads, s, LANES), BF16)] * 2,
        scratch_shapes=[pltpu.VMEM((1, LANES), F32)],
        compiler_params=_params("parallel", "arbitrary"),
    )(small3, bias)


def _dot_row_halves(p, v):
    h = p.shape[0] // 2
    return jnp.concatenate([jnp.dot(p[:h], v, preferred_element_type=F32),
                            jnp.dot(p[h:], v, preferred_element_type=F32)], axis=0)


def _online_update(s, v, m, l, acc):
    m_new = jnp.maximum(m, jnp.max(s, axis=1, keepdims=True))
    alpha = jnp.exp2(m - m_new)
    p = jnp.exp2(s - m_new)
    l_new = alpha * l + jnp.sum(p, axis=1, keepdims=True)
    acc_new = alpha * acc + _dot_row_halves(p.astype(v.dtype), v)
    return m_new, l_new, acc_new


def _softmax_init(tq, dv):
    return (jnp.full((tq, 1), MASK_VALUE, F32), jnp.zeros((tq, 1), F32), jnp.zeros((tq, dv), F32))


def _fox_kernel(q_ref, qx_ref, k_ref, kx_ref, v_ref, o_ref, *, tq, tk, group):
    i = pl.program_id(2)
    n_full = (i * tq) // tk
    ones = jnp.ones((tk, HEAD_DIM), v_ref.dtype)

    def tile(j, carry, diagonal):
        off = pl.multiple_of(j * tk, tk)
        out = []
        for g in range(group):
            hs = slice(g * HEAD_DIM, (g + 1) * HEAD_DIM)
            m, l, acc = carry[g]
            q = jnp.concatenate([q_ref[0, :, hs], qx_ref[0, g]], axis=1)
            k = jnp.concatenate([k_ref[0, pl.ds(off, tk), hs], kx_ref[0, g, pl.ds(off, tk), :]], axis=1)
            s = lax.dot_general(q, k, _NT, preferred_element_type=F32)
            if diagonal:
                row = i * tq + lax.broadcasted_iota(jnp.int32, s.shape, 0)
                col = off + lax.broadcasted_iota(jnp.int32, s.shape, 1)
                s = jnp.where(row >= col, s, MASK_VALUE)
            m_new = jnp.maximum(m, jnp.max(s, axis=1, keepdims=True))
            alpha = jnp.exp2(m - m_new)
            p = jnp.exp2(s - m_new).astype(v_ref.dtype)
            v = jnp.concatenate([v_ref[0, pl.ds(off, tk), hs], ones], axis=1)
            pv = _dot_row_halves(p, v)
            out.append((m_new, alpha * l + pv[:, HEAD_DIM:], alpha * acc + pv[:, :HEAD_DIM]))
        return tuple(out)

    init = tuple((jnp.full((tq, 1), MASK_VALUE, F32), jnp.zeros((tq, HEAD_DIM), F32), jnp.zeros((tq, HEAD_DIM), F32))
                 for _ in range(group))
    carry = lax.fori_loop(0, n_full, lambda j, c: tile(j, c, False), init)
    final = tile(n_full, carry, True)
    for g in range(group):
        _, l, acc = final[g]
        o_ref[0, :, g * HEAD_DIM:(g + 1) * HEAD_DIM] = (acc / l).astype(o_ref.dtype)


def _fox_attention(p3, qx, kx, *, heads, q_off, k_off, v_off):
    b, s, _ = p3.shape
    tq = _tile(s, ATTN_TILE)
    tk = _tile(s, ATTN_TILE)
    group = 1
    gw = group * HEAD_DIM
    assert q_off % group == 0 and k_off % group == 0 and v_off % group == 0
    return pl.pallas_call(
        functools.partial(_fox_kernel, tq=tq, tk=tk, group=group),
        grid=(b, heads // group, s // tq),
        in_specs=[pl.BlockSpec((1, tq, gw), lambda bi, h, i: (bi, i, q_off // group + h)),
                  pl.BlockSpec((1, group, tq, HEAD_DIM), lambda bi, h, i: (bi, h, i, 0)),
                  pl.BlockSpec((1, s, gw), lambda bi, h, i: (bi, 0, k_off // group + h)),
                  pl.BlockSpec((1, group, s, HEAD_DIM), lambda bi, h, i: (bi, h, 0, 0)),
                  pl.BlockSpec((1, s, gw), lambda bi, h, i: (bi, 0, v_off // group + h))],
        out_specs=pl.BlockSpec((1, tq, gw), lambda bi, h, i: (bi, i, h)),
        out_shape=jax.ShapeDtypeStruct((b, s, heads * HEAD_DIM), BF16),
        compiler_params=_params("parallel", "parallel", "arbitrary"),
    )(p3, qx, p3, kx, p3)


def _diff_kernel(q_ref, k_ref, v_ref, lq1_ref, lk1_ref, lq2_ref, lk2_ref, g_ref, o_ref, *, tq, tk, group, lam_init):
    i = pl.program_id(2)
    n_full = (i * tq) // tk

    def tile(j, carry, diagonal):
        off = pl.multiple_of(j * tk, tk)
        if diagonal:
            row = (i * tq + lax.broadcasted_iota(jnp.int32, (tq, tk), 0)) // CHUNK
            col = (off + lax.broadcasted_iota(jnp.int32, (tq, tk), 1)) // CHUNK
            keep = row >= col
        out = []
        for g in range(group):
            v = v_ref[0, pl.ds(off, tk), g * DIFF_V_DIM:(g + 1) * DIFF_V_DIM]
            for half in range(2):
                hs = slice((2 * g + half) * HEAD_DIM, (2 * g + half + 1) * HEAD_DIM)
                s = lax.dot_general(q_ref[0, :, hs], k_ref[0, pl.ds(off, tk), hs], _NT, preferred_element_type=F32)
                if diagonal:
                    s = jnp.where(keep, s, MASK_VALUE)
                out.append(_online_update(s, v, *carry[2 * g + half]))
        return tuple(out)

    init = tuple(_softmax_init(tq, DIFF_V_DIM) for _ in range(2 * group))
    carry = lax.fori_loop(0, n_full, lambda j, c: tile(j, c, False), init)
    final = tile(n_full, carry, True)

    lam = (jnp.exp(jnp.sum(lq1_ref[...] * lk1_ref[...], axis=1, keepdims=True))
           - jnp.exp(jnp.sum(lq2_ref[...] * lk2_ref[...], axis=1, keepdims=True)) + lam_init)
    for g in range(group):
        (_, l1, a1), (_, l2, a2) = final[2 * g], final[2 * g + 1]
        o = a1 / l1 - lam * (a2 / l2)
        ms = jnp.mean(o * o, axis=-1, keepdims=True)
        o = o * lax.rsqrt(ms + NORM_EPS) * g_ref[...]
        o_ref[0, :, g * DIFF_V_DIM:(g + 1) * DIFF_V_DIM] = (o * (1.0 - lam_init)).astype(o_ref.dtype)


def _diff_attention(p3, lam_vecs, g_subln, *, heads, q_off, k_off, v_off, lam_init):
    b, s, _ = p3.shape
    tq = _tile(s, ATTN_TILE)
    tk = _tile(s, ATTN_TILE)
    group = 1
    gw = group * DIFF_V_DIM
    assert q_off % gw == 0 and k_off % gw == 0 and v_off % gw == 0
    vec_spec = pl.BlockSpec((1, HEAD_DIM), lambda bi, h, i: (0, 0))
    return pl.pallas_call(
        functools.partial(_diff_kernel, tq=tq, tk=tk, group=group, lam_init=lam_init),
        grid=(b, heads // group, s // tq),
        in_specs=[pl.BlockSpec((1, tq, gw), lambda bi, h, i: (bi, i, q_off // gw + h)),
                  pl.BlockSpec((1, s, gw), lambda bi, h, i: (bi, 0, k_off // gw + h)),
                  pl.BlockSpec((1, s, gw), lambda bi, h, i: (bi, 0, v_off // gw + h)),
                  vec_spec, vec_spec, vec_spec, vec_spec,
                  pl.BlockSpec((1, DIFF_V_DIM), lambda bi, h, i: (0, 0))],
        out_specs=pl.BlockSpec((1, tq, gw), lambda bi, h, i: (bi, i, h)),
        out_shape=jax.ShapeDtypeStruct((b, s, heads * DIFF_V_DIM), BF16),
        compiler_params=_params("parallel", "parallel", "arbitrary"),
    )(p3, p3, p3, *[v.reshape(1, HEAD_DIM).astype(F32) for v in lam_vecs],
      g_subln.reshape(1, DIFF_V_DIM).astype(F32))


def _ordered_key(x):
    bits = pltpu.bitcast(x, jnp.int32)
    return bits ^ (lax.shift_right_arithmetic(bits, 31) & 0x7FFFFFFF)


def _dsa_kernel(iq_ref, iw_ref, iklo_ref, ikhi_ref, q_ref, k_ref, v_ref, o_ref, keys_ref, *,
                tq, ts, tw, idx_heads, heads, topk, idx_scale):
    i = pl.program_id(1)
    n_score = (i * tq) // ts + 1
    n_wide = (i * tq) // tw + 1
    iw = iw_ref[0][:, IDX_HEAD_DIM:IDX_HEAD_DIM + idx_heads] * idx_scale
    row_chunk = (i * tq + lax.broadcasted_iota(jnp.int32, (tq, ts), 0)) // CHUNK
    col_iota = lax.broadcasted_iota(jnp.int32, (tq, ts), 1)

    def score_tile(j, _):
        off = pl.multiple_of(j * ts, ts)
        ik_lo = iklo_ref[0, pl.ds(off, ts), :]
        ik_hi = ikhi_ref[0, pl.ds(off, ts), :]
        sc = jnp.zeros((tq, ts), F32)
        for p in range(idx_heads // 2):
            a = iq_ref[0, :, p * LANES:(p + 1) * LANES]
            even = lax.dot_general(a, ik_lo, _NT, preferred_element_type=F32)
            odd = lax.dot_general(a, ik_hi, _NT, preferred_element_type=F32)
            sc = sc + jnp.maximum(even, 0.0) * iw[:, 2 * p:2 * p + 1]
            sc = sc + jnp.maximum(odd, 0.0) * iw[:, 2 * p + 1:2 * p + 2]
        valid = row_chunk >= (off + col_iota) // CHUNK
        keys_ref[:, pl.ds(off, ts)] = jnp.where(valid, _ordered_key(sc), INT32_MIN)
        return 0

    lax.fori_loop(0, n_score, score_tile, 0)

    def fill_tile(j, _):
        keys_ref[:, pl.ds(pl.multiple_of(j * ts, ts), ts)] = jnp.full((tq, ts), INT32_MIN, jnp.int32)
        return 0

    lax.fori_loop(n_score, n_wide * (tw // ts), fill_tile, 0)

    def count_ge(cand):
        def body(j, part):
            off = pl.multiple_of(j * tw, tw)
            ge = (keys_ref[:, pl.ds(off, tw)] >= cand).astype(jnp.int32)
            for c in range(tw // LANES):
                part = part + ge[:, c * LANES:(c + 1) * LANES]
            return part
        part = lax.fori_loop(0, n_wide, body, jnp.zeros((tq, LANES), jnp.int32))
        return jnp.sum(part, axis=1, keepdims=True)

    thr = jnp.where(count_ge(jnp.zeros((tq, 1), jnp.int32)) >= topk, 0, INT32_MIN).astype(jnp.int32)

    def bit_step(t, thr):
        cand = thr + lax.shift_left(jnp.int32(1), 30 - t)
        return jnp.where(count_ge(cand) >= topk, cand, thr)

    thr = lax.fori_loop(0, 31, bit_step, thr)
    thr = jnp.maximum(thr, INT32_MIN + 1)

    n_selected = count_ge(thr)
    has_tie = n_selected > topk

    @pl.when(jnp.max(has_tie.astype(jnp.int32)) > 0)
    def _():
        need = topk - count_ge(thr + 1)
        lane_idx = lax.broadcasted_iota(jnp.int32, (tq, tw), 1)

        def ties_before(bound):
            def body(j, part):
                off = pl.multiple_of(j * tw, tw)
                hit = (keys_ref[:, pl.ds(off, tw)] == thr) & (off + lane_idx < bound)
                hit = hit.astype(jnp.int32)
                for c in range(tw // LANES):
                    part = part + hit[:, c * LANES:(c + 1) * LANES]
                return part
            part = lax.fori_loop(0, n_wide, body, jnp.zeros((tq, LANES), jnp.int32))
            return jnp.sum(part, axis=1, keepdims=True)

        index_bits = keys_ref.shape[1].bit_length()

        def index_step(t, last):
            cand = last + lax.shift_left(jnp.int32(1), index_bits - 1 - t)
            return jnp.where(ties_before(cand) < need, cand, last)

        last = lax.fori_loop(0, index_bits, index_step, jnp.zeros((tq, 1), jnp.int32))

        def demote(j, _):
            off = pl.multiple_of(j * tw, tw)
            keys = keys_ref[:, pl.ds(off, tw)]
            drop = has_tie & (keys == thr) & (off + lane_idx > last)
            keys_ref[:, pl.ds(off, tw)] = jnp.where(drop, INT32_MIN, keys)
            return 0

        lax.fori_loop(0, n_wide, demote, 0)

    ones = jnp.ones((tw, HEAD_DIM), v_ref.dtype)

    def attend(j, carry):
        off = pl.multiple_of(j * tw, tw)
        out = []
        for h in range(heads):
            hs = slice(h * HEAD_DIM, (h + 1) * HEAD_DIM)
            m, l, acc = carry[h]
            s = lax.dot_general(q_ref[0, :, hs], k_ref[0, pl.ds(off, tw), hs], _NT, preferred_element_type=F32)
            s = jnp.where(keys_ref[:, pl.ds(off, tw)] >= thr, s, MASK_VALUE)
            m_new = jnp.maximum(m, jnp.max(s, axis=1, keepdims=True))
            alpha = jnp.exp2(m - m_new)
            p = jnp.exp2(s - m_new).astype(v_ref.dtype)
            v = jnp.concatenate([v_ref[0, pl.ds(off, tw), hs], ones], axis=1)
            pv = jnp.dot(p, v, preferred_element_type=F32)
            out.append((m_new, alpha * l + pv[:, HEAD_DIM:], alpha * acc + pv[:, :HEAD_DIM]))
        return tuple(out)

    init = tuple((jnp.full((tq, 1), MASK_VALUE, F32), jnp.zeros((tq, HEAD_DIM), F32), jnp.zeros((tq, HEAD_DIM), F32))
                 for _ in range(heads))
    final = lax.fori_loop(0, n_wide, attend, init)
    for h in range(heads):
        _, l, acc = final[h]
        o_ref[0, :, h * HEAD_DIM:(h + 1) * HEAD_DIM] = (acc / l).astype(o_ref.dtype)


def _dsa_attention(p3, small3, ik_lo, ik_hi, *, heads, idx_heads, iq_off, q_off, k_off, v_off, topk):
    b, s, _ = p3.shape
    tq = _tile(s, DSA_Q_TILE)
    ts = _tile(s, DSA_SCORE_TILE)
    tw = _tile(s, DSA_WIDE_TILE)
    width = heads * HEAD_DIM
    iq_width = idx_heads * IDX_HEAD_DIM
    resident = dict(pipeline_mode=pl.Buffered(1))
    kern = functools.partial(_dsa_kernel, tq=tq, ts=ts, tw=tw, idx_heads=idx_heads, heads=heads, topk=topk,
                             idx_scale=(IDX_HEAD_DIM ** -0.5) * (idx_heads ** -0.5))
    return pl.pallas_call(
        kern,
        grid=(b, s // tq),
        in_specs=[pl.BlockSpec((1, tq, iq_width), lambda bi, i: (bi, i, iq_off)),
                  pl.BlockSpec((1, tq, LANES), lambda bi, i: (bi, i, 0)),
                  pl.BlockSpec((1, s, LANES), lambda bi, i: (bi, 0, 0), **resident),
                  pl.BlockSpec((1, s, LANES), lambda bi, i: (bi, 0, 0), **resident),
                  pl.BlockSpec((1, tq, width), lambda bi, i: (bi, i, q_off)),
                  pl.BlockSpec((1, s, width), lambda bi, i: (bi, 0, k_off), **resident),
                  pl.BlockSpec((1, s, width), lambda bi, i: (bi, 0, v_off), **resident)],
        out_specs=pl.BlockSpec((1, tq, width), lambda bi, i: (bi, i, 0)),
        out_shape=jax.ShapeDtypeStruct((b, s, width), BF16),
        scratch_shapes=[pltpu.VMEM((tq, s), jnp.int32)],
        compiler_params=_params("parallel", "arbitrary"),
    )(p3, small3, ik_lo, ik_hi, p3, p3, p3)


def _merge_kernel(oa_ref, ob_ref, oc_ref, wa_ref, wb_ref, wc_ref, g0_ref, g1_ref, g2_ref, bg_ref, o_ref):
    def branch(o_r, w_r, g_r, n):
        y = jnp.dot(o_r[...], w_r[...], preferred_element_type=F32)
        return jax.nn.sigmoid(g_r[...].astype(F32) + bg_ref[n:n + 1, :]) * y

    out = branch(oa_ref, wa_ref, g0_ref, 0) + branch(ob_ref, wb_ref, g1_ref, 1) + branch(oc_ref, wc_ref, g2_ref, 2)
    o_ref[...] = out.astype(o_ref.dtype)


def _merge(o_a, o_b, o_c, w_a, w_b, w_c, p, b_gate, *, gate_off, d):
    m = o_a.shape[0]
    bm = _tile(m, MERGE_ROWS)
    bn = _tile(d, MERGE_COLS)
    nj = d // bn

    def lhs(o):
        return pl.BlockSpec((bm, o.shape[1]), lambda i, j: (i, 0))

    def rhs(w):
        return pl.BlockSpec((w.shape[0], bn), lambda i, j: (0, j))

    def gate(n):
        return pl.BlockSpec((bm, bn), lambda i, j: (i, gate_off // bn + n * nj + j))

    assert gate_off % bn == 0
    return pl.pallas_call(
        _merge_kernel,
        grid=(m // bm, nj),
        in_specs=[lhs(o_a), lhs(o_b), lhs(o_c), rhs(w_a), rhs(w_b), rhs(w_c), gate(0), gate(1), gate(2),
                  pl.BlockSpec((N_BRANCH, bn), lambda i, j: (0, j))],
        out_specs=pl.BlockSpec((bm, bn), lambda i, j: (i, j)),
        out_shape=jax.ShapeDtypeStruct((m, d), BF16),
        compiler_params=_params("parallel", "arbitrary"),
    )(o_a, o_b, o_c, w_a, w_b, w_c, p, p, p, b_gate.astype(F32))


def _proj_norm_resid_kernel(a_ref, w_ref, g_ref, r_ref, gn_ref, x_ref, h_ref, *, n_k, bn, k_tail):
    kk = pl.program_id(1)
    n = x_ref.shape[1]

    @pl.when(kk == 0)
    def _():
        x_ref[...] = jnp.zeros_like(x_ref)

    def accumulate(tail):
        a = a_ref[...]
        if tail:
            col = lax.broadcasted_iota(jnp.int32, a.shape, 1)
            a = jnp.where(col < k_tail, a.astype(F32), 0.0).astype(a.dtype)
        for c in range(n // bn):
            sl = slice(c * bn, (c + 1) * bn)
            w = w_ref[:, sl]
            if tail:
                row = lax.broadcasted_iota(jnp.int32, w.shape, 0)
                w = jnp.where(row < k_tail, w.astype(F32), 0.0).astype(w.dtype)
            x_ref[:, sl] += jnp.dot(a, w, preferred_element_type=F32)

    if k_tail == 0:
        accumulate(False)
    else:
        pl.when(kk < n_k - 1)(lambda: accumulate(False))
        pl.when(kk == n_k - 1)(lambda: accumulate(True))

    @pl.when(kk == n_k - 1)
    def _():
        rows = min(x_ref.shape[0], EPILOGUE_ROWS)

        def chunk(c, _):
            rs = pl.ds(pl.multiple_of(c * rows, rows), rows)
            y = x_ref[rs, :]
            ms = jnp.mean(y * y, axis=-1, keepdims=True)
            x_new = r_ref[rs, :] + y * lax.rsqrt(ms + NORM_EPS) * g_ref[...]
            x_ref[rs, :] = x_new
            ms2 = jnp.mean(x_new * x_new, axis=-1, keepdims=True)
            h_ref[rs, :] = (x_new * lax.rsqrt(ms2 + NORM_EPS) * gn_ref[...]).astype(h_ref.dtype)
            return 0

        lax.fori_loop(0, x_ref.shape[0] // rows, chunk, 0)


def _proj_norm_resid(a, w, g_post, resid, g_next):
    m, k = a.shape
    n = w.shape[1]
    bm = _tile(m, PROJ_ROWS)
    bk = min(k, PROJ_K)
    n_k = pl.cdiv(k, bk)
    vec = pl.BlockSpec((1, n), lambda i, kk: (0, 0))
    row = pl.BlockSpec((bm, n), lambda i, kk: (i, 0))
    return pl.pallas_call(
        functools.partial(_proj_norm_resid_kernel, n_k=n_k, bn=_tile(n, PROJ_COLS), k_tail=k % bk),
        grid=(m // bm, n_k),
        in_specs=[pl.BlockSpec((bm, bk), lambda i, kk: (i, kk)),
                  pl.BlockSpec((bk, n), lambda i, kk: (kk, 0)),
                  vec, row, vec],
        out_specs=[row, row],
        out_shape=[jax.ShapeDtypeStruct((m, n), F32), jax.ShapeDtypeStruct((m, n), BF16)],
        compiler_params=_params("parallel", "arbitrary"),
    )(a, w, g_post.reshape(1, n).astype(F32), resid, g_next.reshape(1, n).astype(F32))


def _gate_up_conv_kernel(a_ref, wg_ref, wu_ref, cw_ref, cb_ref, o_ref, prev_ref, *, blocks_per_seq):
    i = pl.program_id(0)
    j = pl.program_id(1)
    @pl.when((i % blocks_per_seq) == 0)
    def _():
        prev_ref[j] = jnp.zeros(prev_ref.shape[1:], F32)

    a = a_ref[...]
    bm = a.shape[0]
    z = jnp.dot(a, wg_ref[...], preferred_element_type=F32)
    u = jnp.dot(a, wu_ref[...], preferred_element_type=F32)
    prev2 = prev_ref[j, 0:1, :]
    prev1 = prev_ref[j, 1:2, :]
    prev_ref[j, 0:2, :] = z[bm - 2:bm, :]
    row = lax.broadcasted_iota(jnp.int32, z.shape, 0)
    z1 = jnp.where(row == 0, prev1, pltpu.roll(z, 1, 0))
    z2 = jnp.where(row == 0, prev2, jnp.where(row == 1, prev1, pltpu.roll(z, 2, 0)))
    zc = cw_ref[0:1, :] * z2 + cw_ref[1:2, :] * z1 + cw_ref[2:3, :] * z + cb_ref[...]
    gelu = 0.5 * zc * (1.0 + jnp.tanh(math.sqrt(2.0 / math.pi) * (zc + 0.044715 * (zc * zc * zc))))
    o_ref[...] = (gelu * u).astype(o_ref.dtype)


def _gate_up_conv(h, w_gate, w_up, conv_w, conv_b, seq):
    m, k = h.shape
    n = w_gate.shape[1]
    bm = _tile(seq, MATMUL_ROWS)
    bn = min(n, GATE_UP_COLS)
    n_j = pl.cdiv(n, bn)
    wspec = pl.BlockSpec((k, bn), lambda i, j: (0, j))
    return pl.pallas_call(
        functools.partial(_gate_up_conv_kernel, blocks_per_seq=seq // bm),
        grid=(m // bm, n_j),
        in_specs=[pl.BlockSpec((bm, k), lambda i, j: (i, 0)), wspec, wspec,
                  pl.BlockSpec((CONV_WIDTH, bn), lambda i, j: (0, j)),
                  pl.BlockSpec((1, bn), lambda i, j: (0, j))],
        out_specs=pl.BlockSpec((bm, bn), lambda i, j: (i, j)),
        out_shape=jax.ShapeDtypeStruct((m, n), BF16),
        scratch_shapes=[pltpu.VMEM((n_j, 8, bn), F32)],
        compiler_params=_params("arbitrary", "arbitrary"),
    )(h, w_gate, w_up, conv_w, conv_b)


def _rope_tables(s):
    pos = jnp.arange(s, dtype=F32)

    def cos_sin(d):
        inv_freq = ROPE_THETA ** (-jnp.arange(0, d, 2, dtype=F32) / d)
        ang = pos[:, None] * inv_freq[None, :]
        return jnp.cos(ang), jnp.sin(ang)

    c, sn = cos_sin(HEAD_DIM)
    c128 = jnp.concatenate([c, c], axis=1)
    s128 = jnp.concatenate([-sn, sn], axis=1)
    ci, si = cos_sin(IDX_HEAD_DIM)
    zero = jnp.zeros_like(si)
    c64 = jnp.concatenate([ci, ci, ci, ci], axis=1)
    s_lo = jnp.concatenate([-si, zero, -si, zero], axis=1)
    s_hi = jnp.concatenate([zero, si, zero, si], axis=1)
    return c128, s128, c64, s_lo, s_hi


WEIGHT_BLOCK_BYTES = 4 * 1024 * 1024


def _row_block(rows, row_bytes):
    br = 16
    while br * 2 <= rows and rows % (br * 2) == 0 and br * 2 * row_bytes <= WEIGHT_BLOCK_BYTES:
        br *= 2
    return min(br, rows)


def _cast_kernel(x_ref, o_ref):
    o_ref[...] = x_ref[0].astype(o_ref.dtype)


def _cast_weight(w, layer):
    _, k, n = w.shape
    br = _row_block(k, n * 4)
    return pl.pallas_call(
        _cast_kernel,
        grid=(k // br,),
        in_specs=[pl.BlockSpec((1, br, n), lambda i: (layer, i, 0))],
        out_specs=pl.BlockSpec((br, n), lambda i: (i, 0)),
        out_shape=jax.ShapeDtypeStruct((k, n), BF16),
        compiler_params=_params("parallel"),
    )(w)


F32_SUBLANES = 8


REGROUP_COLS = 256


def _transpose_cast_kernel(x_ref, o_ref):
    o_ref[...] = x_ref[0].T.astype(o_ref.dtype)


def _regroup_in_weight(w_in_t, layer, segments, n_big):
    _, _, k = w_in_t.shape
    rb = math.gcd(REGROUP_COLS, *[width for _, _, width in segments], *[dst for _, dst, _ in segments])
    assert sum(width for _, _, width in segments) == n_big and segments[0][1] == 0
    assert all(src % F32_SUBLANES == 0 for src, _, _ in segments), "segment starts must sit on f32 tile rows"

    def source_row(i):
        row = i * rb
        shift = 0
        for src, dst, _ in segments:
            row = row + jnp.where(i >= dst // rb, (src - dst) - shift, 0)
            shift = src - dst
        return pl.multiple_of(row, F32_SUBLANES)

    return pl.pallas_call(
        _transpose_cast_kernel,
        grid=(n_big // rb,),
        in_specs=[pl.BlockSpec((pl.Element(1), pl.Element(rb), pl.Element(k)),
                               lambda i: (layer, source_row(i), 0))],
        out_specs=pl.BlockSpec((k, rb), lambda i: (0, i)),
        out_shape=jax.ShapeDtypeStruct((k, n_big), BF16),
        compiler_params=_params("parallel"),
    )(w_in_t)


def kernel(x, g_mix_pre, g_mix_post, w_in, b_forget, b_gate, lam_q1, lam_k1, lam_q2, lam_k2, g_subln, w_oa, w_ob,
           w_oc, w_out, g_ffn_pre, g_ffn_post, w_ffn_gate, w_ffn_up, conv_w, conv_b, w_ffn_down):
    b, s, d = x.shape
    depth = w_in.shape[0]
    m = b * s
    fox_w, dsa_w, diffv_w = w_oa.shape[1], w_ob.shape[1], w_oc.shape[1]
    fox_h, dsa_h, diff_h = fox_w // HEAD_DIM, dsa_w // HEAD_DIM, diffv_w // DIFF_V_DIM
    diffqk_w = diff_h * 2 * HEAD_DIM
    known = 3 * fox_w + fox_h + 3 * dsa_w + IDX_HEAD_DIM + 2 * diffqk_w + diffv_w + N_BRANCH * d
    idx_h = (w_in.shape[2] - known) // (IDX_HEAD_DIM + 1)
    iq_w = idx_h * IDX_HEAD_DIM
    topk = min(INDEX_TOPK, s // 4)
    assert known + idx_h * (IDX_HEAD_DIM + 1) == w_in.shape[2]
    assert idx_h % 2 == 0 and IDX_HEAD_DIM + idx_h + fox_h <= LANES and s % CHUNK == 0

    sizes = dict(qa=fox_w, ka=fox_w, va=fox_w, fa=fox_h, qb=dsa_w, kb=dsa_w, vb=dsa_w, iq=iq_w, ik=IDX_HEAD_DIM,
                 iw=idx_h, qc=diffqk_w, kc=diffqk_w, vc=diffv_w, gl=N_BRANCH * d)
    src, pos = {}, 0
    for name, width in sizes.items():
        src[name] = (pos, pos + width)
        pos += width
    big_order = ("qa", "ka", "va", "qb", "kb", "vb", "iq", "qc", "kc", "vc", "gl")
    off, pos = {}, 0
    for name in big_order:
        off[name] = pos
        pos += sizes[name]
    n_big = pos
    bn = math.gcd(INPROJ_COLS, *[sizes[name] for name in big_order])
    assert bn % (2 * LANES) == 0
    assert off["iq"] % iq_w == 0 and off["qb"] % dsa_w == 0
    big_segments = []
    for name in big_order:
        if big_segments and (big_segments[-1][0] + big_segments[-1][2] == src[name][0]
                             and big_segments[-1][1] + big_segments[-1][2] == off[name]):
            big_segments[-1] = (big_segments[-1][0], big_segments[-1][1], big_segments[-1][2] + sizes[name])
        else:
            big_segments.append((src[name][0], off[name], sizes[name]))

    def tiles_of(*names):
        return tuple(t for name in names for t in range(off[name] // bn, (off[name] + sizes[name]) // bn))

    tables = _rope_tables(s)

    w_in_t = jnp.swapaxes(w_in, 1, 2)
    x2 = x.reshape(m, d)
    hcur = _rmsnorm(x2, g_mix_pre[0])
    for l in range(depth):
        w_big = _regroup_in_weight(w_in_t, l, big_segments, n_big)
        small_rows = jnp.concatenate([w_in_t[l, src[n][0]:src[n][1]] for n in ("ik", "iw", "fa")], axis=0)
        small_rows = jnp.pad(small_rows, ((0, LANES - small_rows.shape[0]), (0, 0)))
        w_small = pl.pallas_call(
            _transpose_cast_kernel,
            in_specs=[pl.BlockSpec((1, LANES, d), lambda: (0, 0, 0))],
            out_specs=pl.BlockSpec((d, LANES), lambda: (0, 0)),
            out_shape=jax.ShapeDtypeStruct((d, LANES), BF16),
        )(small_rows[None])

        p = _inproj(hcur, w_big, tables, s, bn=bn, q_tiles=tiles_of("qa", "qb", "qc"),
                    rope128_tiles=tiles_of("qb", "kb", "qc", "kc"), rope64_tiles=tiles_of("iq"),
                    q_scale=HEAD_DIM ** -0.5 * LOG2E)
        small = _inproj_small(hcur, w_small, tables[2:], s)
        p3 = p.reshape(b, s, n_big)
        small3 = small.reshape(b, s, LANES)

        qx, kx = _forget_bias_columns(small3, b_forget[l], heads=fox_h, first_lane=IDX_HEAD_DIM + idx_h)
        o_a = _fox_attention(p3, qx, kx, heads=fox_h, q_off=off["qa"] // HEAD_DIM, k_off=off["ka"] // HEAD_DIM,
                             v_off=off["va"] // HEAD_DIM)

        ik = small3[:, :, :IDX_HEAD_DIM].astype(BF16)
        zeros = jnp.zeros_like(ik)
        ik_lo = jnp.concatenate([ik, zeros], axis=2)
        ik_hi = jnp.concatenate([zeros, ik], axis=2)
        o_b = _dsa_attention(p3, small3, ik_lo, ik_hi, heads=dsa_h, idx_heads=idx_h, iq_off=off["iq"] // iq_w,
                             q_off=off["qb"] // dsa_w, k_off=off["kb"] // dsa_w, v_off=off["vb"] // dsa_w, topk=topk)

        lam_init = 0.8 - 0.6 * math.exp(-0.3 * l)
        o_c = _diff_attention(p3, (lam_q1[l], lam_k1[l], lam_q2[l], lam_k2[l]), g_subln[l], heads=diff_h,
                              q_off=off["qc"], k_off=off["kc"], v_off=off["vc"], lam_init=lam_init)

        merged = _merge(o_a.reshape(m, fox_w), o_b.reshape(m, dsa_w), o_c.reshape(m, diffv_w),
                        _cast_weight(w_oa, l), _cast_weight(w_ob, l), _cast_weight(w_oc, l), p, b_gate[l],
                        gate_off=off["gl"], d=d)
        x2, hcur = _proj_norm_resid(merged, _cast_weight(w_out, l), g_mix_post[l], x2, g_ffn_pre[l])

        act = _gate_up_conv(hcur, _cast_weight(w_ffn_gate, l), _cast_weight(w_ffn_up, l),
                            conv_w[l].astype(F32), conv_b[l][None, :].astype(F32), s)
        w_down = _cast_weight(w_ffn_down, l)
        g_next = g_mix_pre[l + 1] if l + 1 < depth else g_mix_pre[0]
        x2, hcur = _proj_norm_resid(act, w_down, g_ffn_post[l], x2, g_next)
    return x2.reshape(b, s, d)
```

```python
import functools
import math

import jax
import jax.numpy as jnp
from jax import lax
from jax.experimental import pallas as pl
from jax.experimental.pallas import tpu as pltpu

CHUNK = 64
ROPE_THETA = 10000.0
NORM_EPS = 1e-6
HEAD_DIM = 128
IDX_HEAD_DIM = 64
DIFF_V_DIM = 256
INDEX_TOPK = 256
N_BRANCH = 3
CONV_WIDTH = 3

LANES = 128
VMEM_LIMIT_BYTES = 56 * 1024 * 1024

NORM_ROWS = 512
MATMUL_ROWS = 1024
INPROJ_COLS = 1024
GATE_UP_COLS = 512
MERGE_ROWS = 512
MERGE_COLS = 1024
PROJ_ROWS = 512
PROJ_K = 512
PROJ_COLS = 1024
EPILOGUE_ROWS = 64
ATTN_TILE = 1024
DSA_Q_TILE = 256
DSA_SCORE_TILE = 512
DSA_WIDE_TILE = 1024
FORGET_TILE = 512
MASK_VALUE = -1e30
INT32_MIN = -(2 ** 31)
LOG2E = math.log2(math.e)

F32 = jnp.float32
BF16 = jnp.bfloat16
_NT = (((1,), (1,)), ((), ()))


def _params(*semantics):
    return pltpu.CompilerParams(dimension_semantics=semantics, vmem_limit_bytes=VMEM_LIMIT_BYTES)


def _tile(dim, want):
    t = min(dim, want)
    assert dim % t == 0, (dim, want)
    return t


def _rmsnorm_kernel(x_ref, g_ref, o_ref):
    x = x_ref[...]
    ms = jnp.mean(x * x, axis=-1, keepdims=True)
    o_ref[...] = (x * lax.rsqrt(ms + NORM_EPS) * g_ref[...]).astype(o_ref.dtype)


def _rmsnorm(x, g):
    m, d = x.shape
    bm = _tile(m, NORM_ROWS)
    return pl.pallas_call(
        _rmsnorm_kernel,
        grid=(m // bm,),
        in_specs=[pl.BlockSpec((bm, d), lambda i: (i, 0)), pl.BlockSpec((1, d), lambda i: (0, 0))],
        out_specs=pl.BlockSpec((bm, d), lambda i: (i, 0)),
        out_shape=jax.ShapeDtypeStruct((m, d), BF16),
        compiler_params=_params("parallel"),
    )(x, g.reshape(1, d))


def _rope128(x, cos, sin):
    return x * cos + pltpu.roll(x, HEAD_DIM // 2, 1) * sin


def _rope64(x, cos, sin_lo, sin_hi):
    return x * cos + pltpu.roll(x, LANES - 32, 1) * sin_lo + pltpu.roll(x, 32, 1) * sin_hi


def _inproj_kernel(a_ref, w_ref, c128_ref, s128_ref, c64_ref, slo_ref, shi_ref, o_ref, *,
                   q_tiles, rope128_tiles, rope64_tiles, q_scale):
    j = pl.program_id(1)

    def member(tiles):
        hit = j < 0
        for t in tiles:
            hit = hit | (j == t)
        return hit

    acc = jnp.dot(a_ref[...], w_ref[...], preferred_element_type=F32)
    acc = acc * jnp.where(member(q_tiles), q_scale, 1.0).astype(F32)
    is128 = member(rope128_tiles)
    is64 = member(rope64_tiles)
    n_sub = acc.shape[1] // LANES
    o_ref[...] = acc.astype(o_ref.dtype)

    @pl.when(is128)
    def _():
        for c in range(n_sub):
            sl = slice(c * LANES, (c + 1) * LANES)
            o_ref[:, sl] = _rope128(acc[:, sl], c128_ref[...], s128_ref[...]).astype(o_ref.dtype)

    @pl.when(is64)
    def _():
        for c in range(n_sub):
            sl = slice(c * LANES, (c + 1) * LANES)
            o_ref[:, sl] = _rope64(acc[:, sl], c64_ref[...], slo_ref[...], shi_ref[...]).astype(o_ref.dtype)


def _inproj(h, w, tables, seq, *, bn, q_tiles, rope128_tiles, rope64_tiles, q_scale):
    m, k = h.shape
    n = w.shape[1]
    bm = _tile(seq, MATMUL_ROWS)
    pos_blocks = seq // bm
    tab_spec = pl.BlockSpec((bm, LANES), lambda i, j: (i % pos_blocks, 0))
    kern = functools.partial(_inproj_kernel, q_tiles=q_tiles, rope128_tiles=rope128_tiles,
                             rope64_tiles=rope64_tiles, q_scale=q_scale)
    return pl.pallas_call(
        kern,
        grid=(m // bm, n // bn),
        in_specs=[pl.BlockSpec((bm, k), lambda i, j: (i, 0)),
                  pl.BlockSpec((k, bn), lambda i, j: (0, j))] + [tab_spec] * 5,
        out_specs=pl.BlockSpec((bm, bn), lambda i, j: (i, j)),
        out_shape=jax.ShapeDtypeStruct((m, n), BF16),
        compiler_params=_params("parallel", "arbitrary"),
    )(h, w, *tables)


def _inproj_small_kernel(a_ref, w_ref, c64_ref, slo_ref, shi_ref, o_ref):
    acc = jnp.dot(a_ref[...], w_ref[...], preferred_element_type=F32)
    roped = _rope64(acc, c64_ref[...], slo_ref[...], shi_ref[...])
    lane = lax.broadcasted_iota(jnp.int32, acc.shape, 1)
    o_ref[...] = jnp.where(lane < IDX_HEAD_DIM, roped, acc)


def _inproj_small(h, w, tables64, seq):
    m, k = h.shape
    bm = _tile(seq, MATMUL_ROWS)
    pos_blocks = seq // bm
    tab_spec = pl.BlockSpec((bm, LANES), lambda i: (i % pos_blocks, 0))
    return pl.pallas_call(
        _inproj_small_kernel,
        grid=(m // bm,),
        in_specs=[pl.BlockSpec((bm, k), lambda i: (i, 0)),
                  pl.BlockSpec((k, LANES), lambda i: (0, 0))] + [tab_spec] * 3,
        out_specs=pl.BlockSpec((bm, LANES), lambda i: (i, 0)),
        out_shape=jax.ShapeDtypeStruct((m, LANES), F32),
        compiler_params=_params("parallel"),
    )(h, w, *tables64)


def _split3(x):
    hi = x.astype(BF16)
    rem = x - hi.astype(F32)
    mid = rem.astype(BF16)
    lo = (rem - mid.astype(F32)).astype(BF16)
    return hi, mid, lo


def _forget_bias_kernel(x_ref, b_ref, qx_ref, kx_ref, carry_ref, *, heads, first_lane):
    @pl.when(pl.program_id(1) == 0)
    def _():
        carry_ref[...] = jnp.zeros_like(carry_ref)

    x = x_ref[0] + b_ref[...]
    lf = jnp.minimum(x, 0.0) - jnp.log1p(jnp.exp(-jnp.abs(x)))
    t = lf.shape[0]
    tri = (lax.broadcasted_iota(jnp.int32, (t, t), 0) >= lax.broadcasted_iota(jnp.int32, (t, t), 1)).astype(BF16)
    loc = sum(jnp.dot(tri, piece, preferred_element_type=F32) for piece in _split3(lf))
    pieces = _split3((loc + carry_ref[...]) * LOG2E)
    carry_ref[...] = carry_ref[...] + loc[t - 1:t, :]

    row = lax.broadcasted_iota(jnp.int32, (LANES, LANES), 0)
    col = lax.broadcasted_iota(jnp.int32, (LANES, LANES), 1)
    lane = lax.broadcasted_iota(jnp.int32, (t, LANES), 1)
    n = len(pieces)
    for h in range(heads):
        def placed(first_col):
            return sum(jnp.dot(piece, ((row == first_lane + h) & (col == first_col + k)).astype(BF16),
                               preferred_element_type=F32) for k, piece in enumerate(pieces))
        qx_ref[0, h] = jnp.where((lane >= n) & (lane < 2 * n), 1.0, placed(0)).astype(qx_ref.dtype)
        kx_ref[0, h] = jnp.where(lane < n, 1.0, -placed(n)).astype(kx_ref.dtype)


def _forget_bias_columns(small3, b_forget, *, heads, first_lane):
    b, s, _ = small3.shape
    t = _tile(s, FORGET_TILE)
    bias = jnp.zeros((1, LANES), F32).at[0, first_lane:first_lane + heads].set(b_forget.astype(F32))
    out_spec = pl.BlockSpec((1, heads, t, LANES), lambda bi, i: (bi, 0, i, 0))
    return pl.pallas_call(
        functools.partial(_forget_bias_kernel, heads=heads, first_lane=first_lane),
        grid=(b, s // t),
        in_specs=[pl.BlockSpec((1, t, LANES), lambda bi, i: (bi, i, 0)), pl.BlockSpec((1, LANES), lambda bi, i: (0, 0))],
        out_specs=[out_spec] * 2,
        out_shape=[jax.ShapeDtypeStruct((b, heads, s, LANES), BF16)] * 2,
        scratch_shapes=[pltpu.VMEM((1, LANES), F32)],
        compiler_params=_params("parallel", "arbitrary"),
    )(small3, bias)


def _dot_row_halves(p, v):
    h = p.shape[0] // 2
    return jnp.concatenate([jnp.dot(p[:h], v, preferred_element_type=F32),
                            jnp.dot(p[h:], v, preferred_element_type=F32)], axis=0)


def _online_update(s, v, m, l, acc):
    m_new = jnp.maximum(m, jnp.max(s, axis=1, keepdims=True))
    alpha = jnp.exp2(m - m_new)
    p = jnp.exp2(s - m_new)
    l_new = alpha * l + jnp.sum(p, axis=1, keepdims=True)
    acc_new = alpha * acc + _dot_row_halves(p.astype(v.dtype), v)
    return m_new, l_new, acc_new


def _softmax_init(tq, dv):
    return (jnp.full((tq, 1), MASK_VALUE, F32), jnp.zeros((tq, 1), F32), jnp.zeros((tq, dv), F32))


def _fox_kernel(q_ref, qx_ref, k_ref, kx_ref, v_ref, o_ref, *, tq, tk, group):
    i = pl.program_id(2)
    n_full = (i * tq) // tk
    ones = jnp.ones((tk, HEAD_DIM), v_ref.dtype)

    def tile(j, carry, diagonal):
        off = pl.multiple_of(j * tk, tk)
        out = []
        for g in range(group):
            hs = slice(g * HEAD_DIM, (g + 1) * HEAD_DIM)
            m, l, acc = carry[g]
            q = jnp.concatenate([q_ref[0, :, hs], qx_ref[0, g]], axis=1)
            k = jnp.concatenate([k_ref[0, pl.ds(off, tk), hs], kx_ref[0, g, pl.ds(off, tk), :]], axis=1)
            s = lax.dot_general(q, k, _NT, preferred_element_type=F32)
            if diagonal:
                row = i * tq + lax.broadcasted_iota(jnp.int32, s.shape, 0)
                col = off + lax.broadcasted_iota(jnp.int32, s.shape, 1)
                s = jnp.where(row >= col, s, MASK_VALUE)
            m_new = jnp.maximum(m, jnp.max(s, axis=1, keepdims=True))
            alpha = jnp.exp2(m - m_new)
            p = jnp.exp2(s - m_new).astype(v_ref.dtype)
            v = jnp.concatenate([v_ref[0, pl.ds(off, tk), hs], ones], axis=1)
            pv = _dot_row_halves(p, v)
            out.append((m_new, alpha * l + pv[:, HEAD_DIM:], alpha * acc + pv[:, :HEAD_DIM]))
        return tuple(out)

    init = tuple((jnp.full((tq, 1), MASK_VALUE, F32), jnp.zeros((tq, HEAD_DIM), F32), jnp.zeros((tq, HEAD_DIM), F32))
                 for _ in range(group))
    carry = lax.fori_loop(0, n_full, lambda j, c: tile(j, c, False), init)
    final = tile(n_full, carry, True)
    for g in range(group):
        _, l, acc = final[g]
        o_ref[0, :, g * HEAD_DIM:(g + 1) * HEAD_DIM] = (acc / l).astype(o_ref.dtype)


def _fox_attention(p3, qx, kx, *, heads, q_off, k_off, v_off):
    b, s, _ = p3.shape
    tq = _tile(s, ATTN_TILE)
    tk = _tile(s, ATTN_TILE)
    group = 1
    gw = group * HEAD_DIM
    assert q_off % group == 0 and k_off % group == 0 and v_off % group == 0
    return pl.pallas_call(
        functools.partial(_fox_kernel, tq=tq, tk=tk, group=group),
        grid=(b, heads // group, s // tq),
        in_specs=[pl.BlockSpec((1, tq, gw), lambda bi, h, i: (bi, i, q_off // group + h)),
                  pl.BlockSpec((1, group, tq, HEAD_DIM), lambda bi, h, i: (bi, h, i, 0)),
                  pl.BlockSpec((1, s, gw), lambda bi, h, i: (bi, 0, k_off // group + h)),
                  pl.BlockSpec((1, group, s, HEAD_DIM), lambda bi, h, i: (bi, h, 0, 0)),
                  pl.BlockSpec((1, s, gw), lambda bi, h, i: (bi, 0, v_off // group + h))],
        out_specs=pl.BlockSpec((1, tq, gw), lambda bi, h, i: (bi, i, h)),
        out_shape=jax.ShapeDtypeStruct((b, s, heads * HEAD_DIM), BF16),
        compiler_params=_params("parallel", "parallel", "arbitrary"),
    )(p3, qx, p3, kx, p3)


def _diff_kernel(q_ref, k_ref, v_ref, lq1_ref, lk1_ref, lq2_ref, lk2_ref, g_ref, o_ref, *, tq, tk, group, lam_init):
    i = pl.program_id(2)
    n_full = (i * tq) // tk

    def tile(j, carry, diagonal):
        off = pl.multiple_of(j * tk, tk)
        if diagonal:
            row = (i * tq + lax.broadcasted_iota(jnp.int32, (tq, tk), 0)) // CHUNK
            col = (off + lax.broadcasted_iota(jnp.int32, (tq, tk), 1)) // CHUNK
            keep = row >= col
        out = []
        for g in range(group):
            v = v_ref[0, pl.ds(off, tk), g * DIFF_V_DIM:(g + 1) * DIFF_V_DIM]
            for half in range(2):
                hs = slice((2 * g + half) * HEAD_DIM, (2 * g + half + 1) * HEAD_DIM)
                s = lax.dot_general(q_ref[0, :, hs], k_ref[0, pl.ds(off, tk), hs], _NT, preferred_element_type=F32)
                if diagonal:
                    s = jnp.where(keep, s, MASK_VALUE)
                out.append(_online_update(s, v, *carry[2 * g + half]))
        return tuple(out)

    init = tuple(_softmax_init(tq, DIFF_V_DIM) for _ in range(2 * group))
    carry = lax.fori_loop(0, n_full, lambda j, c: tile(j, c, False), init)
    final = tile(n_full, carry, True)

    lam = (jnp.exp(jnp.sum(lq1_ref[...] * lk1_ref[...], axis=1, keepdims=True))
           - jnp.exp(jnp.sum(lq2_ref[...] * lk2_ref[...], axis=1, keepdims=True)) + lam_init)
    for g in range(group):
        (_, l1, a1), (_, l2, a2) = final[2 * g], final[2 * g + 1]
        o = a1 / l1 - lam * (a2 / l2)
        ms = jnp.mean(o * o, axis=-1, keepdims=True)
        o = o * lax.rsqrt(ms + NORM_EPS) * g_ref[...]
        o_ref[0, :, g * DIFF_V_DIM:(g + 1) * DIFF_V_DIM] = (o * (1.0 - lam_init)).astype(o_ref.dtype)


def _diff_attention(p3, lam_vecs, g_subln, *, heads, q_off, k_off, v_off, lam_init):
    b, s, _ = p3.shape
    tq = _tile(s, ATTN_TILE)
    tk = _tile(s, ATTN_TILE)
    group = 1
    gw = group * DIFF_V_DIM
    assert q_off % gw == 0 and k_off % gw == 0 and v_off % gw == 0
    vec_spec = pl.BlockSpec((1, HEAD_DIM), lambda bi, h, i: (0, 0))
    return pl.pallas_call(
        functools.partial(_diff_kernel, tq=tq, tk=tk, group=group, lam_init=lam_init),
        grid=(b, heads // group, s // tq),
        in_specs=[pl.BlockSpec((1, tq, gw), lambda bi, h, i: (bi, i, q_off // gw + h)),
                  pl.BlockSpec((1, s, gw), lambda bi, h, i: (bi, 0, k_off // gw + h)),
                  pl.BlockSpec((1, s, gw), lambda bi, h, i: (bi, 0, v_off // gw + h)),
                  vec_spec, vec_spec, vec_spec, vec_spec,
                  pl.BlockSpec((1, DIFF_V_DIM), lambda bi, h, i: (0, 0))],
        out_specs=pl.BlockSpec((1, tq, gw), lambda bi, h, i: (bi, i, h)),
        out_shape=jax.ShapeDtypeStruct((b, s, heads * DIFF_V_DIM), BF16),
        compiler_params=_params("parallel", "parallel", "arbitrary"),
    )(p3, p3, p3, *[v.reshape(1, HEAD_DIM).astype(F32) for v in lam_vecs],
      g_subln.reshape(1, DIFF_V_DIM).astype(F32))


def _ordered_key(x):
    bits = pltpu.bitcast(x, jnp.int32)
    return bits ^ (lax.shift_right_arithmetic(bits, 31) & 0x7FFFFFFF)


def _dsa_kernel(iq_ref, iw_ref, iklo_ref, ikhi_ref, q_ref, k_ref, v_ref, o_ref, keys_ref, *,
                tq, ts, tw, idx_heads, heads, topk, idx_scale):
    i = pl.program_id(1)
    n_score = (i * tq) // ts + 1
    n_wide = (i * tq) // tw + 1
    iw = iw_ref[0][:, IDX_HEAD_DIM:IDX_HEAD_DIM + idx_heads] * idx_scale
    row_chunk = (i * tq + lax.broadcasted_iota(jnp.int32, (tq, ts), 0)) // CHUNK
    col_iota = lax.broadcasted_iota(jnp.int32, (tq, ts), 1)

    def score_tile(j, _):
        off = pl.multiple_of(j * ts, ts)
        ik_lo = iklo_ref[0, pl.ds(off, ts), :]
        ik_hi = ikhi_ref[0, pl.ds(off, ts), :]
        sc = jnp.zeros((tq, ts), F32)
        for p in range(idx_heads // 2):
            a = iq_ref[0, :, p * LANES:(p + 1) * LANES]
            even = lax.dot_general(a, ik_lo, _NT, preferred_element_type=F32)
            odd = lax.dot_general(a, ik_hi, _NT, preferred_element_type=F32)
            sc = sc + jnp.maximum(even, 0.0) * iw[:, 2 * p:2 * p + 1]
            sc = sc + jnp.maximum(odd, 0.0) * iw[:, 2 * p + 1:2 * p + 2]
        valid = row_chunk >= (off + col_iota) // CHUNK
        keys_ref[:, pl.ds(off, ts)] = jnp.where(valid, _ordered_key(sc), INT32_MIN)
        return 0

    lax.fori_loop(0, n_score, score_tile, 0)

    def fill_tile(j, _):
        keys_ref[:, pl.ds(pl.multiple_of(j * ts, ts), ts)] = jnp.full((tq, ts), INT32_MIN, jnp.int32)
        return 0

    lax.fori_loop(n_score, n_wide * (tw // ts), fill_tile, 0)

    def count_ge(cand):
        def body(j, part):
            off = pl.multiple_of(j * tw, tw)
            ge = (keys_ref[:, pl.ds(off, tw)] >= cand).astype(jnp.int32)
            for c in range(tw // LANES):
                part = part + ge[:, c * LANES:(c + 1) * LANES]
            return part
        part = lax.fori_loop(0, n_wide, body, jnp.zeros((tq, LANES), jnp.int32))
        return jnp.sum(part, axis=1, keepdims=True)

    n_ge_zero = count_ge(jnp.zeros((tq, 1), jnp.int32))
    thr = jnp.where(n_ge_zero >= topk, 0, INT32_MIN).astype(jnp.int32)

    def bit_step(t, carry):
        thr, n_selected = carry
        cand = thr + lax.shift_left(jnp.int32(1), 30 - t)
        n_cand = count_ge(cand)
        accept = n_cand >= topk
        return jnp.where(accept, cand, thr), jnp.where(accept, n_cand, n_selected)

    thr, n_selected = lax.fori_loop(0, 31, bit_step, (thr, n_ge_zero))

    has_tie = (n_selected > topk) & (thr > INT32_MIN)
    thr = jnp.maximum(thr, INT32_MIN + 1)

    @pl.when(jnp.max(has_tie.astype(jnp.int32)) > 0)
    def _():
        need = topk - count_ge(thr + 1)
        lane_idx = lax.broadcasted_iota(jnp.int32, (tq, tw), 1)

        def ties_before(bound):
            def body(j, part):
                off = pl.multiple_of(j * tw, tw)
                hit = (keys_ref[:, pl.ds(off, tw)] == thr) & (off + lane_idx < bound)
                hit = hit.astype(jnp.int32)
                for c in range(tw // LANES):
                    part = part + hit[:, c * LANES:(c + 1) * LANES]
                return part
            part = lax.fori_loop(0, n_wide, body, jnp.zeros((tq, LANES), jnp.int32))
            return jnp.sum(part, axis=1, keepdims=True)

        index_bits = keys_ref.shape[1].bit_length()

        def index_step(t, last):
            cand = last + lax.shift_left(jnp.int32(1), index_bits - 1 - t)
            return jnp.where(ties_before(cand) < need, cand, last)

        last = lax.fori_loop(0, index_bits, index_step, jnp.zeros((tq, 1), jnp.int32))

        def demote(j, _):
            off = pl.multiple_of(j * tw, tw)
            keys = keys_ref[:, pl.ds(off, tw)]
            drop = has_tie & (keys == thr) & (off + lane_idx > last)
            keys_ref[:, pl.ds(off, tw)] = jnp.where(drop, INT32_MIN, keys)
            return 0

        lax.fori_loop(0, n_wide, demote, 0)

    ones = jnp.ones((tw, HEAD_DIM), v_ref.dtype)

    def attend(j, carry):
        off = pl.multiple_of(j * tw, tw)
        out = []
        for h in range(heads):
            hs = slice(h * HEAD_DIM, (h + 1) * HEAD_DIM)
            m, l, acc = carry[h]
            s = lax.dot_general(q_ref[0, :, hs], k_ref[0, pl.ds(off, tw), hs], _NT, preferred_element_type=F32)
            s = jnp.where(keys_ref[:, pl.ds(off, tw)] >= thr, s, MASK_VALUE)
            m_new = jnp.maximum(m, jnp.max(s, axis=1, keepdims=True))
            alpha = jnp.exp2(m - m_new)
            p = jnp.exp2(s - m_new).astype(v_ref.dtype)
            v = jnp.concatenate([v_ref[0, pl.ds(off, tw), hs], ones], axis=1)
            pv = jnp.dot(p, v, preferred_element_type=F32)
            out.append((m_new, alpha * l + pv[:, HEAD_DIM:], alpha * acc + pv[:, :HEAD_DIM]))
        return tuple(out)

    init = tuple((jnp.full((tq, 1), MASK_VALUE, F32), jnp.zeros((tq, HEAD_DIM), F32), jnp.zeros((tq, HEAD_DIM), F32))
                 for _ in range(heads))
    final = lax.fori_loop(0, n_wide, attend, init)
    for h in range(heads):
        _, l, acc = final[h]
        o_ref[0, :, h * HEAD_DIM:(h + 1) * HEAD_DIM] = (acc / l).astype(o_ref.dtype)


def _dsa_attention(p3, small3, ik_lo, ik_hi, *, heads, idx_heads, iq_off, q_off, k_off, v_off, topk):
    b, s, _ = p3.shape
    tq = _tile(s, DSA_Q_TILE)
    ts = _tile(s, DSA_SCORE_TILE)
    tw = _tile(s, DSA_WIDE_TILE)
    width = heads * HEAD_DIM
    iq_width = idx_heads * IDX_HEAD_DIM
    resident = dict(pipeline_mode=pl.Buffered(1))
    kern = functools.partial(_dsa_kernel, tq=tq, ts=ts, tw=tw, idx_heads=idx_heads, heads=heads, topk=topk,
                             idx_scale=(IDX_HEAD_DIM ** -0.5) * (idx_heads ** -0.5))
    return pl.pallas_call(
        kern,
        grid=(b, s // tq),
        in_specs=[pl.BlockSpec((1, tq, iq_width), lambda bi, i: (bi, i, iq_off)),
                  pl.BlockSpec((1, tq, LANES), lambda bi, i: (bi, i, 0)),
                  pl.BlockSpec((1, s, LANES), lambda bi, i: (bi, 0, 0), **resident),
                  pl.BlockSpec((1, s, LANES), lambda bi, i: (bi, 0, 0), **resident),
                  pl.BlockSpec((1, tq, width), lambda bi, i: (bi, i, q_off)),
                  pl.BlockSpec((1, s, width), lambda bi, i: (bi, 0, k_off), **resident),
                  pl.BlockSpec((1, s, width), lambda bi, i: (bi, 0, v_off), **resident)],
        out_specs=pl.BlockSpec((1, tq, width), lambda bi, i: (bi, i, 0)),
        out_shape=jax.ShapeDtypeStruct((b, s, width), BF16),
        scratch_shapes=[pltpu.VMEM((tq, s), jnp.int32)],
        compiler_params=_params("parallel", "arbitrary"),
    )(p3, small3, ik_lo, ik_hi, p3, p3, p3)


def _merge_kernel(oa_ref, ob_ref, oc_ref, wa_ref, wb_ref, wc_ref, g0_ref, g1_ref, g2_ref, bg_ref, o_ref):
    def branch(o_r, w_r, g_r, n):
        y = jnp.dot(o_r[...], w_r[...], preferred_element_type=F32)
        return jax.nn.sigmoid(g_r[...].astype(F32) + bg_ref[n:n + 1, :]) * y

    out = branch(oa_ref, wa_ref, g0_ref, 0) + branch(ob_ref, wb_ref, g1_ref, 1) + branch(oc_ref, wc_ref, g2_ref, 2)
    o_ref[...] = out.astype(o_ref.dtype)


def _merge(o_a, o_b, o_c, w_a, w_b, w_c, p, b_gate, *, gate_off, d):
    m = o_a.shape[0]
    bm = _tile(m, MERGE_ROWS)
    bn = _tile(d, MERGE_COLS)
    nj = d // bn

    def lhs(o):
        return pl.BlockSpec((bm, o.shape[1]), lambda i, j: (i, 0))

    def rhs(w):
        return pl.BlockSpec((w.shape[0], bn), lambda i, j: (0, j))

    def gate(n):
        return pl.BlockSpec((bm, bn), lambda i, j: (i, gate_off // bn + n * nj + j))

    assert gate_off % bn == 0
    return pl.pallas_call(
        _merge_kernel,
        grid=(m // bm, nj),
        in_specs=[lhs(o_a), lhs(o_b), lhs(o_c), rhs(w_a), rhs(w_b), rhs(w_c), gate(0), gate(1), gate(2),
                  pl.BlockSpec((N_BRANCH, bn), lambda i, j: (0, j))],
        out_specs=pl.BlockSpec((bm, bn), lambda i, j: (i, j)),
        out_shape=jax.ShapeDtypeStruct((m, d), BF16),
        compiler_params=_params("parallel", "arbitrary"),
    )(o_a, o_b, o_c, w_a, w_b, w_c, p, p, p, b_gate.astype(F32))


def _proj_norm_resid_kernel(a_ref, w_ref, g_ref, r_ref, gn_ref, x_ref, h_ref, *, n_k, bn, k_tail):
    kk = pl.program_id(1)
    n = x_ref.shape[1]

    @pl.when(kk == 0)
    def _():
        x_ref[...] = jnp.zeros_like(x_ref)

    def accumulate(tail):
        a = a_ref[...]
        if tail:
            col = lax.broadcasted_iota(jnp.int32, a.shape, 1)
            a = jnp.where(col < k_tail, a.astype(F32), 0.0).astype(a.dtype)
        for c in range(n // bn):
            sl = slice(c * bn, (c + 1) * bn)
            w = w_ref[:, sl]
            if tail:
                row = lax.broadcasted_iota(jnp.int32, w.shape, 0)
                w = jnp.where(row < k_tail, w.astype(F32), 0.0).astype(w.dtype)
            x_ref[:, sl] += jnp.dot(a, w, preferred_element_type=F32)

    if k_tail == 0:
        accumulate(False)
    else:
        pl.when(kk < n_k - 1)(lambda: accumulate(False))
        pl.when(kk == n_k - 1)(lambda: accumulate(True))

    @pl.when(kk == n_k - 1)
    def _():
        rows = min(x_ref.shape[0], EPILOGUE_ROWS)

        def chunk(c, _):
            rs = pl.ds(pl.multiple_of(c * rows, rows), rows)
            y = x_ref[rs, :]
            ms = jnp.mean(y * y, axis=-1, keepdims=True)
            x_new = r_ref[rs, :] + y * lax.rsqrt(ms + NORM_EPS) * g_ref[...]
            x_ref[rs, :] = x_new
            ms2 = jnp.mean(x_new * x_new, axis=-1, keepdims=True)
            h_ref[rs, :] = (x_new * lax.rsqrt(ms2 + NORM_EPS) * gn_ref[...]).astype(h_ref.dtype)
            return 0

        lax.fori_loop(0, x_ref.shape[0] // rows, chunk, 0)


def _proj_norm_resid(a, w, g_post, resid, g_next):
    m, k = a.shape
    n = w.shape[1]
    bm = _tile(m, PROJ_ROWS)
    bk = min(k, PROJ_K)
    n_k = pl.cdiv(k, bk)
    vec = pl.BlockSpec((1, n), lambda i, kk: (0, 0))
    row = pl.BlockSpec((bm, n), lambda i, kk: (i, 0))
    return pl.pallas_call(
        functools.partial(_proj_norm_resid_kernel, n_k=n_k, bn=_tile(n, PROJ_COLS), k_tail=k % bk),
        grid=(m // bm, n_k),
        in_specs=[pl.BlockSpec((bm, bk), lambda i, kk: (i, kk)),
                  pl.BlockSpec((bk, n), lambda i, kk: (kk, 0)),
                  vec, row, vec],
        out_specs=[row, row],
        out_shape=[jax.ShapeDtypeStruct((m, n), F32), jax.ShapeDtypeStruct((m, n), BF16)],
        compiler_params=_params("parallel", "arbitrary"),
    )(a, w, g_post.reshape(1, n).astype(F32), resid, g_next.reshape(1, n).astype(F32))


def _gate_up_conv_kernel(a_ref, wg_ref, wu_ref, cw_ref, cb_ref, o_ref, prev_ref, *, blocks_per_seq):
    i = pl.program_id(0)
    j = pl.program_id(1)
    @pl.when((i % blocks_per_seq) == 0)
    def _():
        prev_ref[j] = jnp.zeros(prev_ref.shape[1:], F32)

    a = a_ref[...]
    bm = a.shape[0]
    z = jnp.dot(a, wg_ref[...], preferred_element_type=F32)
    u = jnp.dot(a, wu_ref[...], preferred_element_type=F32)
    prev2 = prev_ref[j, 0:1, :]
    prev1 = prev_ref[j, 1:2, :]
    prev_ref[j, 0:2, :] = z[bm - 2:bm, :]
    row = lax.broadcasted_iota(jnp.int32, z.shape, 0)
    z1 = jnp.where(row == 0, prev1, pltpu.roll(z, 1, 0))
    z2 = jnp.where(row == 0, prev2, jnp.where(row == 1, prev1, pltpu.roll(z, 2, 0)))
    zc = cw_ref[0:1, :] * z2 + cw_ref[1:2, :] * z1 + cw_ref[2:3, :] * z + cb_ref[...]
    gelu = 0.5 * zc * (1.0 + jnp.tanh(math.sqrt(2.0 / math.pi) * (zc + 0.044715 * (zc * zc * zc))))
    o_ref[...] = (gelu * u).astype(o_ref.dtype)


def _gate_up_conv(h, w_gate, w_up, conv_w, conv_b, seq):
    m, k = h.shape
    n = w_gate.shape[1]
    bm = _tile(seq, MATMUL_ROWS)
    bn = min(n, GATE_UP_COLS)
    n_j = pl.cdiv(n, bn)
    wspec = pl.BlockSpec((k, bn), lambda i, j: (0, j))
    return pl.pallas_call(
        functools.partial(_gate_up_conv_kernel, blocks_per_seq=seq // bm),
        grid=(m // bm, n_j),
        in_specs=[pl.BlockSpec((bm, k), lambda i, j: (i, 0)), wspec, wspec,
                  pl.BlockSpec((CONV_WIDTH, bn), lambda i, j: (0, j)),
                  pl.BlockSpec((1, bn), lambda i, j: (0, j))],
        out_specs=pl.BlockSpec((bm, bn), lambda i, j: (i, j)),
        out_shape=jax.ShapeDtypeStruct((m, n), BF16),
        scratch_shapes=[pltpu.VMEM((n_j, 8, bn), F32)],
        compiler_params=_params("arbitrary", "arbitrary"),
    )(h, w_gate, w_up, conv_w, conv_b)


def _rope_tables(s):
    pos = jnp.arange(s, dtype=F32)

    def cos_sin(d):
        inv_freq = ROPE_THETA ** (-jnp.arange(0, d, 2, dtype=F32) / d)
        ang = pos[:, None] * inv_freq[None, :]
        return jnp.cos(ang), jnp.sin(ang)

    c, sn = cos_sin(HEAD_DIM)
    c128 = jnp.concatenate([c, c], axis=1)
    s128 = jnp.concatenate([-sn, sn], axis=1)
    ci, si = cos_sin(IDX_HEAD_DIM)
    zero = jnp.zeros_like(si)
    c64 = jnp.concatenate([ci, ci, ci, ci], axis=1)
    s_lo = jnp.concatenate([-si, zero, -si, zero], axis=1)
    s_hi = jnp.concatenate([zero, si, zero, si], axis=1)
    return c128, s128, c64, s_lo, s_hi


WEIGHT_BLOCK_BYTES = 4 * 1024 * 1024


def _row_block(rows, row_bytes):
    br = 16
    while br * 2 <= rows and rows % (br * 2) == 0 and br * 2 * row_bytes <= WEIGHT_BLOCK_BYTES:
        br *= 2
    return min(br, rows)


def _cast_kernel(x_ref, o_ref):
    o_ref[...] = x_ref[0].astype(o_ref.dtype)


def _cast_weight(w, layer):
    _, k, n = w.shape
    br = _row_block(k, n * 4)
    return pl.pallas_call(
        _cast_kernel,
        grid=(k // br,),
        in_specs=[pl.BlockSpec((1, br, n), lambda i: (layer, i, 0))],
        out_specs=pl.BlockSpec((br, n), lambda i: (i, 0)),
        out_shape=jax.ShapeDtypeStruct((k, n), BF16),
        compiler_params=_params("parallel"),
    )(w)


F32_SUBLANES = 8


REGROUP_COLS = 256


def _transpose_cast_kernel(x_ref, o_ref):
    o_ref[...] = x_ref[0].T.astype(o_ref.dtype)


def _regroup_in_weight(w_in_t, layer, segments, n_big):
    _, _, k = w_in_t.shape
    rb = math.gcd(REGROUP_COLS, *[width for _, _, width in segments], *[dst for _, dst, _ in segments])
    assert sum(width for _, _, width in segments) == n_big and segments[0][1] == 0
    assert all(src % F32_SUBLANES == 0 for src, _, _ in segments), "segment starts must sit on f32 tile rows"

    def source_row(i):
        row = i * rb
        shift = 0
        for src, dst, _ in segments:
            row = row + jnp.where(i >= dst // rb, (src - dst) - shift, 0)
            shift = src - dst
        return pl.multiple_of(row, F32_SUBLANES)

    return pl.pallas_call(
        _transpose_cast_kernel,
        grid=(n_big // rb,),
        in_specs=[pl.BlockSpec((pl.Element(1), pl.Element(rb), pl.Element(k)),
                               lambda i: (layer, source_row(i), 0))],
        out_specs=pl.BlockSpec((k, rb), lambda i: (0, i)),
        out_shape=jax.ShapeDtypeStruct((k, n_big), BF16),
        compiler_params=_params("parallel"),
    )(w_in_t)


def kernel(x, g_mix_pre, g_mix_post, w_in, b_forget, b_gate, lam_q1, lam_k1, lam_q2, lam_k2, g_subln, w_oa, w_ob,
           w_oc, w_out, g_ffn_pre, g_ffn_post, w_ffn_gate, w_ffn_up, conv_w, conv_b, w_ffn_down):
    b, s, d = x.shape
    depth = w_in.shape[0]
    m = b * s
    fox_w, dsa_w, diffv_w = w_oa.shape[1], w_ob.shape[1], w_oc.shape[1]
    fox_h, dsa_h, diff_h = fox_w // HEAD_DIM, dsa_w // HEAD_DIM, diffv_w // DIFF_V_DIM
    diffqk_w = diff_h * 2 * HEAD_DIM
    known = 3 * fox_w + fox_h + 3 * dsa_w + IDX_HEAD_DIM + 2 * diffqk_w + diffv_w + N_BRANCH * d
    idx_h = (w_in.shape[2] - known) // (IDX_HEAD_DIM + 1)
    iq_w = idx_h * IDX_HEAD_DIM
    topk = min(INDEX_TOPK, s // 4)
    assert known + idx_h * (IDX_HEAD_DIM + 1) == w_in.shape[2]
    assert idx_h % 2 == 0 and IDX_HEAD_DIM + idx_h + fox_h <= LANES and s % CHUNK == 0

    sizes = dict(qa=fox_w, ka=fox_w, va=fox_w, fa=fox_h, qb=dsa_w, kb=dsa_w, vb=dsa_w, iq=iq_w, ik=IDX_HEAD_DIM,
                 iw=idx_h, qc=diffqk_w, kc=diffqk_w, vc=diffv_w, gl=N_BRANCH * d)
    src, pos = {}, 0
    for name, width in sizes.items():
        src[name] = (pos, pos + width)
        pos += width
    big_order = ("qa", "ka", "va", "qb", "kb", "vb", "iq", "qc", "kc", "vc", "gl")
    off, pos = {}, 0
    for name in big_order:
        off[name] = pos
        pos += sizes[name]
    n_big = pos
    bn = math.gcd(INPROJ_COLS, *[sizes[name] for name in big_order])
    assert bn % (2 * LANES) == 0
    assert off["iq"] % iq_w == 0 and off["qb"] % dsa_w == 0
    big_segments = []
    for name in big_order:
        if big_segments and (big_segments[-1][0] + big_segments[-1][2] == src[name][0]
                             and big_segments[-1][1] + big_segments[-1][2] == off[name]):
            big_segments[-1] = (big_segments[-1][0], big_segments[-1][1], big_segments[-1][2] + sizes[name])
        else:
            big_segments.append((src[name][0], off[name], sizes[name]))

    def tiles_of(*names):
        return tuple(t for name in names for t in range(off[name] // bn, (off[name] + sizes[name]) // bn))

    tables = _rope_tables(s)

    w_in_t = jnp.swapaxes(w_in, 1, 2)
    x2 = x.reshape(m, d)
    hcur = _rmsnorm(x2, g_mix_pre[0])
    for l in range(depth):
        w_big = _regroup_in_weight(w_in_t, l, big_segments, n_big)
        small_rows = jnp.concatenate([w_in_t[l, src[n][0]:src[n][1]] for n in ("ik", "iw", "fa")], axis=0)
        small_rows = jnp.pad(small_rows, ((0, LANES - small_rows.shape[0]), (0, 0)))
        w_small = pl.pallas_call(
            _transpose_cast_kernel,
            in_specs=[pl.BlockSpec((1, LANES, d), lambda: (0, 0, 0))],
            out_specs=pl.BlockSpec((d, LANES), lambda: (0, 0)),
            out_shape=jax.ShapeDtypeStruct((d, LANES), BF16),
        )(small_rows[None])

        p = _inproj(hcur, w_big, tables, s, bn=bn, q_tiles=tiles_of("qa", "qb", "qc"),
                    rope128_tiles=tiles_of("qb", "kb", "qc", "kc"), rope64_tiles=tiles_of("iq"),
                    q_scale=HEAD_DIM ** -0.5 * LOG2E)
        small = _inproj_small(hcur, w_small, tables[2:], s)
        p3 = p.reshape(b, s, n_big)
        small3 = small.reshape(b, s, LANES)

        qx, kx = _forget_bias_columns(small3, b_forget[l], heads=fox_h, first_lane=IDX_HEAD_DIM + idx_h)
        o_a = _fox_attention(p3, qx, kx, heads=fox_h, q_off=off["qa"] // HEAD_DIM, k_off=off["ka"] // HEAD_DIM,
                             v_off=off["va"] // HEAD_DIM)

        ik = small3[:, :, :IDX_HEAD_DIM].astype(BF16)
        zeros = jnp.zeros_like(ik)
        ik_lo = jnp.concatenate([ik, zeros], axis=2)
        ik_hi = jnp.concatenate([zeros, ik], axis=2)
        o_b = _dsa_attention(p3, small3, ik_lo, ik_hi, heads=dsa_h, idx_heads=idx_h, iq_off=off["iq"] // iq_w,
                             q_off=off["qb"] // dsa_w, k_off=off["kb"] // dsa_w, v_off=off["vb"] // dsa_w, topk=topk)

        lam_init = 0.8 - 0.6 * math.exp(-0.3 * l)
        o_c = _diff_attention(p3, (lam_q1[l], lam_k1[l], lam_q2[l], lam_k2[l]), g_subln[l], heads=diff_h,
                              q_off=off["qc"], k_off=off["kc"], v_off=off["vc"], lam_init=lam_init)

        merged = _merge(o_a.reshape(m, fox_w), o_b.reshape(m, dsa_w), o_c.reshape(m, diffv_w),
                        _cast_weight(w_oa, l), _cast_weight(w_ob, l), _cast_weight(w_oc, l), p, b_gate[l],
                        gate_off=off["gl"], d=d)
        x2, hcur = _proj_norm_resid(merged, _cast_weight(w_out, l), g_mix_post[l], x2, g_ffn_pre[l])

        act = _gate_up_conv(hcur, _cast_weight(w_ffn_gate, l), _cast_weight(w_ffn_up, l),
                            conv_w[l].astype(F32), conv_b[l][None, :].astype(F32), s)
        w_down = _cast_weight(w_ffn_down, l)
        g_next = g_mix_pre[l + 1] if l + 1 < depth else g_mix_pre[0]
        x2, hcur = _proj_norm_resid(act, w_down, g_ffn_post[l], x2, g_next)
    return x2.reshape(b, s, d)
```
